```python
import math
import jax
import jax.numpy as jnp
from jax import lax
import numpy as np

D_MODEL = 1024
BATCH = 4
SEQ = 4096
DEPTH = 1
DEC_BATCH = 16
DEC_SEQ = 16
PAST_LEN = 2048

CHUNK = 64
Q_BLOCK = 128
N_HEADS = 8
QK_HEAD_DIM = 64
V_HEAD_DIM = 2 * QK_HEAD_DIM
ATTN_WIDTH = N_HEADS * V_HEAD_DIM
D_RNN = 1024
N_RNN_BLOCKS = 8
RNN_BLOCK = D_RNN // N_RNN_BLOCKS
CONV_WIDTH = 4
RG_C = 8.0
N_GROUPS = 4
EXPERTS_PER_GROUP = 4
N_EXPERTS = N_GROUPS * EXPERTS_PER_GROUP
TOP_K_IN_GROUP = 2
D_EXPERT = 512
N_BRANCHES = 2
EPS = 1e-6

Q_COLS = N_HEADS * 2 * QK_HEAD_DIM
K_COLS = N_HEADS * 2 * QK_HEAD_DIM
V_COLS = ATTN_WIDTH
IN_WIDTH = Q_COLS + K_COLS + V_COLS + 2 * D_RNN + N_BRANCHES * D_MODEL

kernel_name = 'hybrid_diffattn_rglru_hmoe_stream_step'


def _lambda_init(layer):
    return 0.8 - 0.6 * math.exp(-0.3 * layer)


def _rmsnorm(x, g):
    xf = x.astype(jnp.float32)
    y = xf * lax.rsqrt(jnp.mean(xf * xf, axis=-1, keepdims=True) + EPS)
    return (y * g.astype(jnp.float32)).astype(x.dtype)


def _diff_scores(qb, k, v, lam, mask):
    scale = QK_HEAD_DIM ** -0.5
    s = jnp.einsum('bqhmd,bkhmd->bmhqk', qb, k).astype(jnp.float32) * scale
    if mask is not None:
        s = jnp.where(mask, s, jnp.finfo(jnp.float32).min)
    p = jax.nn.softmax(s, axis=-1)
    w = p[:, 0] - lam * p[:, 1]
    return jnp.einsum('bhqk,bkhd->bqhd', w, v.astype(jnp.float32))


def _diff_attn_prompt(q, k, v, lam):
    b, s = q.shape[0], q.shape[1]
    n_blocks = s // Q_BLOCK
    k_chunk = jnp.arange(s) // CHUNK

    def one_block(i):
        start = i * Q_BLOCK
        qb = lax.dynamic_slice_in_dim(q, start, Q_BLOCK, axis=1)
        q_chunk = (start + jnp.arange(Q_BLOCK)) // CHUNK
        mask = k_chunk[None, :] <= q_chunk[:, None]
        return _diff_scores(qb, k, v, lam, mask)

    o = lax.map(one_block, jnp.arange(n_blocks))
    return jnp.moveaxis(o, 0, 1).reshape(b, s, N_HEADS, V_HEAD_DIM)


def _lin_combine(left, right):
    a_l, b_l = left
    a_r, b_r = right
    return a_l * a_r, a_r * b_l + b_r


def _rglru_branch(xr, gate_in, conv_state, h0, conv_w, conv_b, w_ra, b_ra, w_ri, b_ri, rg_lambda):
    t = xr.shape[1]
    xpad = jnp.concatenate([conv_state.astype(xr.dtype), xr], axis=1)
    xc = conv_b
    for j in range(CONV_WIDTH):
        xc = xc + xpad[:, j:j + t] * conv_w[j]
    new_conv = xpad[:, -(CONV_WIDTH - 1):]
    f32 = jnp.float32
    xf = xc.astype(f32)
    xb = xf.reshape(xf.shape[:2] + (N_RNN_BLOCKS, RNN_BLOCK))
    r = jax.nn.sigmoid(jnp.einsum('btni,nij->btnj', xb, w_ra.astype(f32)).reshape(xf.shape) + b_ra.astype(f32))
    i = jax.nn.sigmoid(jnp.einsum('btni,nij->btnj', xb, w_ri.astype(f32)).reshape(xf.shape) + b_ri.astype(f32))
    log_a = -RG_C * r * jax.nn.softplus(-rg_lambda.astype(f32))
    a = jnp.exp(log_a)
    mult = jnp.sqrt(-jnp.expm1(2.0 * log_a))
    bvec = mult * (i * xf)
    bvec = bvec.at[:, 0].add(a[:, 0] * h0.astype(f32))
    _, h = lax.associative_scan(_lin_combine, (a, bvec), axis=1)
    y = h * jax.nn.gelu(gate_in.astype(f32))
    return y.astype(xr.dtype), new_conv, h[:, -1].astype(xr.dtype)


def _token_mixer(xn, lp, layer, past):
    b, t, _ = xn.shape
    f32 = jnp.float32
    proj = xn @ lp['w_in']
    o1 = Q_COLS
    o2 = o1 + K_COLS
    o3 = o2 + V_COLS
    o4 = o3 + D_RNN
    o5 = o4 + D_RNN
    q = proj[..., :o1].reshape(b, t, N_HEADS, 2, QK_HEAD_DIM)
    k = proj[..., o1:o2].reshape(b, t, N_HEADS, 2, QK_HEAD_DIM)
    v = proj[..., o2:o3].reshape(b, t, N_HEADS, V_HEAD_DIM)
    xr = proj[..., o3:o4]
    gr = proj[..., o4:o5]
    gates = proj[..., o5:]
    lam_init = _lambda_init(layer)
    lam = (jnp.exp(jnp.sum(lp['lambda_q1'].astype(f32) * lp['lambda_k1'].astype(f32)))
           - jnp.exp(jnp.sum(lp['lambda_q2'].astype(f32) * lp['lambda_k2'].astype(f32))) + lam_init)
    if past is None:
        attn = _diff_attn_prompt(q, k, v, lam)
        conv_state = jnp.zeros((b, CONV_WIDTH - 1, D_RNN), xn.dtype)
        h0 = jnp.zeros((b, D_RNN), xn.dtype)
    else:
        cache_k, cache_v, conv_state, h0 = past
        k_all = jnp.concatenate([cache_k.astype(k.dtype), k], axis=1)
        v_all = jnp.concatenate([cache_v.astype(v.dtype), v], axis=1)
        attn = _diff_scores(q, k_all, v_all, lam, None)
    attn = _rmsnorm(attn, lp['subln_g']) * (1.0 - lam_init)
    attn = attn.astype(xn.dtype).reshape(b, t, ATTN_WIDTH)
    y_rnn, new_conv, h_last = _rglru_branch(xr, gr, conv_state, h0, lp['conv_w'], lp['conv_b'],
                                            lp['w_rg_a'], lp['b_rg_a'], lp['w_rg_i'], lp['b_rg_i'],
                                            lp['rg_lambda'])
    g = jax.nn.sigmoid(gates.astype(f32) + lp['b_gate'].astype(f32))
    g_a = g[..., :D_MODEL]
    g_r = g[..., D_MODEL:]
    merged = g_a * (attn @ lp['w_attn_proj']).astype(f32) + g_r * (y_rnn @ lp['w_rnn_proj']).astype(f32)
    out = merged.astype(xn.dtype) @ lp['w_out']
    return out, (k, v, new_conv, h_last)


def _hier_moe(xn, lp):
    f32 = jnp.float32
    shp = xn.shape
    xt = xn.reshape(-1, D_MODEL)
    n = xt.shape[0]
    g_logits = (xt @ lp['w_group'] + lp['b_group']).astype(f32)
    g_prob = jax.nn.softmax(g_logits, axis=-1)
    g_sel = jnp.argmax(g_logits, axis=-1)
    g_w = jnp.take_along_axis(g_prob, g_sel[:, None], axis=-1)
    e_logits = (xt @ lp['w_router'] + lp['b_router']).astype(f32).reshape(n, N_GROUPS, EXPERTS_PER_GROUP)
    e_in = jnp.take_along_axis(e_logits, g_sel[:, None, None], axis=1)[:, 0]
    top_v, top_i = lax.top_k(e_in, TOP_K_IN_GROUP)
    w_sel = jax.nn.softmax(top_v, axis=-1) * g_w
    e_ids = g_sel[:, None] * EXPERTS_PER_GROUP + top_i
    combine = jnp.sum(jax.nn.one_hot(e_ids, N_EXPERTS, dtype=f32) * w_sel[..., None], axis=1)
    y = jnp.zeros((n, D_MODEL), f32)
    for e in range(N_EXPERTS):
        h = jax.nn.silu(xt @ lp['w1'][e]) * (xt @ lp['w3'][e])
        y = y + combine[:, e:e + 1] * (h @ lp['w2'][e]).astype(f32)
    return y.astype(xn.dtype).reshape(shp)


def _layer(x, lp, layer, past):
    mix, new_state = _token_mixer(_rmsnorm(x, lp['norm1_g']), lp, layer, past)
    x = x + mix
    x = x + _hier_moe(_rmsnorm(x, lp['norm2_g']), lp)
    return x, new_state


def setup_inputs(seed: int = 0) -> dict:
    key = jax.random.key(seed)
    ks = jax.random.split(key, 40)
    f32 = jnp.float32
    nrm = lambda k, shape, s: jax.random.normal(k, shape, f32) * s
    u = jax.random.uniform(ks[20], (DEPTH, D_RNN), f32, minval=0.9, maxval=0.999)
    p = u ** (1.0 / RG_C)
    rg_lambda = jnp.log(p) - jnp.log1p(-p)
    return {
        'x_prompt': nrm(ks[0], (BATCH, SEQ, D_MODEL), 1.0),
        'x_sample': nrm(ks[1], (DEC_BATCH, DEC_SEQ, D_MODEL), 1.0),
        'cache_k': nrm(ks[2], (DEPTH, DEC_BATCH, PAST_LEN, N_HEADS, 2, QK_HEAD_DIM), 1.0),
        'cache_v': nrm(ks[3], (DEPTH, DEC_BATCH, PAST_LEN, N_HEADS, V_HEAD_DIM), 1.0),
        'state_conv': nrm(ks[4], (DEPTH, DEC_BATCH, CONV_WIDTH - 1, D_RNN), 1.0),
        'state_rnn': nrm(ks[5], (DEPTH, DEC_BATCH, D_RNN), 0.5),
        'norm1_g': 1.0 + nrm(ks[6], (DEPTH, D_MODEL), 0.02),
        'w_in': nrm(ks[7], (DEPTH, D_MODEL, IN_WIDTH), D_MODEL ** -0.5),
        'lambda_q1': nrm(ks[8], (DEPTH, QK_HEAD_DIM), 0.1),
        'lambda_k1': nrm(ks[9], (DEPTH, QK_HEAD_DIM), 0.1),
        'lambda_q2': nrm(ks[10], (DEPTH, QK_HEAD_DIM), 0.1),
        'lambda_k2': nrm(ks[11], (DEPTH, QK_HEAD_DIM), 0.1),
        'subln_g': 1.0 + nrm(ks[12], (DEPTH, V_HEAD_DIM), 0.02),
        'w_attn_proj': nrm(ks[13], (DEPTH, ATTN_WIDTH, D_MODEL), ATTN_WIDTH ** -0.5),
        'conv_w': nrm(ks[14], (DEPTH, CONV_WIDTH, D_RNN), CONV_WIDTH ** -0.5),
        'conv_b': nrm(ks[15], (DEPTH, D_RNN), 0.01),
        'w_rg_a': nrm(ks[16], (DEPTH, N_RNN_BLOCKS, RNN_BLOCK, RNN_BLOCK), RNN_BLOCK ** -0.5),
        'b_rg_a': nrm(ks[17], (DEPTH, D_RNN), 0.01),
        'w_rg_i': nrm(ks[18], (DEPTH, N_RNN_BLOCKS, RNN_BLOCK, RNN_BLOCK), RNN_BLOCK ** -0.5),
        'b_rg_i': nrm(ks[19], (DEPTH, D_RNN), 0.01),
        'rg_lambda': rg_lambda,
        'w_rnn_proj': nrm(ks[21], (DEPTH, D_RNN, D_MODEL), D_RNN ** -0.5),
        'b_gate': nrm(ks[22], (DEPTH, N_BRANCHES * D_MODEL), 0.01),
        'w_out': nrm(ks[23], (DEPTH, D_MODEL, D_MODEL), D_MODEL ** -0.5),
        'norm2_g': 1.0 + nrm(ks[24], (DEPTH, D_MODEL), 0.02),
        'w_group': nrm(ks[25], (DEPTH, D_MODEL, N_GROUPS), D_MODEL ** -0.5),
        'b_group': nrm(ks[26], (DEPTH, N_GROUPS), 0.01),
        'w_router': nrm(ks[27], (DEPTH, D_MODEL, N_EXPERTS), D_MODEL ** -0.5),
        'b_router': nrm(ks[28], (DEPTH, N_EXPERTS), 0.01),
        'w1': nrm(ks[29], (DEPTH, N_EXPERTS, D_MODEL, D_EXPERT), D_MODEL ** -0.5),
        'w3': nrm(ks[30], (DEPTH, N_EXPERTS, D_MODEL, D_EXPERT), D_MODEL ** -0.5),
        'w2': nrm(ks[31], (DEPTH, N_EXPERTS, D_EXPERT, D_MODEL), D_EXPERT ** -0.5),
        'final_norm_g': 1.0 + nrm(ks[32], (D_MODEL,), 0.02),
    }


def reference(x_prompt, x_sample, cache_k, cache_v, state_conv, state_rnn, norm1_g, w_in,
              lambda_q1, lambda_k1, lambda_q2, lambda_k2, subln_g, w_attn_proj, conv_w, conv_b,
              w_rg_a, b_rg_a, w_rg_i, b_rg_i, rg_lambda, w_rnn_proj, b_gate, w_out, norm2_g,
              w_group, b_group, w_router, b_router, w1, w3, w2, final_norm_g):
    xp = x_prompt
    xs = x_sample
    kp_l, vp_l, cp_l, hp_l = [], [], [], []
    ks_l, vs_l, cs_l, hs_l = [], [], [], []
    for l in range(DEPTH):
        lp = {
            'norm1_g': norm1_g[l], 'w_in': w_in[l],
            'lambda_q1': lambda_q1[l], 'lambda_k1': lambda_k1[l],
            'lambda_q2': lambda_q2[l], 'lambda_k2': lambda_k2[l],
            'subln_g': subln_g[l], 'w_attn_proj': w_attn_proj[l],
            'conv_w': conv_w[l], 'conv_b': conv_b[l],
            'w_rg_a': w_rg_a[l], 'b_rg_a': b_rg_a[l], 'w_rg_i': w_rg_i[l], 'b_rg_i': b_rg_i[l],
            'rg_lambda': rg_lambda[l], 'w_rnn_proj': w_rnn_proj[l], 'b_gate': b_gate[l],
            'w_out': w_out[l], 'norm2_g': norm2_g[l],
            'w_group': w_group[l], 'b_group': b_group[l],
            'w_router': w_router[l], 'b_router': b_router[l],
            'w1': w1[l], 'w3': w3[l], 'w2': w2[l],
        }
        xp, (kp, vp, cp, hp) = _layer(xp, lp, l, None)
        xs, (kk, vv, cc, hh) = _layer(xs, lp, l, (cache_k[l], cache_v[l], state_conv[l], state_rnn[l]))
        kp_l.append(kp)
        vp_l.append(vp)
        cp_l.append(cp)
        hp_l.append(hp)
        ks_l.append(kk)
        vs_l.append(vv)
        cs_l.append(cc)
        hs_l.append(hh)
    y_prompt = _rmsnorm(xp, final_norm_g)
    y_sample = _rmsnorm(xs, final_norm_g)
    return (y_prompt, y_sample,
            jnp.stack(kp_l), jnp.stack(vp_l), jnp.stack(cp_l), jnp.stack(hp_l),
            jnp.stack(ks_l), jnp.stack(vs_l), jnp.stack(cs_l), jnp.stack(hs_l))
```

```python
import functools
import math

import jax
import jax.numpy as jnp
from jax import lax
from jax.experimental import pallas as pl
from jax.experimental.pallas import tpu as pltpu

F32 = jnp.float32
BF16 = jnp.bfloat16

D_MODEL = 1024
CHUNK = 64
N_HEADS = 8
QK_HEAD_DIM = 64
V_HEAD_DIM = 128
HEAD_COLS = 2 * QK_HEAD_DIM
D_RNN = 1024
N_RNN_BLOCKS = 8
RNN_BLOCK = D_RNN // N_RNN_BLOCKS
CONV_WIDTH = 4
RG_C = 8.0
N_GROUPS = 4
EXPERTS_PER_GROUP = 4
N_EXPERTS = N_GROUPS * EXPERTS_PER_GROUP
D_EXPERT = 512
EPS = 1e-6
LAMBDA_INIT = 0.8 - 0.6 * math.exp(-0.3 * 0)
N_SEG = 7
IN_WIDTH = N_SEG * D_MODEL
ROUTE_LANES = 128
EXPERT_LANE0 = N_GROUPS
SUBLANES = 8
VMEM_LIMIT = 56 * 1024 * 1024
NEG = float(jnp.finfo(jnp.float32).min)


def _params(*sem):
    return pltpu.CompilerParams(dimension_semantics=sem, vmem_limit_bytes=VMEM_LIMIT)


def _rms(x, g):
    return x * lax.rsqrt(jnp.mean(x * x, axis=-1, keepdims=True) + EPS) * g


def _sigmoid(x):
    return 1.0 / (1.0 + jnp.exp(-x))


def _in_proj_kernel(x_ref, g_ref, w_ref, q_ref, k_ref, v_ref, kb_ref, vb_ref, xr_ref, gr_ref, gt_ref):
    xn = _rms(x_ref[...], g_ref[...]).astype(BF16)

    def seg(j):
        return jnp.dot(xn, w_ref[:, j * D_MODEL:(j + 1) * D_MODEL], preferred_element_type=F32)

    q_ref[...] = (seg(0) * (QK_HEAD_DIM ** -0.5)).astype(BF16)
    k = seg(1)
    k_ref[...] = k
    kb_ref[...] = k.astype(BF16)
    v = seg(2)
    v_ref[...] = v
    vb_ref[...] = v.astype(BF16)
    xr_ref[...] = seg(3)
    gr_ref[...] = seg(4)
    gt_ref[:, :D_MODEL] = seg(5)
    gt_ref[:, D_MODEL:] = seg(6)


def _in_proj(x2d, g1, w_in_b, tm):
    n = x2d.shape[0]
    row = lambda i: (i, 0)
    const = lambda i: (0, 0)
    wide = lambda dt: jax.ShapeDtypeStruct((n, D_MODEL), dt)
    return pl.pallas_call(
        _in_proj_kernel,
        grid=(n // tm,),
        in_specs=[pl.BlockSpec((tm, D_MODEL), row),
                  pl.BlockSpec((1, D_MODEL), const),
                  pl.BlockSpec((D_MODEL, IN_WIDTH), const, pipeline_mode=pl.Buffered(1))],
        out_specs=[pl.BlockSpec((tm, D_MODEL), row)] * 7 + [pl.BlockSpec((tm, 2 * D_MODEL), row)],
        out_shape=[wide(BF16), wide(F32), wide(F32), wide(BF16), wide(BF16), wide(F32), wide(F32),
                   jax.ShapeDtypeStruct((n, 2 * D_MODEL), F32)],
        compiler_params=_params("parallel"),
    )(x2d, g1, w_in_b)


def _lam(lq1, lk1, lq2, lk2):
    return (jnp.exp(jnp.sum(lq1[...] * lk1[...], axis=-1, keepdims=True))
            - jnp.exp(jnp.sum(lq2[...] * lk2[...], axis=-1, keepdims=True)) + LAMBDA_INIT)


def _split_q(q):
    lane = lax.broadcasted_iota(jnp.int32, q.shape, 1)
    zero = jnp.zeros_like(q)
    return jnp.where(lane < QK_HEAD_DIM, q, zero), jnp.where(lane >= QK_HEAD_DIM, q, zero)


def _scores(qm, k):
    return lax.dot_general(qm, k, (((1,), (1,)), ((), ())), preferred_element_type=F32)


def _subln(o, g):
    return (_rms(o, g) * (1.0 - LAMBDA_INIT)).astype(BF16)


def _attn_prompt_kernel(lq1, lk1, lq2, lk2, sg_ref, q_ref, k_ref, v_ref, o_ref,
                        m1, l1, a1, m2, l2, a2, *, tq):
    i = pl.program_id(2)
    q1, q2 = _split_q(q_ref[...])
    for m, l, a in ((m1, l1, a1), (m2, l2, a2)):
        m[...] = jnp.full(m.shape, -jnp.inf, F32)
        l[...] = jnp.zeros(l.shape, F32)
        a[...] = jnp.zeros(a.shape, F32)

    def update(s, v, m, l, a):
        m_prev = m[...]
        m_new = jnp.maximum(m_prev, jnp.max(s, axis=-1, keepdims=True))
        p = jnp.exp(s - m_new)
        alpha = jnp.exp(m_prev - m_new)
        l[...] = alpha * l[...] + jnp.sum(p, axis=-1, keepdims=True)
        a[...] = alpha * a[...] + jnp.dot(p.astype(BF16), v, preferred_element_type=F32)
        m[...] = m_new

    def tile(kt, mask):
        start = pl.multiple_of(kt * tq, tq)
        k = k_ref[pl.ds(start, tq), :]
        v = v_ref[pl.ds(start, tq), :]
        s1 = _scores(q1, k)
        s2 = _scores(q2, k)
        if mask is not None:
            s1 = jnp.where(mask, s1, NEG)
            s2 = jnp.where(mask, s2, NEG)
        update(s1, v, m1, l1, a1)
        update(s2, v, m2, l2, a2)

    def body(kt, c):
        tile(kt, None)
        return c

    lax.fori_loop(0, i, body, 0)
    rc = lax.broadcasted_iota(jnp.int32, (tq, tq), 0) // CHUNK
    cc = lax.broadcasted_iota(jnp.int32, (tq, tq), 1) // CHUNK
    tile(i, cc <= rc)

    lam = _lam(lq1, lk1, lq2, lk2)
    o = a1[...] / l1[...] - lam * (a2[...] / l2[...])
    o_ref[...] = _subln(o, sg_ref[...])


def _attn_prompt(q, kb, vb, lams, subln_g, batch, seq, tq):
    n = q.shape[0]
    nq = seq // tq
    small = lambda b, h, i: (0, 0)
    qmap = lambda b, h, i: (b * nq + i, h)
    kvmap = lambda b, h, i: (b, h)
    stat = pltpu.VMEM((tq, 1), F32)
    acc = pltpu.VMEM((tq, V_HEAD_DIM), F32)
    return pl.pallas_call(
        functools.partial(_attn_prompt_kernel, tq=tq),
        grid=(batch, N_HEADS, nq),
        in_specs=[pl.BlockSpec((1, QK_HEAD_DIM), small)] * 4 + [
            pl.BlockSpec((1, V_HEAD_DIM), small),
            pl.BlockSpec((tq, HEAD_COLS), qmap),
            pl.BlockSpec((seq, HEAD_COLS), kvmap),
            pl.BlockSpec((seq, V_HEAD_DIM), kvmap)],
        out_specs=pl.BlockSpec((tq, V_HEAD_DIM), qmap),
        out_shape=jax.ShapeDtypeStruct((n, N_HEADS * V_HEAD_DIM), BF16),
        scratch_shapes=[stat, stat, acc, stat, stat, acc],
        compiler_params=_params("parallel", "parallel", "arbitrary"),
    )(*lams, subln_g, q, kb, vb)


def _attn_sample_kernel(lq1, lk1, lq2, lk2, sg_ref, q_ref, kn_ref, vn_ref, ck_ref, cv_ref, o_ref):
    lam = _lam(lq1, lk1, lq2, lk2)
    for h in range(N_HEADS):
        cols = slice(h * HEAD_COLS, (h + 1) * HEAD_COLS)
        q1, q2 = _split_q(q_ref[:, cols])
        kc = ck_ref[0, :, cols].astype(BF16)
        vc = cv_ref[0, :, cols].astype(BF16)
        kn = kn_ref[:, cols]
        vn = vn_ref[:, cols]

        def branch(qm):
            sc = _scores(qm, kc)
            sn = _scores(qm, kn)
            m = jnp.maximum(jnp.max(sc, axis=-1, keepdims=True), jnp.max(sn, axis=-1, keepdims=True))
            pc = jnp.exp(sc - m)
            pn = jnp.exp(sn - m)
            l = jnp.sum(pc, axis=-1, keepdims=True) + jnp.sum(pn, axis=-1, keepdims=True)
            acc = (jnp.dot(pc.astype(BF16), vc, preferred_element_type=F32)
                   + jnp.dot(pn.astype(BF16), vn, preferred_element_type=F32))
            return acc / l

        o = branch(q1) - lam * branch(q2)
        o_ref[:, cols] = _subln(o, sg_ref[...])


def _attn_sample(q, kb, vb, cache_k, cache_v, lams, subln_g, batch, t):
    past = cache_k.shape[1]
    small = lambda b: (0, 0)
    row = lambda b: (b, 0)
    cache = lambda b: (b, 0, 0)
    return pl.pallas_call(
        _attn_sample_kernel,
        grid=(batch,),
        in_specs=[pl.BlockSpec((1, QK_HEAD_DIM), small)] * 4 + [
            pl.BlockSpec((1, V_HEAD_DIM), small),
            pl.BlockSpec((t, D_MODEL), row),
            pl.BlockSpec((t, D_MODEL), row),
            pl.BlockSpec((t, D_MODEL), row),
            pl.BlockSpec((1, past, D_MODEL), cache),
            pl.BlockSpec((1, past, D_MODEL), cache)],
        out_specs=pl.BlockSpec((t, D_MODEL), row),
        out_shape=jax.ShapeDtypeStruct((batch * t, D_MODEL), BF16),
        compiler_params=_params("parallel"),
    )(*lams, subln_g, q, kb, vb, cache_k, cache_v)


def _rglru_kernel(xr_ref, gr_ref, cs_ref, h0_ref, cw_ref, cb_ref, wa_ref, ba_ref, wi_ref, bi_ref, lam_ref,
                  y_ref, nc_ref, hl_ref, xpad, hcar, a_s, b_s, h_s, *, tt):
    t = pl.program_id(1)
    pad = SUBLANES

    @pl.when(t == 0)
    def _():
        xpad[0:pad, :] = jnp.zeros((pad, D_RNN), F32)
        xpad[pad - (CONV_WIDTH - 1):pad, :] = cs_ref[0]
        hcar[...] = h0_ref[0]

    xpad[pad:pad + tt, :] = xr_ref[...]
    xc = cb_ref[...]
    for j in range(CONV_WIDTH):
        off = pad - (CONV_WIDTH - 1) + j
        xc = xc + xpad[off:off + tt, :] * cw_ref[j:j + 1, :]
    xcb = xc.astype(BF16)

    z = -lam_ref[...]
    softplus = jnp.maximum(z, 0.0) + jnp.log1p(jnp.exp(-jnp.abs(z)))
    for n in range(N_RNN_BLOCKS):
        cols = slice(n * RNN_BLOCK, (n + 1) * RNN_BLOCK)
        xs = xcb[:, cols]
        r = _sigmoid(jnp.dot(xs, wa_ref[n], preferred_element_type=F32) + ba_ref[:, cols])
        i = _sigmoid(jnp.dot(xs, wi_ref[n], preferred_element_type=F32) + bi_ref[:, cols])
        log_a = -RG_C * r * softplus[:, cols]
        a_s[:, cols] = jnp.exp(log_a)
        th = jnp.tanh(log_a)
        b_s[:, cols] = jnp.sqrt(-2.0 * th / (1.0 - th)) * (i * xc[:, cols])

    def step(s, h):
        h = a_s[pl.ds(s, 1), :] * h + b_s[pl.ds(s, 1), :]
        h_s[pl.ds(s, 1), :] = h
        return h

    h = lax.fori_loop(0, tt, step, hcar[...], unroll=8)
    hcar[...] = h
    y_ref[...] = (h_s[...] * jax.nn.gelu(gr_ref[...])).astype(BF16)
    nc_ref[0] = xpad[pad + tt - (CONV_WIDTH - 1):pad + tt, :]
    hl_ref[0] = h
    xpad[0:pad, :] = xpad[tt:tt + pad, :]


def _rglru(xr, gr, conv_state, h0, conv_w, conv_b, w_a_b, b_a, w_i_b, b_i, rg_lambda, batch, t, tt):
    nt = t // tt
    row = lambda b, s: (b * nt + s, 0)
    const2 = lambda b, s: (0, 0)
    const3 = lambda b, s: (0, 0, 0)
    per_b = lambda b, s: (b, 0, 0)
    vec = pl.BlockSpec((1, D_RNN), const2)
    blk = pl.BlockSpec((N_RNN_BLOCKS, RNN_BLOCK, RNN_BLOCK), const3)
    big = pltpu.VMEM((tt, D_RNN), F32)
    return pl.pallas_call(
        functools.partial(_rglru_kernel, tt=tt),
        grid=(batch, nt),
        in_specs=[pl.BlockSpec((tt, D_RNN), row), pl.BlockSpec((tt, D_RNN), row),
                  pl.BlockSpec((1, CONV_WIDTH - 1, D_RNN), per_b), pl.BlockSpec((1, 1, D_RNN), per_b),
                  pl.BlockSpec((CONV_WIDTH, D_RNN), const2), vec, blk, vec, blk, vec, vec],
        out_specs=[pl.BlockSpec((tt, D_RNN), row),
                   pl.BlockSpec((1, CONV_WIDTH - 1, D_RNN), per_b),
                   pl.BlockSpec((1, 1, D_RNN), per_b)],
        out_shape=[jax.ShapeDtypeStruct((batch * t, D_RNN), BF16),
                   jax.ShapeDtypeStruct((batch, CONV_WIDTH - 1, D_RNN), F32),
                   jax.ShapeDtypeStruct((batch, 1, D_RNN), F32)],
        scratch_shapes=[pltpu.VMEM((tt + SUBLANES, D_RNN), F32), pltpu.VMEM((1, D_RNN), F32), big, big, big],
        compiler_params=_params("parallel", "arbitrary"),
    )(xr, gr, conv_state, h0, conv_w, conv_b, w_a_b, b_a, w_i_b, b_i, rg_lambda)


def _first_lane_of_max(vals, valid, lane):
    masked = jnp.where(valid, vals, -jnp.inf)
    mx = jnp.max(masked, axis=-1, keepdims=True)
    idx = jnp.min(jnp.where(valid & (masked == mx), lane, ROUTE_LANES), axis=-1, keepdims=True)
    return mx, idx


def _route(logits):
    lane = lax.broadcasted_iota(jnp.int32, logits.shape, 1)
    is_group = lane < N_GROUPS
    gmax, gsel = _first_lane_of_max(logits, is_group, lane)
    gsum = jnp.sum(jnp.where(is_group, jnp.exp(logits - gmax), 0.0), axis=-1, keepdims=True)
    g_w = 1.0 / gsum
    expert = lane - EXPERT_LANE0
    in_group = (expert >= 0) & (expert < N_EXPERTS) & ((expert // EXPERTS_PER_GROUP) == gsel)
    v1, i1 = _first_lane_of_max(logits, in_group, lane)
    v2, i2 = _first_lane_of_max(logits, in_group & (lane != i1), lane)
    e2 = jnp.exp(v2 - v1)
    den = 1.0 + e2
    return jnp.where(lane == i1, (1.0 / den) * g_w, 0.0) + jnp.where(lane == i2, (e2 / den) * g_w, 0.0)


def _merge_kernel(at_ref, y_ref, gt_ref, x_ref, wa_ref, wr_ref, wo_ref, bg_ref, g2_ref, wc_ref, bc_ref,
                  x1_ref, xn_ref, cm_ref):
    pa = jnp.dot(at_ref[...], wa_ref[...], preferred_element_type=F32)
    pr = jnp.dot(y_ref[...], wr_ref[...], preferred_element_type=F32)
    g = _sigmoid(gt_ref[...] + bg_ref[...])
    merged = g[:, :D_MODEL] * pa + g[:, D_MODEL:] * pr
    x1 = x_ref[...] + jnp.dot(merged.astype(BF16), wo_ref[...], preferred_element_type=F32)
    x1_ref[...] = x1
    xn = _rms(x1, g2_ref[...]).astype(BF16)
    xn_ref[...] = xn
    cm_ref[...] = _route(jnp.dot(xn, wc_ref[...], preferred_element_type=F32) + bc_ref[...])


def _merge(attn, y_rnn, gates, x2d, wa_b, wr_b, wo_b, b_gate, g2, w_cat_b, b_cat, tm):
    n = x2d.shape[0]
    row = lambda i: (i, 0)
    const = lambda i: (0, 0)
    sq = pl.BlockSpec((D_MODEL, D_MODEL), const)
    return pl.pallas_call(
        _merge_kernel,
        grid=(n // tm,),
        in_specs=[pl.BlockSpec((tm, D_MODEL), row), pl.BlockSpec((tm, D_MODEL), row),
                  pl.BlockSpec((tm, 2 * D_MODEL), row), pl.BlockSpec((tm, D_MODEL), row),
                  sq, sq, sq, pl.BlockSpec((1, 2 * D_MODEL), const), pl.BlockSpec((1, D_MODEL), const),
                  pl.BlockSpec((D_MODEL, ROUTE_LANES), const), pl.BlockSpec((1, ROUTE_LANES), const)],
        out_specs=[pl.BlockSpec((tm, D_MODEL), row), pl.BlockSpec((tm, D_MODEL), row),
                   pl.BlockSpec((tm, ROUTE_LANES), row)],
        out_shape=[jax.ShapeDtypeStruct((n, D_MODEL), F32), jax.ShapeDtypeStruct((n, D_MODEL), BF16),
                   jax.ShapeDtypeStruct((n, ROUTE_LANES), F32)],
        compiler_params=_params("parallel"),
    )(attn, y_rnn, gates, x2d, wa_b, wr_b, wo_b, b_gate, g2, w_cat_b, b_cat)


def _moe_kernel(xn_ref, cm_ref, x1_ref, w1_ref, w3_ref, w2_ref, gf_ref, o_ref, acc):
    e = pl.program_id(1)

    @pl.when(e == 0)
    def _():
        acc[...] = jnp.zeros(acc.shape, F32)

    xn = xn_ref[...]
    a = jnp.dot(xn, w1_ref[0], preferred_element_type=F32)
    h = (a * _sigmoid(a)) * jnp.dot(xn, w3_ref[0], preferred_element_type=F32)
    y = jnp.dot(h.astype(BF16), w2_ref[0], preferred_element_type=F32)
    cm = cm_ref[...]
    lane = lax.broadcasted_iota(jnp.int32, cm.shape, 1)
    c = jnp.sum(jnp.where(lane == e + EXPERT_LANE0, cm, 0.0), axis=-1, keepdims=True)
    acc[...] += c * y

    @pl.when(e == pl.num_programs(1) - 1)
    def _():
        o_ref[...] = _rms(x1_ref[...] + acc[...], gf_ref[...])


def _moe(xn2, comb, x1, w1_b, w3_b, w2_b, gf, tm):
    n = xn2.shape[0]
    row = lambda i, e: (i, 0)
    per_e = lambda i, e: (e, 0, 0)
    return pl.pallas_call(
        _moe_kernel,
        grid=(n // tm, N_EXPERTS),
        in_specs=[pl.BlockSpec((tm, D_MODEL), row), pl.BlockSpec((tm, ROUTE_LANES), row),
                  pl.BlockSpec((tm, D_MODEL), row),
                  pl.BlockSpec((1, D_MODEL, D_EXPERT), per_e), pl.BlockSpec((1, D_MODEL, D_EXPERT), per_e),
                  pl.BlockSpec((1, D_EXPERT, D_MODEL), per_e), pl.BlockSpec((1, D_MODEL), lambda i, e: (0, 0))],
        out_specs=pl.BlockSpec((tm, D_MODEL), row),
        out_shape=jax.ShapeDtypeStruct((n, D_MODEL), F32),
        scratch_shapes=[pltpu.VMEM((tm, D_MODEL), F32)],
        compiler_params=_params("parallel", "arbitrary"),
    )(xn2, comb, x1, w1_b, w3_b, w2_b, gf)


def _stream(x, past, w, *, tm_in, tm_merge, tm_moe, tq, tt):
    batch, t, _ = x.shape
    n = batch * t
    x2d = x.reshape(n, D_MODEL)
    q, k, v, kb, vb, xr, gr, gates = _in_proj(x2d, w["g1"], w["w_in"], tm_in)
    if past is None:
        attn = _attn_prompt(q, kb, vb, w["lams"], w["subln_g"], batch, t, tq)
        conv_state = jnp.zeros((batch, CONV_WIDTH - 1, D_RNN), F32)
        h0 = jnp.zeros((batch, 1, D_RNN), F32)
    else:
        cache_k, cache_v, conv_state, h0 = past
        attn = _attn_sample(q, kb, vb, cache_k, cache_v, w["lams"], w["subln_g"], batch, t)
    y_rnn, new_conv, h_last = _rglru(xr, gr, conv_state, h0, w["conv_w"], w["conv_b"], w["w_a"], w["b_a"],
                                     w["w_i"], w["b_i"], w["rg_lambda"], batch, t, tt)
    x1, xn2, comb = _merge(attn, y_rnn, gates, x2d, w["w_attn"], w["w_rnn"], w["w_out"], w["b_gate"],
                           w["g2"], w["w_cat"], w["b_cat"], tm_merge)
    y = _moe(xn2, comb, x1, w["w1"], w["w3"], w["w2"], w["gf"], tm_moe)
    return (y.reshape(batch, t, D_MODEL),
            k.reshape(1, batch, t, N_HEADS, 2, QK_HEAD_DIM),
            v.reshape(1, batch, t, N_HEADS, V_HEAD_DIM),
            new_conv.reshape(1, batch, CONV_WIDTH - 1, D_RNN),
            h_last.reshape(1, batch, D_RNN))


def kernel(x_prompt, x_sample, cache_k, cache_v, state_conv, state_rnn, norm1_g, w_in, lambda_q1, lambda_k1, lambda_q2, lambda_k2, subln_g, w_attn_proj, conv_w, conv_b, w_rg_a, b_rg_a, w_rg_i, b_rg_i, rg_lambda, w_rnn_proj, b_gate, w_out, norm2_g, w_group, b_group, w_router, b_router, w1, w3, w2, final_norm_g):
    assert norm1_g.shape[0] == 1, "single-layer model"
    dec_batch, past_len = cache_k.shape[1], cache_k.shape[2]
    pad = ROUTE_LANES - N_GROUPS - N_EXPERTS
    w = {
        "g1": norm1_g, "g2": norm2_g, "gf": final_norm_g.reshape(1, D_MODEL),
        "w_in": w_in[0].astype(BF16),
        "lams": (lambda_q1, lambda_k1, lambda_q2, lambda_k2),
        "subln_g": subln_g,
        "w_attn": w_attn_proj[0].astype(BF16), "w_rnn": w_rnn_proj[0].astype(BF16),
        "w_out": w_out[0].astype(BF16),
        "conv_w": conv_w[0], "conv_b": conv_b,
        "w_a": w_rg_a[0].astype(BF16), "b_a": b_rg_a, "w_i": w_rg_i[0].astype(BF16), "b_i": b_rg_i,
        "rg_lambda": rg_lambda, "b_gate": b_gate,
        "w_cat": jnp.pad(jnp.concatenate([w_group[0], w_router[0]], axis=1), ((0, 0), (0, pad))).astype(BF16),
        "b_cat": jnp.pad(jnp.concatenate([b_group[0], b_router[0]]), (0, pad)).reshape(1, ROUTE_LANES),
        "w1": w1[0].astype(BF16), "w3": w3[0].astype(BF16), "w2": w2[0].astype(BF16),
    }
    yp, kp, vp, cp, hp = _stream(x_prompt, None, w, tm_in=256, tm_merge=256, tm_moe=512, tq=256, tt=256)
    past = (cache_k[0].reshape(dec_batch, past_len, D_MODEL), cache_v[0].reshape(dec_batch, past_len, D_MODEL),
            state_conv[0], state_rnn[0].reshape(dec_batch, 1, D_RNN))
    dec_t = x_sample.shape[1]
    ys, ks, vs, cs, hs = _stream(x_sample, past, w, tm_in=256, tm_merge=256, tm_moe=256, tq=dec_t, tt=dec_t)
    return (yp, ys, kp, vp, cp, hp, ks, vs, cs, hs)
```

```python
import functools
import math

import jax
import jax.numpy as jnp
from jax import lax
from jax.experimental import pallas as pl
from jax.experimental.pallas import tpu as pltpu

F32 = jnp.float32
BF16 = jnp.bfloat16

D_MODEL = 1024
CHUNK = 64
N_HEADS = 8
QK_HEAD_DIM = 64
V_HEAD_DIM = 128
HEAD_COLS = 2 * QK_HEAD_DIM
D_RNN = 1024
N_RNN_BLOCKS = 8
RNN_BLOCK = D_RNN // N_RNN_BLOCKS
CONV_WIDTH = 4
RG_C = 8.0
N_GROUPS = 4
EXPERTS_PER_GROUP = 4
N_EXPERTS = N_GROUPS * EXPERTS_PER_GROUP
D_EXPERT = 512
EPS = 1e-6
LAMBDA_INIT = 0.8 - 0.6 * math.exp(-0.3 * 0)
N_SEG = 7
IN_WIDTH = N_SEG * D_MODEL
ROUTE_LANES = 128
EXPERT_LANE0 = N_GROUPS
SUBLANES = 8
LANES = 128
HEADS_PER_STEP = 2
VMEM_LIMIT = 56 * 1024 * 1024
NEG = float(jnp.finfo(jnp.float32).min)


def _params(*sem):
    return pltpu.CompilerParams(dimension_semantics=sem, vmem_limit_bytes=VMEM_LIMIT)


def _rms(x, g):
    return x * lax.rsqrt(jnp.mean(x * x, axis=-1, keepdims=True) + EPS) * g


def _sigmoid(x):
    return 1.0 / (1.0 + jnp.exp(-x))


def _in_proj_kernel(x_ref, g_ref, w_ref, q_ref, k_ref, v_ref, kb_ref, vb_ref, xr_ref, gr_ref, gt_ref):
    xn = _rms(x_ref[...], g_ref[...]).astype(BF16)

    def seg(j):
        return jnp.dot(xn, w_ref[:, j * D_MODEL:(j + 1) * D_MODEL], preferred_element_type=F32)

    q_ref[...] = (seg(0) * (QK_HEAD_DIM ** -0.5)).astype(BF16)
    k = seg(1)
    k_ref[...] = k
    kb_ref[...] = k.astype(BF16)
    v = seg(2)
    v_ref[...] = v
    vb_ref[...] = v.astype(BF16)
    xr_ref[...] = seg(3)
    gr_ref[...] = seg(4)
    gt_ref[:, :D_MODEL] = seg(5)
    gt_ref[:, D_MODEL:] = seg(6)


def _in_proj(x2d, g1, w_in_b, tm):
    n = x2d.shape[0]
    row = lambda i: (i, 0)
    const = lambda i: (0, 0)
    wide = lambda dt: jax.ShapeDtypeStruct((n, D_MODEL), dt)
    return pl.pallas_call(
        _in_proj_kernel,
        grid=(n // tm,),
        in_specs=[pl.BlockSpec((tm, D_MODEL), row),
                  pl.BlockSpec((1, D_MODEL), const),
                  pl.BlockSpec((D_MODEL, IN_WIDTH), const, pipeline_mode=pl.Buffered(1))],
        out_specs=[pl.BlockSpec((tm, D_MODEL), row)] * 7 + [pl.BlockSpec((tm, 2 * D_MODEL), row)],
        out_shape=[wide(BF16), wide(F32), wide(F32), wide(BF16), wide(BF16), wide(F32), wide(F32),
                   jax.ShapeDtypeStruct((n, 2 * D_MODEL), F32)],
        compiler_params=_params("parallel"),
    )(x2d, g1, w_in_b)


def _lam(lq1, lk1, lq2, lk2):
    return (jnp.exp(jnp.sum(lq1[...] * lk1[...], axis=-1, keepdims=True))
            - jnp.exp(jnp.sum(lq2[...] * lk2[...], axis=-1, keepdims=True)) + LAMBDA_INIT)


def _split_q(q):
    lane = lax.broadcasted_iota(jnp.int32, q.shape, 1)
    zero = jnp.zeros_like(q)
    return jnp.where(lane < QK_HEAD_DIM, q, zero), jnp.where(lane >= QK_HEAD_DIM, q, zero)


def _scores(qm, k):
    return lax.dot_general(qm, k, (((1,), (1,)), ((), ())), preferred_element_type=F32)


def _subln(o, g):
    return (_rms(o, g) * (1.0 - LAMBDA_INIT)).astype(BF16)


def _attn_prompt_kernel(lq1, lk1, lq2, lk2, sg_ref, q_ref, k_ref, v_ref, o_ref, m_ref, acc_ref, *, tq):
    i = pl.program_id(2)
    nchunk = tq // LANES
    m_ref[...] = jnp.full(m_ref.shape, -jnp.inf, F32)
    acc_ref[...] = jnp.zeros(acc_ref.shape, F32)
    ones = jnp.ones((tq, LANES), BF16)
    qq = []
    for h in range(HEADS_PER_STEP):
        q1, q2 = _split_q(q_ref[:, h * HEAD_COLS:(h + 1) * HEAD_COLS])
        qq.append(jnp.concatenate([q1, q2], axis=0))

    def tile(kt, h, masked):
        start = pl.multiple_of(kt * tq, tq)
        s = _scores(qq[h], k_ref[pl.ds(start, tq), h * HEAD_COLS:(h + 1) * HEAD_COLS])
        chunks = [s[:, c * LANES:(c + 1) * LANES] for c in range(nchunk)]
        if masked:
            row = lax.broadcasted_iota(jnp.int32, (2 * tq, LANES), 0)
            lane = lax.broadcasted_iota(jnp.int32, (2 * tq, LANES), 1)
            qchunk = (row % tq) // CHUNK
            chunks = [jnp.where((lane + c * LANES) // CHUNK <= qchunk, x, NEG) for c, x in enumerate(chunks)]
        part = chunks[0]
        for x in chunks[1:]:
            part = jnp.maximum(part, x)
        m_prev = m_ref[h]
        m_next = jnp.maximum(m_prev, jnp.max(part, axis=1, keepdims=True))
        p = jnp.concatenate([jnp.exp(x - m_next).astype(BF16) for x in chunks], axis=1)
        alpha = jnp.exp(m_prev - m_next)
        v1 = jnp.concatenate([v_ref[pl.ds(start, tq), h * V_HEAD_DIM:(h + 1) * V_HEAD_DIM], ones], axis=1)
        acc_ref[h] = (jnp.concatenate([alpha, alpha], axis=1) * acc_ref[h]
                      + jnp.dot(p, v1, preferred_element_type=F32))
        m_ref[h] = m_next

    def body(kt, c):
        for h in range(HEADS_PER_STEP):
            tile(kt, h, False)
        return c

    lax.fori_loop(0, i, body, 0)
    lam = _lam(lq1, lk1, lq2, lk2)
    for h in range(HEADS_PER_STEP):
        tile(i, h, True)
    for h in range(HEADS_PER_STEP):
        o1 = acc_ref[h, :tq, :V_HEAD_DIM] / acc_ref[h, :tq, V_HEAD_DIM:]
        o2 = acc_ref[h, tq:, :V_HEAD_DIM] / acc_ref[h, tq:, V_HEAD_DIM:]
        o_ref[:, h * V_HEAD_DIM:(h + 1) * V_HEAD_DIM] = _subln(o1 - lam * o2, sg_ref[...])


def _attn_prompt(q, kb, vb, lams, subln_g, batch, seq, tq):
    n = q.shape[0]
    nq = seq // tq
    small = lambda b, h, i: (0, 0)
    qmap = lambda b, h, i: (b * nq + i, h)
    kvmap = lambda b, h, i: (b, h)
    return pl.pallas_call(
        functools.partial(_attn_prompt_kernel, tq=tq),
        grid=(batch, N_HEADS // HEADS_PER_STEP, nq),
        in_specs=[pl.BlockSpec((1, QK_HEAD_DIM), small)] * 4 + [
            pl.BlockSpec((1, V_HEAD_DIM), small),
            pl.BlockSpec((tq, HEADS_PER_STEP * HEAD_COLS), qmap),
            pl.BlockSpec((seq, HEADS_PER_STEP * HEAD_COLS), kvmap),
            pl.BlockSpec((seq, HEADS_PER_STEP * V_HEAD_DIM), kvmap)],
        out_specs=pl.BlockSpec((tq, HEADS_PER_STEP * V_HEAD_DIM), qmap),
        out_shape=jax.ShapeDtypeStruct((n, N_HEADS * V_HEAD_DIM), BF16),
        scratch_shapes=[pltpu.VMEM((HEADS_PER_STEP, 2 * tq, LANES), F32),
                        pltpu.VMEM((HEADS_PER_STEP, 2 * tq, 2 * LANES), F32)],
        compiler_params=_params("parallel", "parallel", "arbitrary"),
    )(*lams, subln_g, q, kb, vb)


def _attn_sample_kernel(lq1, lk1, lq2, lk2, sg_ref, q_ref, kn_ref, vn_ref, ck_ref, cv_ref, o_ref):
    lam = _lam(lq1, lk1, lq2, lk2)
    for h in range(N_HEADS):
        cols = slice(h * HEAD_COLS, (h + 1) * HEAD_COLS)
        q1, q2 = _split_q(q_ref[:, cols])
        kc = ck_ref[0, :, cols].astype(BF16)
        vc = cv_ref[0, :, cols].astype(BF16)
        kn = kn_ref[:, cols]
        vn = vn_ref[:, cols]

        def branch(qm):
            sc = _scores(qm, kc)
            sn = _scores(qm, kn)
            m = jnp.maximum(jnp.max(sc, axis=-1, keepdims=True), jnp.max(sn, axis=-1, keepdims=True))
            pc = jnp.exp(sc - m)
            pn = jnp.exp(sn - m)
            l = jnp.sum(pc, axis=-1, keepdims=True) + jnp.sum(pn, axis=-1, keepdims=True)
            acc = (jnp.dot(pc.astype(BF16), vc, preferred_element_type=F32)
                   + jnp.dot(pn.astype(BF16), vn, preferred_element_type=F32))
            return acc / l

        o = branch(q1) - lam * branch(q2)
        o_ref[:, cols] = _subln(o, sg_ref[...])


def _attn_sample(q, kb, vb, cache_k, cache_v, lams, subln_g, batch, t):
    past = cache_k.shape[1]
    small = lambda b: (0, 0)
    row = lambda b: (b, 0)
    cache = lambda b: (b, 0, 0)
    return pl.pallas_call(
        _attn_sample_kernel,
        grid=(batch,),
        in_specs=[pl.BlockSpec((1, QK_HEAD_DIM), small)] * 4 + [
            pl.BlockSpec((1, V_HEAD_DIM), small),
            pl.BlockSpec((t, D_MODEL), row),
            pl.BlockSpec((t, D_MODEL), row),
            pl.BlockSpec((t, D_MODEL), row),
            pl.BlockSpec((1, past, D_MODEL), cache),
            pl.BlockSpec((1, past, D_MODEL), cache)],
        out_specs=pl.BlockSpec((t, D_MODEL), row),
        out_shape=jax.ShapeDtypeStruct((batch * t, D_MODEL), BF16),
        compiler_params=_params("parallel"),
    )(*lams, subln_g, q, kb, vb, cache_k, cache_v)


def _rglru_kernel(xr_ref, gr_ref, cs_ref, h0_ref, cw_ref, cb_ref, wa_ref, ba_ref, wi_ref, bi_ref, lam_ref,
                  y_ref, nc_ref, hl_ref, xpad, hcar, a_s, b_s, h_s, *, tt):
    t = pl.program_id(1)
    pad = SUBLANES

    @pl.when(t == 0)
    def _():
        xpad[0:pad, :] = jnp.zeros((pad, D_RNN), F32)
        xpad[pad - (CONV_WIDTH - 1):pad, :] = cs_ref[0]
        hcar[...] = h0_ref[0]

    xpad[pad:pad + tt, :] = xr_ref[...]
    xc = cb_ref[...]
    for j in range(CONV_WIDTH):
        off = pad - (CONV_WIDTH - 1) + j
        xc = xc + xpad[off:off + tt, :] * cw_ref[j:j + 1, :]
    xcb = xc.astype(BF16)

    z = -lam_ref[...]
    softplus = jnp.maximum(z, 0.0) + jnp.log1p(jnp.exp(-jnp.abs(z)))
    for n in range(N_RNN_BLOCKS):
        cols = slice(n * RNN_BLOCK, (n + 1) * RNN_BLOCK)
        xs = xcb[:, cols]
        r = _sigmoid(jnp.dot(xs, wa_ref[n], preferred_element_type=F32) + ba_ref[:, cols])
        i = _sigmoid(jnp.dot(xs, wi_ref[n], preferred_element_type=F32) + bi_ref[:, cols])
        log_a = -RG_C * r * softplus[:, cols]
        a_s[:, cols] = jnp.exp(log_a)
        th = jnp.tanh(log_a)
        b_s[:, cols] = jnp.sqrt(-2.0 * th / (1.0 - th)) * (i * xc[:, cols])

    def step(s, h):
        h = a_s[pl.ds(s, 1), :] * h + b_s[pl.ds(s, 1), :]
        h_s[pl.ds(s, 1), :] = h
        return h

    h = lax.fori_loop(0, tt, step, hcar[...], unroll=8)
    hcar[...] = h
    y_ref[...] = (h_s[...] * jax.nn.gelu(gr_ref[...])).astype(BF16)
    nc_ref[0] = xpad[pad + tt - (CONV_WIDTH - 1):pad + tt, :]
    hl_ref[0] = h
    xpad[0:pad, :] = xpad[tt:tt + pad, :]


def _rglru(xr, gr, conv_state, h0, conv_w, conv_b, w_a_b, b_a, w_i_b, b_i, rg_lambda, batch, t, tt):
    nt = t // tt
    row = lambda b, s: (b * nt + s, 0)
    const2 = lambda b, s: (0, 0)
    const3 = lambda b, s: (0, 0, 0)
    per_b = lambda b, s: (b, 0, 0)
    vec = pl.BlockSpec((1, D_RNN), const2)
    blk = pl.BlockSpec((N_RNN_BLOCKS, RNN_BLOCK, RNN_BLOCK), const3)
    big = pltpu.VMEM((tt, D_RNN), F32)
    return pl.pallas_call(
        functools.partial(_rglru_kernel, tt=tt),
        grid=(batch, nt),
        in_specs=[pl.BlockSpec((tt, D_RNN), row), pl.BlockSpec((tt, D_RNN), row),
                  pl.BlockSpec((1, CONV_WIDTH - 1, D_RNN), per_b), pl.BlockSpec((1, 1, D_RNN), per_b),
                  pl.BlockSpec((CONV_WIDTH, D_RNN), const2), vec, blk, vec, blk, vec, vec],
        out_specs=[pl.BlockSpec((tt, D_RNN), row),
                   pl.BlockSpec((1, CONV_WIDTH - 1, D_RNN), per_b),
                   pl.BlockSpec((1, 1, D_RNN), per_b)],
        out_shape=[jax.ShapeDtypeStruct((batch * t, D_RNN), BF16),
                   jax.ShapeDtypeStruct((batch, CONV_WIDTH - 1, D_RNN), F32),
                   jax.ShapeDtypeStruct((batch, 1, D_RNN), F32)],
        scratch_shapes=[pltpu.VMEM((tt + SUBLANES, D_RNN), F32), pltpu.VMEM((1, D_RNN), F32), big, big, big],
        compiler_params=_params("parallel", "arbitrary"),
    )(xr, gr, conv_state, h0, conv_w, conv_b, w_a_b, b_a, w_i_b, b_i, rg_lambda)


def _first_lane_of_max(vals, valid, lane):
    masked = jnp.where(valid, vals, -jnp.inf)
    mx = jnp.max(masked, axis=-1, keepdims=True)
    idx = jnp.min(jnp.where(valid & (masked == mx), lane, ROUTE_LANES), axis=-1, keepdims=True)
    return mx, idx


def _route(logits):
    lane = lax.broadcasted_iota(jnp.int32, logits.shape, 1)
    is_group = lane < N_GROUPS
    gmax, gsel = _first_lane_of_max(logits, is_group, lane)
    gsum = jnp.sum(jnp.where(is_group, jnp.exp(logits - gmax), 0.0), axis=-1, keepdims=True)
    g_w = 1.0 / gsum
    expert = lane - EXPERT_LANE0
    in_group = (expert >= 0) & (expert < N_EXPERTS) & ((expert // EXPERTS_PER_GROUP) == gsel)
    v1, i1 = _first_lane_of_max(logits, in_group, lane)
    v2, i2 = _first_lane_of_max(logits, in_group & (lane != i1), lane)
    e2 = jnp.exp(v2 - v1)
    den = 1.0 + e2
    return jnp.where(lane == i1, (1.0 / den) * g_w, 0.0) + jnp.where(lane == i2, (e2 / den) * g_w, 0.0)


def _merge_kernel(at_ref, y_ref, gt_ref, x_ref, wa_ref, wr_ref, wo_ref, bg_ref, g2_ref, wc_ref, bc_ref,
                  x1_ref, xn_ref, cm_ref):
    pa = jnp.dot(at_ref[...], wa_ref[...], preferred_element_type=F32)
    pr = jnp.dot(y_ref[...], wr_ref[...], preferred_element_type=F32)
    g = _sigmoid(gt_ref[...] + bg_ref[...])
    merged = g[:, :D_MODEL] * pa + g[:, D_MODEL:] * pr
    x1 = x_ref[...] + jnp.dot(merged.astype(BF16), wo_ref[...], preferred_element_type=F32)
    x1_ref[...] = x1
    xn = _rms(x1, g2_ref[...]).astype(BF16)
    xn_ref[...] = xn
    cm_ref[...] = _route(jnp.dot(xn, wc_ref[...], preferred_element_type=F32) + bc_ref[...])


def _merge(attn, y_rnn, gates, x2d, wa_b, wr_b, wo_b, b_gate, g2, w_cat_b, b_cat, tm):
    n = x2d.shape[0]
    row = lambda i: (i, 0)
    const = lambda i: (0, 0)
    sq = pl.BlockSpec((D_MODEL, D_MODEL), const)
    return pl.pallas_call(
        _merge_kernel,
        grid=(n // tm,),
        in_specs=[pl.BlockSpec((tm, D_MODEL), row), pl.BlockSpec((tm, D_MODEL), row),
                  pl.BlockSpec((tm, 2 * D_MODEL), row), pl.BlockSpec((tm, D_MODEL), row),
                  sq, sq, sq, pl.BlockSpec((1, 2 * D_MODEL), const), pl.BlockSpec((1, D_MODEL), const),
                  pl.BlockSpec((D_MODEL, ROUTE_LANES), const), pl.BlockSpec((1, ROUTE_LANES), const)],
        out_specs=[pl.BlockSpec((tm, D_MODEL), row), pl.BlockSpec((tm, D_MODEL), row),
                   pl.BlockSpec((tm, ROUTE_LANES), row)],
        out_shape=[jax.ShapeDtypeStruct((n, D_MODEL), F32), jax.ShapeDtypeStruct((n, D_MODEL), BF16),
                   jax.ShapeDtypeStruct((n, ROUTE_LANES), F32)],
        compiler_params=_params("parallel"),
    )(attn, y_rnn, gates, x2d, wa_b, wr_b, wo_b, b_gate, g2, w_cat_b, b_cat)


def _moe_kernel(xn_ref, cm_ref, x1_ref, w1_ref, w3_ref, w2_ref, gf_ref, o_ref, acc):
    e = pl.program_id(1)

    @pl.when(e == 0)
    def _():
        acc[...] = jnp.zeros(acc.shape, F32)

    xn = xn_ref[...]
    a = jnp.dot(xn, w1_ref[0], preferred_element_type=F32)
    h = (a * _sigmoid(a)) * jnp.dot(xn, w3_ref[0], preferred_element_type=F32)
    y = jnp.dot(h.astype(BF16), w2_ref[0], preferred_element_type=F32)
    cm = cm_ref[...]
    lane = lax.broadcasted_iota(jnp.int32, cm.shape, 1)
    c = jnp.sum(jnp.where(lane == e + EXPERT_LANE0, cm, 0.0), axis=-1, keepdims=True)
    acc[...] += c * y

    @pl.when(e == pl.num_programs(1) - 1)
    def _():
        o_ref[...] = _rms(x1_ref[...] + acc[...], gf_ref[...])


def _moe(xn2, comb, x1, w1_b, w3_b, w2_b, gf, tm):
    n = xn2.shape[0]
    row = lambda i, e: (i, 0)
    per_e = lambda i, e: (e, 0, 0)
    return pl.pallas_call(
        _moe_kernel,
        grid=(n // tm, N_EXPERTS),
        in_specs=[pl.BlockSpec((tm, D_MODEL), row), pl.BlockSpec((tm, ROUTE_LANES), row),
                  pl.BlockSpec((tm, D_MODEL), row),
                  pl.BlockSpec((1, D_MODEL, D_EXPERT), per_e), pl.BlockSpec((1, D_MODEL, D_EXPERT), per_e),
                  pl.BlockSpec((1, D_EXPERT, D_MODEL), per_e), pl.BlockSpec((1, D_MODEL), lambda i, e: (0, 0))],
        out_specs=pl.BlockSpec((tm, D_MODEL), row),
        out_shape=jax.ShapeDtypeStruct((n, D_MODEL), F32),
        scratch_shapes=[pltpu.VMEM((tm, D_MODEL), F32)],
        compiler_params=_params("parallel", "arbitrary"),
    )(xn2, comb, x1, w1_b, w3_b, w2_b, gf)


def _stream(x, past, w, *, tm_in, tm_merge, tm_moe, tq, tt):
    batch, t, _ = x.shape
    n = batch * t
    x2d = x.reshape(n, D_MODEL)
    q, k, v, kb, vb, xr, gr, gates = _in_proj(x2d, w["g1"], w["w_in"], tm_in)
    if past is None:
        attn = _attn_prompt(q, kb, vb, w["lams"], w["subln_g"], batch, t, tq)
        conv_state = jnp.zeros((batch, CONV_WIDTH - 1, D_RNN), F32)
        h0 = jnp.zeros((batch, 1, D_RNN), F32)
    else:
        cache_k, cache_v, conv_state, h0 = past
        attn = _attn_sample(q, kb, vb, cache_k, cache_v, w["lams"], w["subln_g"], batch, t)
    y_rnn, new_conv, h_last = _rglru(xr, gr, conv_state, h0, w["conv_w"], w["conv_b"], w["w_a"], w["b_a"],
                                     w["w_i"], w["b_i"], w["rg_lambda"], batch, t, tt)
    x1, xn2, comb = _merge(attn, y_rnn, gates, x2d, w["w_attn"], w["w_rnn"], w["w_out"], w["b_gate"],
                           w["g2"], w["w_cat"], w["b_cat"], tm_merge)
    y = _moe(xn2, comb, x1, w["w1"], w["w3"], w["w2"], w["gf"], tm_moe)
    return (y.reshape(batch, t, D_MODEL),
            k.reshape(1, batch, t, N_HEADS, 2, QK_HEAD_DIM),
            v.reshape(1, batch, t, N_HEADS, V_HEAD_DIM),
            new_conv.reshape(1, batch, CONV_WIDTH - 1, D_RNN),
            h_last.reshape(1, batch, D_RNN))


def kernel(x_prompt, x_sample, cache_k, cache_v, state_conv, state_rnn, norm1_g, w_in, lambda_q1, lambda_k1, lambda_q2, lambda_k2, subln_g, w_attn_proj, conv_w, conv_b, w_rg_a, b_rg_a, w_rg_i, b_rg_i, rg_lambda, w_rnn_proj, b_gate, w_out, norm2_g, w_group, b_group, w_router, b_router, w1, w3, w2, final_norm_g):
    assert norm1_g.shape[0] == 1, "single-layer model"
    dec_batch, past_len = cache_k.shape[1], cache_k.shape[2]
    pad = ROUTE_LANES - N_GROUPS - N_EXPERTS
    w = {
        "g1": norm1_g, "g2": norm2_g, "gf": final_norm_g.reshape(1, D_MODEL),
        "w_in": w_in[0].astype(BF16),
        "lams": (lambda_q1, lambda_k1, lambda_q2, lambda_k2),
        "subln_g": subln_g,
        "w_attn": w_attn_proj[0].astype(BF16), "w_rnn": w_rnn_proj[0].astype(BF16),
        "w_out": w_out[0].astype(BF16),
        "conv_w": conv_w[0], "conv_b": conv_b,
        "w_a": w_rg_a[0].astype(BF16), "b_a": b_rg_a, "w_i": w_rg_i[0].astype(BF16), "b_i": b_rg_i,
        "rg_lambda": rg_lambda, "b_gate": b_gate,
        "w_cat": jnp.pad(jnp.concatenate([w_group[0], w_router[0]], axis=1), ((0, 0), (0, pad))).astype(BF16),
        "b_cat": jnp.pad(jnp.concatenate([b_group[0], b_router[0]]), (0, pad)).reshape(1, ROUTE_LANES),
        "w1": w1[0].astype(BF16), "w3": w3[0].astype(BF16), "w2": w2[0].astype(BF16),
    }
    yp, kp, vp, cp, hp = _stream(x_prompt, None, w, tm_in=256, tm_merge=256, tm_moe=512, tq=512, tt=256)
    past = (cache_k[0].reshape(dec_batch, past_len, D_MODEL), cache_v[0].reshape(dec_batch, past_len, D_MODEL),
            state_conv[0], state_rnn[0].reshape(dec_batch, 1, D_RNN))
    dec_t = x_sample.shape[1]
    ys, ks, vs, cs, hs = _stream(x_sample, past, w, tm_in=256, tm_merge=256, tm_moe=256, tq=dec_t, tt=dec_t)
    return (yp, ys, kp, vp, cp, hp, ks, vs, cs, hs)
```

```python
import functools
import math

import jax
import jax.numpy as jnp
from jax import lax
from jax.experimental import pallas as pl
from jax.experimental.pallas import tpu as pltpu

F32 = jnp.float32
BF16 = jnp.bfloat16

D_MODEL = 1024
CHUNK = 64
N_HEADS = 8
QK_HEAD_DIM = 64
V_HEAD_DIM = 128
HEAD_COLS = 2 * QK_HEAD_DIM
D_RNN = 1024
N_RNN_BLOCKS = 8
RNN_BLOCK = D_RNN // N_RNN_BLOCKS
CONV_WIDTH = 4
RG_C = 8.0
N_GROUPS = 4
EXPERTS_PER_GROUP = 4
N_EXPERTS = N_GROUPS * EXPERTS_PER_GROUP
D_EXPERT = 512
EPS = 1e-6
LAMBDA_INIT = 0.8 - 0.6 * math.exp(-0.3 * 0)
N_SEG = 7
IN_WIDTH = N_SEG * D_MODEL
ROUTE_LANES = 128
MOE_ROWS = 256
EXPERT_LANE0 = N_GROUPS
SUBLANES = 8
LANES = 128
HEADS_PER_STEP = 2
VMEM_LIMIT = 56 * 1024 * 1024
NEG = float(jnp.finfo(jnp.float32).min)


def _params(*sem):
    return pltpu.CompilerParams(dimension_semantics=sem, vmem_limit_bytes=VMEM_LIMIT)


def _rms(x, g):
    return x * lax.rsqrt(jnp.mean(x * x, axis=-1, keepdims=True) + EPS) * g


def _sigmoid(x):
    return 1.0 / (1.0 + jnp.exp(-x))


def _in_proj_kernel(x_ref, g_ref, w_ref, q_ref, k_ref, v_ref, kb_ref, vb_ref, xr_ref, gr_ref, gt_ref):
    xn = _rms(x_ref[...], g_ref[...]).astype(BF16)

    def seg(j):
        return jnp.dot(xn, w_ref[:, j * D_MODEL:(j + 1) * D_MODEL], preferred_element_type=F32)

    q_ref[...] = (seg(0) * (QK_HEAD_DIM ** -0.5)).astype(BF16)
    k = seg(1)
    k_ref[...] = k
    kb_ref[...] = k.astype(BF16)
    v = seg(2)
    v_ref[...] = v
    vb_ref[...] = v.astype(BF16)
    xr_ref[...] = seg(3)
    gr_ref[...] = seg(4)
    gt_ref[:, :D_MODEL] = seg(5)
    gt_ref[:, D_MODEL:] = seg(6)


def _in_proj(x2d, g1, w_in_b, tm):
    n = x2d.shape[0]
    row = lambda i: (i, 0)
    const = lambda i: (0, 0)
    wide = lambda dt: jax.ShapeDtypeStruct((n, D_MODEL), dt)
    return pl.pallas_call(
        _in_proj_kernel,
        grid=(n // tm,),
        in_specs=[pl.BlockSpec((tm, D_MODEL), row),
                  pl.BlockSpec((1, D_MODEL), const),
                  pl.BlockSpec((D_MODEL, IN_WIDTH), const, pipeline_mode=pl.Buffered(1))],
        out_specs=[pl.BlockSpec((tm, D_MODEL), row)] * 7 + [pl.BlockSpec((tm, 2 * D_MODEL), row)],
        out_shape=[wide(BF16), wide(F32), wide(F32), wide(BF16), wide(BF16), wide(F32), wide(F32),
                   jax.ShapeDtypeStruct((n, 2 * D_MODEL), F32)],
        compiler_params=_params("parallel"),
    )(x2d, g1, w_in_b)


def _lam(lq1, lk1, lq2, lk2):
    return (jnp.exp(jnp.sum(lq1[...] * lk1[...], axis=-1, keepdims=True))
            - jnp.exp(jnp.sum(lq2[...] * lk2[...], axis=-1, keepdims=True)) + LAMBDA_INIT)


def _split_q(q):
    lane = lax.broadcasted_iota(jnp.int32, q.shape, 1)
    zero = jnp.zeros_like(q)
    return jnp.where(lane < QK_HEAD_DIM, q, zero), jnp.where(lane >= QK_HEAD_DIM, q, zero)


def _scores(qm, k):
    return lax.dot_general(qm, k, (((1,), (1,)), ((), ())), preferred_element_type=F32)


def _subln(o, g):
    return (_rms(o, g) * (1.0 - LAMBDA_INIT)).astype(BF16)


def _attn_prompt_kernel(lq1, lk1, lq2, lk2, sg_ref, q_ref, k_ref, v_ref, o_ref, m_ref, acc_ref, *, tq):
    i = pl.program_id(2)
    nchunk = tq // LANES
    m_ref[...] = jnp.full(m_ref.shape, -jnp.inf, F32)
    acc_ref[...] = jnp.zeros(acc_ref.shape, F32)
    ones = jnp.ones((tq, LANES), BF16)
    qq = []
    for h in range(HEADS_PER_STEP):
        q1, q2 = _split_q(q_ref[:, h * HEAD_COLS:(h + 1) * HEAD_COLS])
        qq.append(jnp.concatenate([q1, q2], axis=0))

    def tile(kt, h, masked):
        start = pl.multiple_of(kt * tq, tq)
        s = _scores(qq[h], k_ref[pl.ds(start, tq), h * HEAD_COLS:(h + 1) * HEAD_COLS])
        chunks = [s[:, c * LANES:(c + 1) * LANES] for c in range(nchunk)]
        if masked:
            row = lax.broadcasted_iota(jnp.int32, (2 * tq, LANES), 0)
            lane = lax.broadcasted_iota(jnp.int32, (2 * tq, LANES), 1)
            qchunk = (row % tq) // CHUNK
            chunks = [jnp.where((lane + c * LANES) // CHUNK <= qchunk, x, NEG) for c, x in enumerate(chunks)]
        part = chunks[0]
        for x in chunks[1:]:
            part = jnp.maximum(part, x)
        m_prev = m_ref[h]
        m_next = jnp.maximum(m_prev, jnp.max(part, axis=1, keepdims=True))
        p = jnp.concatenate([jnp.exp(x - m_next).astype(BF16) for x in chunks], axis=1)
        alpha = jnp.exp(m_prev - m_next)
        v1 = jnp.concatenate([v_ref[pl.ds(start, tq), h * V_HEAD_DIM:(h + 1) * V_HEAD_DIM], ones], axis=1)
        acc_ref[h] = (jnp.concatenate([alpha, alpha], axis=1) * acc_ref[h]
                      + jnp.dot(p, v1, preferred_element_type=F32))
        m_ref[h] = m_next

    def body(kt, c):
        for h in range(HEADS_PER_STEP):
            tile(kt, h, False)
        return c

    lax.fori_loop(0, i, body, 0)
    lam = _lam(lq1, lk1, lq2, lk2)
    for h in range(HEADS_PER_STEP):
        tile(i, h, True)
    for h in range(HEADS_PER_STEP):
        o1 = acc_ref[h, :tq, :V_HEAD_DIM] / acc_ref[h, :tq, V_HEAD_DIM:]
        o2 = acc_ref[h, tq:, :V_HEAD_DIM] / acc_ref[h, tq:, V_HEAD_DIM:]
        o_ref[:, h * V_HEAD_DIM:(h + 1) * V_HEAD_DIM] = _subln(o1 - lam * o2, sg_ref[...])


def _attn_prompt(q, kb, vb, lams, subln_g, batch, seq, tq):
    n = q.shape[0]
    nq = seq // tq
    small = lambda b, h, i: (0, 0)
    qmap = lambda b, h, i: (b * nq + i, h)
    kvmap = lambda b, h, i: (b, h)
    return pl.pallas_call(
        functools.partial(_attn_prompt_kernel, tq=tq),
        grid=(batch, N_HEADS // HEADS_PER_STEP, nq),
        in_specs=[pl.BlockSpec((1, QK_HEAD_DIM), small)] * 4 + [
            pl.BlockSpec((1, V_HEAD_DIM), small),
            pl.BlockSpec((tq, HEADS_PER_STEP * HEAD_COLS), qmap),
            pl.BlockSpec((seq, HEADS_PER_STEP * HEAD_COLS), kvmap),
            pl.BlockSpec((seq, HEADS_PER_STEP * V_HEAD_DIM), kvmap)],
        out_specs=pl.BlockSpec((tq, HEADS_PER_STEP * V_HEAD_DIM), qmap),
        out_shape=jax.ShapeDtypeStruct((n, N_HEADS * V_HEAD_DIM), BF16),
        scratch_shapes=[pltpu.VMEM((HEADS_PER_STEP, 2 * tq, LANES), F32),
                        pltpu.VMEM((HEADS_PER_STEP, 2 * tq, 2 * LANES), F32)],
        compiler_params=_params("parallel", "parallel", "arbitrary"),
    )(*lams, subln_g, q, kb, vb)


def _attn_sample_kernel(lq1, lk1, lq2, lk2, sg_ref, q_ref, kn_ref, vn_ref, ck_ref, cv_ref, o_ref):
    lam = _lam(lq1, lk1, lq2, lk2)
    for h in range(N_HEADS):
        cols = slice(h * HEAD_COLS, (h + 1) * HEAD_COLS)
        q1, q2 = _split_q(q_ref[:, cols])
        kc = ck_ref[0, :, cols].astype(BF16)
        vc = cv_ref[0, :, cols].astype(BF16)
        kn = kn_ref[:, cols]
        vn = vn_ref[:, cols]

        def branch(qm):
            sc = _scores(qm, kc)
            sn = _scores(qm, kn)
            m = jnp.maximum(jnp.max(sc, axis=-1, keepdims=True), jnp.max(sn, axis=-1, keepdims=True))
            pc = jnp.exp(sc - m)
            pn = jnp.exp(sn - m)
            l = jnp.sum(pc, axis=-1, keepdims=True) + jnp.sum(pn, axis=-1, keepdims=True)
            acc = (jnp.dot(pc.astype(BF16), vc, preferred_element_type=F32)
                   + jnp.dot(pn.astype(BF16), vn, preferred_element_type=F32))
            return acc / l

        o = branch(q1) - lam * branch(q2)
        o_ref[:, cols] = _subln(o, sg_ref[...])


def _attn_sample(q, kb, vb, cache_k, cache_v, lams, subln_g, batch, t):
    past = cache_k.shape[1]
    small = lambda b: (0, 0)
    row = lambda b: (b, 0)
    cache = lambda b: (b, 0, 0)
    return pl.pallas_call(
        _attn_sample_kernel,
        grid=(batch,),
        in_specs=[pl.BlockSpec((1, QK_HEAD_DIM), small)] * 4 + [
            pl.BlockSpec((1, V_HEAD_DIM), small),
            pl.BlockSpec((t, D_MODEL), row),
            pl.BlockSpec((t, D_MODEL), row),
            pl.BlockSpec((t, D_MODEL), row),
            pl.BlockSpec((1, past, D_MODEL), cache),
            pl.BlockSpec((1, past, D_MODEL), cache)],
        out_specs=pl.BlockSpec((t, D_MODEL), row),
        out_shape=jax.ShapeDtypeStruct((batch * t, D_MODEL), BF16),
        compiler_params=_params("parallel"),
    )(*lams, subln_g, q, kb, vb, cache_k, cache_v)


def _rglru_kernel(xr_ref, gr_ref, cs_ref, h0_ref, cw_ref, cb_ref, wa_ref, ba_ref, wi_ref, bi_ref, lam_ref,
                  y_ref, nc_ref, hl_ref, xpad, hcar, a_s, b_s, h_s, *, tt):
    t = pl.program_id(1)
    pad = SUBLANES

    @pl.when(t == 0)
    def _():
        xpad[0:pad, :] = jnp.zeros((pad, D_RNN), F32)
        xpad[pad - (CONV_WIDTH - 1):pad, :] = cs_ref[0]
        hcar[...] = h0_ref[0]

    xpad[pad:pad + tt, :] = xr_ref[...]
    xc = cb_ref[...]
    for j in range(CONV_WIDTH):
        off = pad - (CONV_WIDTH - 1) + j
        xc = xc + xpad[off:off + tt, :] * cw_ref[j:j + 1, :]
    xcb = xc.astype(BF16)

    z = -lam_ref[...]
    softplus = jnp.maximum(z, 0.0) + jnp.log1p(jnp.exp(-jnp.abs(z)))
    for n in range(N_RNN_BLOCKS):
        cols = slice(n * RNN_BLOCK, (n + 1) * RNN_BLOCK)
        xs = xcb[:, cols]
        r = _sigmoid(jnp.dot(xs, wa_ref[n], preferred_element_type=F32) + ba_ref[:, cols])
        i = _sigmoid(jnp.dot(xs, wi_ref[n], preferred_element_type=F32) + bi_ref[:, cols])
        log_a = -RG_C * r * softplus[:, cols]
        a_s[:, cols] = jnp.exp(log_a)
        th = jnp.tanh(log_a)
        b_s[:, cols] = jnp.sqrt(-2.0 * th / (1.0 - th)) * (i * xc[:, cols])

    def step(s, h):
        h = a_s[pl.ds(s, 1), :] * h + b_s[pl.ds(s, 1), :]
        h_s[pl.ds(s, 1), :] = h
        return h

    h = lax.fori_loop(0, tt, step, hcar[...], unroll=8)
    hcar[...] = h
    y_ref[...] = (h_s[...] * jax.nn.gelu(gr_ref[...])).astype(BF16)
    nc_ref[0] = xpad[pad + tt - (CONV_WIDTH - 1):pad + tt, :]
    hl_ref[0] = h
    xpad[0:pad, :] = xpad[tt:tt + pad, :]


def _rglru(xr, gr, conv_state, h0, conv_w, conv_b, w_a_b, b_a, w_i_b, b_i, rg_lambda, batch, t, tt):
    nt = t // tt
    row = lambda b, s: (b * nt + s, 0)
    const2 = lambda b, s: (0, 0)
    const3 = lambda b, s: (0, 0, 0)
    per_b = lambda b, s: (b, 0, 0)
    vec = pl.BlockSpec((1, D_RNN), const2)
    blk = pl.BlockSpec((N_RNN_BLOCKS, RNN_BLOCK, RNN_BLOCK), const3)
    big = pltpu.VMEM((tt, D_RNN), F32)
    return pl.pallas_call(
        functools.partial(_rglru_kernel, tt=tt),
        grid=(batch, nt),
        in_specs=[pl.BlockSpec((tt, D_RNN), row), pl.BlockSpec((tt, D_RNN), row),
                  pl.BlockSpec((1, CONV_WIDTH - 1, D_RNN), per_b), pl.BlockSpec((1, 1, D_RNN), per_b),
                  pl.BlockSpec((CONV_WIDTH, D_RNN), const2), vec, blk, vec, blk, vec, vec],
        out_specs=[pl.BlockSpec((tt, D_RNN), row),
                   pl.BlockSpec((1, CONV_WIDTH - 1, D_RNN), per_b),
                   pl.BlockSpec((1, 1, D_RNN), per_b)],
        out_shape=[jax.ShapeDtypeStruct((batch * t, D_RNN), BF16),
                   jax.ShapeDtypeStruct((batch, CONV_WIDTH - 1, D_RNN), F32),
                   jax.ShapeDtypeStruct((batch, 1, D_RNN), F32)],
        scratch_shapes=[pltpu.VMEM((tt + SUBLANES, D_RNN), F32), pltpu.VMEM((1, D_RNN), F32), big, big, big],
        compiler_params=_params("parallel", "arbitrary"),
    )(xr, gr, conv_state, h0, conv_w, conv_b, w_a_b, b_a, w_i_b, b_i, rg_lambda)


def _first_lane_of_max(vals, valid, lane):
    masked = jnp.where(valid, vals, -jnp.inf)
    mx = jnp.max(masked, axis=-1, keepdims=True)
    idx = jnp.min(jnp.where(valid & (masked == mx), lane, ROUTE_LANES), axis=-1, keepdims=True)
    return mx, idx


def _route(logits):
    lane = lax.broadcasted_iota(jnp.int32, logits.shape, 1)
    is_group = lane < N_GROUPS
    gmax, gsel = _first_lane_of_max(logits, is_group, lane)
    gsum = jnp.sum(jnp.where(is_group, jnp.exp(logits - gmax), 0.0), axis=-1, keepdims=True)
    g_w = 1.0 / gsum
    expert = lane - EXPERT_LANE0
    in_group = (expert >= 0) & (expert < N_EXPERTS) & ((expert // EXPERTS_PER_GROUP) == gsel)
    v1, i1 = _first_lane_of_max(logits, in_group, lane)
    v2, i2 = _first_lane_of_max(logits, in_group & (lane != i1), lane)
    e2 = jnp.exp(v2 - v1)
    den = 1.0 + e2
    return jnp.where(lane == i1, (1.0 / den) * g_w, 0.0) + jnp.where(lane == i2, (e2 / den) * g_w, 0.0)


def _merge_kernel(at_ref, y_ref, gt_ref, x_ref, wa_ref, wr_ref, wo_ref, bg_ref, g2_ref, wc_ref, bc_ref,
                  x1_ref, xn_ref, cm_ref):
    pa = jnp.dot(at_ref[...], wa_ref[...], preferred_element_type=F32)
    pr = jnp.dot(y_ref[...], wr_ref[...], preferred_element_type=F32)
    g = _sigmoid(gt_ref[...] + bg_ref[...])
    merged = g[:, :D_MODEL] * pa + g[:, D_MODEL:] * pr
    x1 = x_ref[...] + jnp.dot(merged.astype(BF16), wo_ref[...], preferred_element_type=F32)
    x1_ref[...] = x1
    xn = _rms(x1, g2_ref[...]).astype(BF16)
    xn_ref[...] = xn
    cm_ref[...] = _route(jnp.dot(xn, wc_ref[...], preferred_element_type=F32) + bc_ref[...])


def _merge(attn, y_rnn, gates, x2d, wa_b, wr_b, wo_b, b_gate, g2, w_cat_b, b_cat, tm):
    n = x2d.shape[0]
    row = lambda i: (i, 0)
    const = lambda i: (0, 0)
    sq = pl.BlockSpec((D_MODEL, D_MODEL), const)
    return pl.pallas_call(
        _merge_kernel,
        grid=(n // tm,),
        in_specs=[pl.BlockSpec((tm, D_MODEL), row), pl.BlockSpec((tm, D_MODEL), row),
                  pl.BlockSpec((tm, 2 * D_MODEL), row), pl.BlockSpec((tm, D_MODEL), row),
                  sq, sq, sq, pl.BlockSpec((1, 2 * D_MODEL), const), pl.BlockSpec((1, D_MODEL), const),
                  pl.BlockSpec((D_MODEL, ROUTE_LANES), const), pl.BlockSpec((1, ROUTE_LANES), const)],
        out_specs=[pl.BlockSpec((tm, D_MODEL), row), pl.BlockSpec((tm, D_MODEL), row),
                   pl.BlockSpec((tm, ROUTE_LANES), row)],
        out_shape=[jax.ShapeDtypeStruct((n, D_MODEL), F32), jax.ShapeDtypeStruct((n, D_MODEL), BF16),
                   jax.ShapeDtypeStruct((n, ROUTE_LANES), F32)],
        compiler_params=_params("parallel"),
    )(attn, y_rnn, gates, x2d, wa_b, wr_b, wo_b, b_gate, g2, w_cat_b, b_cat)


def _route_tables(comb, tile):
    n = comb.shape[0]
    nt = n // tile
    c = comb[:, EXPERT_LANE0:EXPERT_LANE0 + N_EXPERTS]
    sel = (c != 0.0).reshape(nt, tile, N_EXPERTS)
    csum = jnp.cumsum(sel.astype(jnp.int32), axis=1)
    rank = jnp.where(sel, csum - 1, -1).astype(F32)
    counts = csum[:, -1, :].reshape(nt * N_EXPERTS)
    rank_t = rank.transpose(0, 2, 1).reshape(nt * N_EXPERTS, 1, tile)
    rank = jnp.pad(rank.reshape(n, N_EXPERTS), ((0, 0), (0, ROUTE_LANES - N_EXPERTS)), constant_values=-1.0)
    hi = c.astype(BF16)
    rest = c - hi.astype(F32)
    mid = rest.astype(BF16)
    lo = (rest - mid.astype(F32)).astype(BF16)
    c3 = jnp.pad(jnp.concatenate([hi, mid, lo], axis=1), ((0, 0), (0, ROUTE_LANES - 3 * N_EXPERTS)))
    return counts, rank, rank_t, c3


def _moe_kernel(cnt_ref, x_ref, c3_ref, rk_ref, rkt_ref, w1_ref, w3_ref, w2_ref, y_ref, *, tile):
    t = pl.program_id(0)
    e = pl.program_id(1)

    @pl.when(e == 0)
    def _():
        y_ref[...] = jnp.zeros(y_ref.shape, F32)

    count = cnt_ref[t * N_EXPERTS + e]
    lane = lax.broadcasted_iota(jnp.int32, (tile, ROUTE_LANES), 1)
    rank_col = jnp.sum(jnp.where(lane == e, rk_ref[...], 0.0), axis=1, keepdims=True)
    rank_row = rkt_ref[0]
    lane_r = lax.broadcasted_iota(jnp.int32, (MOE_ROWS, ROUTE_LANES), 1)
    mine = (lane_r % N_EXPERTS == e) & (lane_r < 3 * N_EXPERTS)

    def block(b, carry):
        base = (b * MOE_ROWS).astype(F32)
        slot = lax.broadcasted_iota(jnp.int32, (MOE_ROWS, tile), 0).astype(F32) + base
        onehot = jnp.where(rank_row == slot, 1.0, 0.0).astype(BF16)
        xc = jnp.dot(onehot, x_ref[...], preferred_element_type=F32).astype(BF16)
        c3 = jnp.dot(onehot, c3_ref[...], preferred_element_type=F32)
        cw = jnp.sum(jnp.where(mine, c3, 0.0), axis=1, keepdims=True)
        a = jnp.dot(xc, w1_ref[0], preferred_element_type=F32)
        h = (a * _sigmoid(a)) * jnp.dot(xc, w3_ref[0], preferred_element_type=F32)
        yw = (cw * jnp.dot(h.astype(BF16), w2_ref[0], preferred_element_type=F32)).astype(BF16)
        slot_t = lax.broadcasted_iota(jnp.int32, (tile, MOE_ROWS), 1).astype(F32) + base
        onehot_t = jnp.where(rank_col == slot_t, 1.0, 0.0).astype(BF16)
        y_ref[...] += jnp.dot(onehot_t, yw, preferred_element_type=F32)
        return carry

    lax.fori_loop(0, (count + MOE_ROWS - 1) // MOE_ROWS, block, 0)


def _moe(xn2, comb, w1_b, w3_b, w2_b, tile):
    n = xn2.shape[0]
    counts, rank, rank_t, c3 = _route_tables(comb, tile)
    row = lambda t, e, cnt: (t, 0)
    per_e = lambda t, e, cnt: (e, 0, 0)
    return pl.pallas_call(
        functools.partial(_moe_kernel, tile=tile),
        grid_spec=pltpu.PrefetchScalarGridSpec(
            num_scalar_prefetch=1,
            grid=(n // tile, N_EXPERTS),
            in_specs=[pl.BlockSpec((tile, D_MODEL), row), pl.BlockSpec((tile, ROUTE_LANES), row),
                      pl.BlockSpec((tile, ROUTE_LANES), row),
                      pl.BlockSpec((1, 1, tile), lambda t, e, cnt: (t * N_EXPERTS + e, 0, 0)),
                      pl.BlockSpec((1, D_MODEL, D_EXPERT), per_e), pl.BlockSpec((1, D_MODEL, D_EXPERT), per_e),
                      pl.BlockSpec((1, D_EXPERT, D_MODEL), per_e)],
            out_specs=pl.BlockSpec((tile, D_MODEL), row)),
        out_shape=jax.ShapeDtypeStruct((n, D_MODEL), F32),
        compiler_params=_params("parallel", "arbitrary"),
    )(counts, xn2, c3, rank, rank_t, w1_b, w3_b, w2_b)


def _final_kernel(x1_ref, y_ref, g_ref, o_ref):
    o_ref[...] = _rms(x1_ref[...] + y_ref[...], g_ref[...])


def _final(x1, y, gf, tm):
    n = x1.shape[0]
    row = lambda i: (i, 0)
    return pl.pallas_call(
        _final_kernel,
        grid=(n // tm,),
        in_specs=[pl.BlockSpec((tm, D_MODEL), row), pl.BlockSpec((tm, D_MODEL), row),
                  pl.BlockSpec((1, D_MODEL), lambda i: (0, 0))],
        out_specs=pl.BlockSpec((tm, D_MODEL), row),
        out_shape=jax.ShapeDtypeStruct((n, D_MODEL), F32),
        compiler_params=_params("parallel"),
    )(x1, y, gf)


def _stream(x, past, w, *, tm_in, tm_merge, tm_moe, tq, tt):
    batch, t, _ = x.shape
    n = batch * t
    x2d = x.reshape(n, D_MODEL)
    q, k, v, kb, vb, xr, gr, gates = _in_proj(x2d, w["g1"], w["w_in"], tm_in)
    if past is None:
        attn = _attn_prompt(q, kb, vb, w["lams"], w["subln_g"], batch, t, tq)
        conv_state = jnp.zeros((batch, CONV_WIDTH - 1, D_RNN), F32)
        h0 = jnp.zeros((batch, 1, D_RNN), F32)
    else:
        cache_k, cache_v, conv_state, h0 = past
        attn = _attn_sample(q, kb, vb, cache_k, cache_v, w["lams"], w["subln_g"], batch, t)
    y_rnn, new_conv, h_last = _rglru(xr, gr, conv_state, h0, w["conv_w"], w["conv_b"], w["w_a"], w["b_a"],
                                     w["w_i"], w["b_i"], w["rg_lambda"], batch, t, tt)
    x1, xn2, comb = _merge(attn, y_rnn, gates, x2d, w["w_attn"], w["w_rnn"], w["w_out"], w["b_gate"],
                           w["g2"], w["w_cat"], w["b_cat"], tm_merge)
    y = _final(x1, _moe(xn2, comb, w["w1"], w["w3"], w["w2"], tm_moe), w["gf"], tm_merge)
    return (y.reshape(batch, t, D_MODEL),
            k.reshape(1, batch, t, N_HEADS, 2, QK_HEAD_DIM),
            v.reshape(1, batch, t, N_HEADS, V_HEAD_DIM),
            new_conv.reshape(1, batch, CONV_WIDTH - 1, D_RNN),
            h_last.reshape(1, batch, D_RNN))


def kernel(x_prompt, x_sample, cache_k, cache_v, state_conv, state_rnn, norm1_g, w_in, lambda_q1, lambda_k1, lambda_q2, lambda_k2, subln_g, w_attn_proj, conv_w, conv_b, w_rg_a, b_rg_a, w_rg_i, b_rg_i, rg_lambda, w_rnn_proj, b_gate, w_out, norm2_g, w_group, b_group, w_router, b_router, w1, w3, w2, final_norm_g):
    assert norm1_g.shape[0] == 1, "single-layer model"
    dec_batch, past_len = cache_k.shape[1], cache_k.shape[2]
    pad = ROUTE_LANES - N_GROUPS - N_EXPERTS
    w = {
        "g1": norm1_g, "g2": norm2_g, "gf": final_norm_g.reshape(1, D_MODEL),
        "w_in": w_in[0].astype(BF16),
        "lams": (lambda_q1, lambda_k1, lambda_q2, lambda_k2),
        "subln_g": subln_g,
        "w_attn": w_attn_proj[0].astype(BF16), "w_rnn": w_rnn_proj[0].astype(BF16),
        "w_out": w_out[0].astype(BF16),
        "conv_w": conv_w[0], "conv_b": conv_b,
        "w_a": w_rg_a[0].astype(BF16), "b_a": b_rg_a, "w_i": w_rg_i[0].astype(BF16), "b_i": b_rg_i,
        "rg_lambda": rg_lambda, "b_gate": b_gate,
        "w_cat": jnp.pad(jnp.concatenate([w_group[0], w_router[0]], axis=1), ((0, 0), (0, pad))).astype(BF16),
        "b_cat": jnp.pad(jnp.concatenate([b_group[0], b_router[0]]), (0, pad)).reshape(1, ROUTE_LANES),
        "w1": w1[0].astype(BF16), "w3": w3[0].astype(BF16), "w2": w2[0].astype(BF16),
    }
    yp, kp, vp, cp, hp = _stream(x_prompt, None, w, tm_in=256, tm_merge=256, tm_moe=2048, tq=512, tt=256)
    past = (cache_k[0].reshape(dec_batch, past_len, D_MODEL), cache_v[0].reshape(dec_batch, past_len, D_MODEL),
            state_conv[0], state_rnn[0].reshape(dec_batch, 1, D_RNN))
    dec_t = x_sample.shape[1]
    ys, ks, vs, cs, hs = _stream(x_sample, past, w, tm_in=256, tm_merge=256, tm_moe=256, tq=dec_t, tt=dec_t)
    return (yp, ys, kp, vp, cp, hp, ks, vs, cs, hs)
```

```python
import functools
import math

import jax
import jax.numpy as jnp
from jax import lax
from jax.experimental import pallas as pl
from jax.experimental.pallas import tpu as pltpu

F32 = jnp.float32
BF16 = jnp.bfloat16

D_MODEL = 1024
CHUNK = 64
N_HEADS = 8
QK_HEAD_DIM = 64
V_HEAD_DIM = 128
HEAD_COLS = 2 * QK_HEAD_DIM
D_RNN = 1024
N_RNN_BLOCKS = 8
RNN_BLOCK = D_RNN // N_RNN_BLOCKS
CONV_WIDTH = 4
RG_C = 8.0
N_GROUPS = 4
EXPERTS_PER_GROUP = 4
N_EXPERTS = N_GROUPS * EXPERTS_PER_GROUP
D_EXPERT = 512
EPS = 1e-6
LAMBDA_INIT = 0.8 - 0.6 * math.exp(-0.3 * 0)
N_SEG = 7
IN_WIDTH = N_SEG * D_MODEL
ROUTE_LANES = 128
MOE_ROWS = 256
EXPERT_LANE0 = N_GROUPS
SUBLANES = 8
LANES = 128
HEADS_PER_STEP = 2
VMEM_LIMIT = 56 * 1024 * 1024
NEG = float(jnp.finfo(jnp.float32).min)


def _params(*sem):
    return pltpu.CompilerParams(dimension_semantics=sem, vmem_limit_bytes=VMEM_LIMIT)


def _rms(x, g):
    return x * lax.rsqrt(jnp.mean(x * x, axis=-1, keepdims=True) + EPS) * g


def _sigmoid(x):
    return 1.0 / (1.0 + jnp.exp(-x))


def _in_proj_kernel(x_ref, g_ref, w_ref, wkt_ref, q_ref, k_ref, v_ref, kb_ref, vb_ref, xr_ref, gr_ref, gt_ref,
                    *, k_transposed):
    xn = _rms(x_ref[...], g_ref[...]).astype(BF16)

    def seg(j):
        return jnp.dot(xn, w_ref[:, j * D_MODEL:(j + 1) * D_MODEL], preferred_element_type=F32)

    q_ref[...] = (seg(0) * (QK_HEAD_DIM ** -0.5)).astype(BF16)
    if k_transposed:
        kt = lax.dot_general(wkt_ref[...], xn, (((1,), (1,)), ((), ())), preferred_element_type=F32)
        k_ref[0] = kt
        kb_ref[0, 0] = kt.astype(BF16)
    else:
        k = seg(1)
        k_ref[...] = k
        kb_ref[...] = k.astype(BF16)
    v = seg(2)
    v_ref[...] = v
    vb_ref[...] = v.astype(BF16)
    xr_ref[...] = seg(3)
    gr_ref[...] = seg(4)
    gt_ref[:, :D_MODEL] = seg(5)
    gt_ref[:, D_MODEL:] = seg(6)


def _in_proj(x2d, g1, w_in_b, w_kt_b, tm, seq=None):
    n = x2d.shape[0]
    row = lambda i: (i, 0)
    const = lambda i: (0, 0)
    wide = lambda dt: jax.ShapeDtypeStruct((n, D_MODEL), dt)
    wide_spec = pl.BlockSpec((tm, D_MODEL), row)
    if seq is None:
        k_shape, k_spec, kb_shape, kb_spec = wide(F32), wide_spec, wide(BF16), wide_spec
    else:
        nt = seq // tm
        k_shape = jax.ShapeDtypeStruct((n // seq, D_MODEL, seq), F32)
        k_spec = pl.BlockSpec((1, D_MODEL, tm), lambda i: (i // nt, 0, i % nt))
        kb_shape = jax.ShapeDtypeStruct((n // seq, nt, D_MODEL, tm), BF16)
        kb_spec = pl.BlockSpec((1, 1, D_MODEL, tm), lambda i: (i // nt, i % nt, 0, 0))
    return pl.pallas_call(
        functools.partial(_in_proj_kernel, k_transposed=seq is not None),
        grid=(n // tm,),
        in_specs=[wide_spec,
                  pl.BlockSpec((1, D_MODEL), const),
                  pl.BlockSpec((D_MODEL, IN_WIDTH), const, pipeline_mode=pl.Buffered(1)),
                  pl.BlockSpec((D_MODEL, D_MODEL), const, pipeline_mode=pl.Buffered(1))],
        out_specs=[wide_spec, k_spec, wide_spec, kb_spec, wide_spec, wide_spec, wide_spec,
                   pl.BlockSpec((tm, 2 * D_MODEL), row)],
        out_shape=[wide(BF16), k_shape, wide(F32), kb_shape, wide(BF16), wide(F32), wide(F32),
                   jax.ShapeDtypeStruct((n, 2 * D_MODEL), F32)],
        compiler_params=_params("parallel"),
    )(x2d, g1, w_in_b, w_kt_b)


def _lam(lq1, lk1, lq2, lk2):
    return (jnp.exp(jnp.sum(lq1[...] * lk1[...], axis=-1, keepdims=True))
            - jnp.exp(jnp.sum(lq2[...] * lk2[...], axis=-1, keepdims=True)) + LAMBDA_INIT)


def _split_q(q):
    lane = lax.broadcasted_iota(jnp.int32, q.shape, 1)
    zero = jnp.zeros_like(q)
    return jnp.where(lane < QK_HEAD_DIM, q, zero), jnp.where(lane >= QK_HEAD_DIM, q, zero)


def _scores(qm, k):
    return lax.dot_general(qm, k, (((1,), (1,)), ((), ())), preferred_element_type=F32)


def _subln(o, g):
    return (_rms(o, g) * (1.0 - LAMBDA_INIT)).astype(BF16)


def _attn_prompt_kernel(lq1, lk1, lq2, lk2, sg_ref, q_ref, k_ref, v_ref, o_ref, m_ref, acc_ref, *, tq, kw):
    i = pl.program_id(2)
    m_ref[...] = jnp.full(m_ref.shape, -jnp.inf, F32)
    acc_ref[...] = jnp.zeros(acc_ref.shape, F32)
    ones = jnp.ones((tq, LANES), BF16)
    qq = []
    for h in range(HEADS_PER_STEP):
        q1, q2 = _split_q(q_ref[:, h * HEAD_COLS:(h + 1) * HEAD_COLS])
        qq.append(jnp.concatenate([q1, q2], axis=0))

    def tile(kt, h, masked):
        start = pl.multiple_of(kt * tq, tq)
        chunks = []
        for j in range(tq // kw):
            kt_blk = k_ref[0, kt * (tq // kw) + j, h * HEAD_COLS:(h + 1) * HEAD_COLS, :]
            s = jnp.dot(qq[h], kt_blk, preferred_element_type=F32)
            chunks += [s[:, c * LANES:(c + 1) * LANES] for c in range(kw // LANES)]
        if masked:
            row = lax.broadcasted_iota(jnp.int32, (2 * tq, LANES), 0)
            lane = lax.broadcasted_iota(jnp.int32, (2 * tq, LANES), 1)
            qchunk = (row % tq) // CHUNK
            chunks = [jnp.where((lane + c * LANES) // CHUNK <= qchunk, x, NEG) for c, x in enumerate(chunks)]
        part = chunks[0]
        for x in chunks[1:]:
            part = jnp.maximum(part, x)
        m_prev = m_ref[h]
        m_next = jnp.maximum(m_prev, jnp.max(part, axis=1, keepdims=True))
        p = jnp.concatenate([jnp.exp(x - m_next).astype(BF16) for x in chunks], axis=1)
        alpha = jnp.exp(m_prev - m_next)
        v1 = jnp.concatenate([v_ref[pl.ds(start, tq), h * V_HEAD_DIM:(h + 1) * V_HEAD_DIM], ones], axis=1)
        acc_ref[h] = (jnp.concatenate([alpha, alpha], axis=1) * acc_ref[h]
                      + jnp.dot(p, v1, preferred_element_type=F32))
        m_ref[h] = m_next

    def body(kt, c):
        for h in range(HEADS_PER_STEP):
            tile(kt, h, False)
        return c

    lax.fori_loop(0, i, body, 0)
    lam = _lam(lq1, lk1, lq2, lk2)
    for h in range(HEADS_PER_STEP):
        tile(i, h, True)
    for h in range(HEADS_PER_STEP):
        o1 = acc_ref[h, :tq, :V_HEAD_DIM] / acc_ref[h, :tq, V_HEAD_DIM:]
        o2 = acc_ref[h, tq:, :V_HEAD_DIM] / acc_ref[h, tq:, V_HEAD_DIM:]
        o_ref[:, h * V_HEAD_DIM:(h + 1) * V_HEAD_DIM] = _subln(o1 - lam * o2, sg_ref[...])


def _attn_prompt(q, kbt, vb, lams, subln_g, batch, seq, tq):
    n = q.shape[0]
    nq = seq // tq
    nkb, kw = kbt.shape[1], kbt.shape[3]
    small = lambda b, h, i: (0, 0)
    qmap = lambda b, h, i: (b * nq + i, h)
    kvmap = lambda b, h, i: (b, h)
    return pl.pallas_call(
        functools.partial(_attn_prompt_kernel, tq=tq, kw=kw),
        grid=(batch, N_HEADS // HEADS_PER_STEP, nq),
        in_specs=[pl.BlockSpec((1, QK_HEAD_DIM), small)] * 4 + [
            pl.BlockSpec((1, V_HEAD_DIM), small),
            pl.BlockSpec((tq, HEADS_PER_STEP * HEAD_COLS), qmap),
            pl.BlockSpec((1, nkb, HEADS_PER_STEP * HEAD_COLS, kw), lambda b, h, i: (b, 0, h, 0)),
            pl.BlockSpec((seq, HEADS_PER_STEP * V_HEAD_DIM), kvmap)],
        out_specs=pl.BlockSpec((tq, HEADS_PER_STEP * V_HEAD_DIM), qmap),
        out_shape=jax.ShapeDtypeStruct((n, N_HEADS * V_HEAD_DIM), BF16),
        scratch_shapes=[pltpu.VMEM((HEADS_PER_STEP, 2 * tq, LANES), F32),
                        pltpu.VMEM((HEADS_PER_STEP, 2 * tq, 2 * LANES), F32)],
        compiler_params=_params("parallel", "parallel", "arbitrary"),
    )(*lams, subln_g, q, kbt, vb)


def _attn_sample_kernel(lq1, lk1, lq2, lk2, sg_ref, q_ref, kn_ref, vn_ref, ck_ref, cv_ref, o_ref, *, past):
    lam = _lam(lq1, lk1, lq2, lk2)
    for h in range(N_HEADS):
        cols = slice(h * HEAD_COLS, (h + 1) * HEAD_COLS)
        q1, q2 = _split_q(q_ref[:, cols])
        kct = ck_ref[0, h * HEAD_COLS:(h + 1) * HEAD_COLS, :].astype(BF16)
        vc = cv_ref[0, pl.ds(h, past, stride=N_HEADS), :].astype(BF16)
        kn = kn_ref[:, cols]
        vn = vn_ref[:, cols]

        def branch(qm):
            sc = jnp.dot(qm, kct, preferred_element_type=F32)
            sn = _scores(qm, kn)
            m = jnp.maximum(jnp.max(sc, axis=-1, keepdims=True), jnp.max(sn, axis=-1, keepdims=True))
            pc = jnp.exp(sc - m)
            pn = jnp.exp(sn - m)
            l = jnp.sum(pc, axis=-1, keepdims=True) + jnp.sum(pn, axis=-1, keepdims=True)
            acc = (jnp.dot(pc.astype(BF16), vc, preferred_element_type=F32)
                   + jnp.dot(pn.astype(BF16), vn, preferred_element_type=F32))
            return acc / l

        o = branch(q1) - lam * branch(q2)
        o_ref[:, cols] = _subln(o, sg_ref[...])


def _attn_sample(q, kb, vb, cache_kt, cache_v, lams, subln_g, batch, t):
    past = cache_kt.shape[2]
    small = lambda b: (0, 0)
    row = lambda b: (b, 0)
    cache = lambda b: (b, 0, 0)
    return pl.pallas_call(
        functools.partial(_attn_sample_kernel, past=past),
        grid=(batch,),
        in_specs=[pl.BlockSpec((1, QK_HEAD_DIM), small)] * 4 + [
            pl.BlockSpec((1, V_HEAD_DIM), small),
            pl.BlockSpec((t, D_MODEL), row),
            pl.BlockSpec((t, D_MODEL), row),
            pl.BlockSpec((t, D_MODEL), row),
            pl.BlockSpec((1, D_MODEL, past), cache),
            pl.BlockSpec((1, past * N_HEADS, V_HEAD_DIM), cache)],
        out_specs=pl.BlockSpec((t, D_MODEL), row),
        out_shape=jax.ShapeDtypeStruct((batch * t, D_MODEL), BF16),
        compiler_params=_params("parallel"),
    )(*lams, subln_g, q, kb, vb, cache_kt, cache_v)


def _rglru_kernel(xr_ref, gr_ref, cs_ref, h0_ref, cw_ref, cb_ref, wa_ref, ba_ref, wi_ref, bi_ref, lam_ref,
                  y_ref, nc_ref, hl_ref, xpad, hcar, a_s, b_s, h_s, *, tt):
    t = pl.program_id(1)
    pad = SUBLANES

    @pl.when(t == 0)
    def _():
        xpad[0:pad, :] = jnp.zeros((pad, D_RNN), F32)
        xpad[pad - (CONV_WIDTH - 1):pad, :] = cs_ref[0]
        hcar[...] = h0_ref[0]

    xpad[pad:pad + tt, :] = xr_ref[...]
    xc = cb_ref[...]
    for j in range(CONV_WIDTH):
        off = pad - (CONV_WIDTH - 1) + j
        xc = xc + xpad[off:off + tt, :] * cw_ref[j:j + 1, :]
    xcb = xc.astype(BF16)

    z = -lam_ref[...]
    softplus = jnp.maximum(z, 0.0) + jnp.log1p(jnp.exp(-jnp.abs(z)))
    for n in range(N_RNN_BLOCKS):
        cols = slice(n * RNN_BLOCK, (n + 1) * RNN_BLOCK)
        xs = xcb[:, cols]
        r = _sigmoid(jnp.dot(xs, wa_ref[n], preferred_element_type=F32) + ba_ref[:, cols])
        i = _sigmoid(jnp.dot(xs, wi_ref[n], preferred_element_type=F32) + bi_ref[:, cols])
        log_a = -RG_C * r * softplus[:, cols]
        a_s[:, cols] = jnp.exp(log_a)
        th = jnp.tanh(log_a)
        b_s[:, cols] = jnp.sqrt(-2.0 * th / (1.0 - th)) * (i * xc[:, cols])

    def step(s, h):
        h = a_s[pl.ds(s, 1), :] * h + b_s[pl.ds(s, 1), :]
        h_s[pl.ds(s, 1), :] = h
        return h

    h = lax.fori_loop(0, tt, step, hcar[...], unroll=8)
    hcar[...] = h
    y_ref[...] = (h_s[...] * jax.nn.gelu(gr_ref[...])).astype(BF16)
    nc_ref[0] = xpad[pad + tt - (CONV_WIDTH - 1):pad + tt, :]
    hl_ref[0] = h
    xpad[0:pad, :] = xpad[tt:tt + pad, :]


def _rglru(xr, gr, conv_state, h0, conv_w, conv_b, w_a_b, b_a, w_i_b, b_i, rg_lambda, batch, t, tt):
    nt = t // tt
    row = lambda b, s: (b * nt + s, 0)
    const2 = lambda b, s: (0, 0)
    const3 = lambda b, s: (0, 0, 0)
    per_b = lambda b, s: (b, 0, 0)
    vec = pl.BlockSpec((1, D_RNN), const2)
    blk = pl.BlockSpec((N_RNN_BLOCKS, RNN_BLOCK, RNN_BLOCK), const3)
    big = pltpu.VMEM((tt, D_RNN), F32)
    return pl.pallas_call(
        functools.partial(_rglru_kernel, tt=tt),
        grid=(batch, nt),
        in_specs=[pl.BlockSpec((tt, D_RNN), row), pl.BlockSpec((tt, D_RNN), row),
                  pl.BlockSpec((1, CONV_WIDTH - 1, D_RNN), per_b), pl.BlockSpec((1, 1, D_RNN), per_b),
                  pl.BlockSpec((CONV_WIDTH, D_RNN), const2), vec, blk, vec, blk, vec, vec],
        out_specs=[pl.BlockSpec((tt, D_RNN), row),
                   pl.BlockSpec((1, CONV_WIDTH - 1, D_RNN), per_b),
                   pl.BlockSpec((1, 1, D_RNN), per_b)],
        out_shape=[jax.ShapeDtypeStruct((batch * t, D_RNN), BF16),
                   jax.ShapeDtypeStruct((batch, CONV_WIDTH - 1, D_RNN), F32),
                   jax.ShapeDtypeStruct((batch, 1, D_RNN), F32)],
        scratch_shapes=[pltpu.VMEM((tt + SUBLANES, D_RNN), F32), pltpu.VMEM((1, D_RNN), F32), big, big, big],
        compiler_params=_params("parallel", "arbitrary"),
    )(xr, gr, conv_state, h0, conv_w, conv_b, w_a_b, b_a, w_i_b, b_i, rg_lambda)


def _first_lane_of_max(vals, valid, lane):
    masked = jnp.where(valid, vals, -jnp.inf)
    mx = jnp.max(masked, axis=-1, keepdims=True)
    idx = jnp.min(jnp.where(valid & (masked == mx), lane, ROUTE_LANES), axis=-1, keepdims=True)
    return mx, idx


def _route(logits):
    lane = lax.broadcasted_iota(jnp.int32, logits.shape, 1)
    is_group = lane < N_GROUPS
    gmax, gsel = _first_lane_of_max(logits, is_group, lane)
    gsum = jnp.sum(jnp.where(is_group, jnp.exp(logits - gmax), 0.0), axis=-1, keepdims=True)
    g_w = 1.0 / gsum
    expert = lane - EXPERT_LANE0
    in_group = (expert >= 0) & (expert < N_EXPERTS) & ((expert // EXPERTS_PER_GROUP) == gsel)
    v1, i1 = _first_lane_of_max(logits, in_group, lane)
    v2, i2 = _first_lane_of_max(logits, in_group & (lane != i1), lane)
    e2 = jnp.exp(v2 - v1)
    den = 1.0 + e2
    return jnp.where(lane == i1, (1.0 / den) * g_w, 0.0) + jnp.where(lane == i2, (e2 / den) * g_w, 0.0)


def _merge_kernel(at_ref, y_ref, gt_ref, x_ref, wa_ref, wr_ref, wo_ref, bg_ref, g2_ref, wc_ref, bc_ref,
                  x1_ref, xn_ref, cm_ref):
    pa = jnp.dot(at_ref[...], wa_ref[...], preferred_element_type=F32)
    pr = jnp.dot(y_ref[...], wr_ref[...], preferred_element_type=F32)
    g = _sigmoid(gt_ref[...] + bg_ref[...])
    merged = g[:, :D_MODEL] * pa + g[:, D_MODEL:] * pr
    x1 = x_ref[...] + jnp.dot(merged.astype(BF16), wo_ref[...], preferred_element_type=F32)
    x1_ref[...] = x1
    xn = _rms(x1, g2_ref[...]).astype(BF16)
    xn_ref[...] = xn
    cm_ref[...] = _route(jnp.dot(xn, wc_ref[...], preferred_element_type=F32) + bc_ref[...])


def _merge(attn, y_rnn, gates, x2d, wa_b, wr_b, wo_b, b_gate, g2, w_cat_b, b_cat, tm):
    n = x2d.shape[0]
    row = lambda i: (i, 0)
    const = lambda i: (0, 0)
    sq = pl.BlockSpec((D_MODEL, D_MODEL), const)
    return pl.pallas_call(
        _merge_kernel,
        grid=(n // tm,),
        in_specs=[pl.BlockSpec((tm, D_MODEL), row), pl.BlockSpec((tm, D_MODEL), row),
                  pl.BlockSpec((tm, 2 * D_MODEL), row), pl.BlockSpec((tm, D_MODEL), row),
                  sq, sq, sq, pl.BlockSpec((1, 2 * D_MODEL), const), pl.BlockSpec((1, D_MODEL), const),
                  pl.BlockSpec((D_MODEL, ROUTE_LANES), const), pl.BlockSpec((1, ROUTE_LANES), const)],
        out_specs=[pl.BlockSpec((tm, D_MODEL), row), pl.BlockSpec((tm, D_MODEL), row),
                   pl.BlockSpec((tm, ROUTE_LANES), row)],
        out_shape=[jax.ShapeDtypeStruct((n, D_MODEL), F32), jax.ShapeDtypeStruct((n, D_MODEL), BF16),
                   jax.ShapeDtypeStruct((n, ROUTE_LANES), F32)],
        compiler_params=_params("parallel"),
    )(attn, y_rnn, gates, x2d, wa_b, wr_b, wo_b, b_gate, g2, w_cat_b, b_cat)


def _route_tables(comb, tile):
    n = comb.shape[0]
    nt = n // tile
    c = comb[:, EXPERT_LANE0:EXPERT_LANE0 + N_EXPERTS]
    sel = (c != 0.0).reshape(nt, tile, N_EXPERTS)
    csum = jnp.cumsum(sel.astype(jnp.int32), axis=1)
    rank = jnp.where(sel, csum - 1, -1).astype(F32)
    counts = csum[:, -1, :].reshape(nt * N_EXPERTS)
    rank_t = rank.transpose(0, 2, 1).reshape(nt * N_EXPERTS, 1, tile)
    rank = jnp.pad(rank.reshape(n, N_EXPERTS), ((0, 0), (0, ROUTE_LANES - N_EXPERTS)), constant_values=-1.0)
    hi = c.astype(BF16)
    rest = c - hi.astype(F32)
    mid = rest.astype(BF16)
    lo = (rest - mid.astype(F32)).astype(BF16)
    c3 = jnp.pad(jnp.concatenate([hi, mid, lo], axis=1), ((0, 0), (0, ROUTE_LANES - 3 * N_EXPERTS)))
    return counts, rank, rank_t, c3


def _moe_kernel(cnt_ref, x_ref, c3_ref, rk_ref, rkt_ref, w1_ref, w3_ref, w2_ref, y_ref, *, tile):
    t = pl.program_id(0)
    e = pl.program_id(1)

    @pl.when(e == 0)
    def _():
        y_ref[...] = jnp.zeros(y_ref.shape, F32)

    count = cnt_ref[t * N_EXPERTS + e]
    lane = lax.broadcasted_iota(jnp.int32, (tile, ROUTE_LANES), 1)
    rank_col = jnp.sum(jnp.where(lane == e, rk_ref[...], 0.0), axis=1, keepdims=True)
    rank_row = rkt_ref[0]
    lane_r = lax.broadcasted_iota(jnp.int32, (MOE_ROWS, ROUTE_LANES), 1)
    mine = (lane_r % N_EXPERTS == e) & (lane_r < 3 * N_EXPERTS)

    def block(b, carry):
        base = (b * MOE_ROWS).astype(F32)
        slot = lax.broadcasted_iota(jnp.int32, (MOE_ROWS, tile), 0).astype(F32) + base
        onehot = jnp.where(rank_row == slot, 1.0, 0.0).astype(BF16)
        xc = jnp.dot(onehot, x_ref[...], preferred_element_type=F32).astype(BF16)
        c3 = jnp.dot(onehot, c3_ref[...], preferred_element_type=F32)
        cw = jnp.sum(jnp.where(mine, c3, 0.0), axis=1, keepdims=True)
        a = jnp.dot(xc, w1_ref[0], preferred_element_type=F32)
        h = (a * _sigmoid(a)) * jnp.dot(xc, w3_ref[0], preferred_element_type=F32)
        yw = (cw * jnp.dot(h.astype(BF16), w2_ref[0], preferred_element_type=F32)).astype(BF16)
        slot_t = lax.broadcasted_iota(jnp.int32, (tile, MOE_ROWS), 1).astype(F32) + base
        onehot_t = jnp.where(rank_col == slot_t, 1.0, 0.0).astype(BF16)
        y_ref[...] += jnp.dot(onehot_t, yw, preferred_element_type=F32)
        return carry

    lax.fori_loop(0, (count + MOE_ROWS - 1) // MOE_ROWS, block, 0)


def _moe(xn2, comb, w1_b, w3_b, w2_b, tile):
    n = xn2.shape[0]
    counts, rank, rank_t, c3 = _route_tables(comb, tile)
    row = lambda t, e, cnt: (t, 0)
    per_e = lambda t, e, cnt: (e, 0, 0)
    return pl.pallas_call(
        functools.partial(_moe_kernel, tile=tile),
        grid_spec=pltpu.PrefetchScalarGridSpec(
            num_scalar_prefetch=1,
            grid=(n // tile, N_EXPERTS),
            in_specs=[pl.BlockSpec((tile, D_MODEL), row), pl.BlockSpec((tile, ROUTE_LANES), row),
                      pl.BlockSpec((tile, ROUTE_LANES), row),
                      pl.BlockSpec((1, 1, tile), lambda t, e, cnt: (t * N_EXPERTS + e, 0, 0)),
                      pl.BlockSpec((1, D_MODEL, D_EXPERT), per_e), pl.BlockSpec((1, D_MODEL, D_EXPERT), per_e),
                      pl.BlockSpec((1, D_EXPERT, D_MODEL), per_e)],
            out_specs=pl.BlockSpec((tile, D_MODEL), row)),
        out_shape=jax.ShapeDtypeStruct((n, D_MODEL), F32),
        compiler_params=_params("parallel", "arbitrary"),
    )(counts, xn2, c3, rank, rank_t, w1_b, w3_b, w2_b)


def _final_kernel(x1_ref, y_ref, g_ref, o_ref):
    o_ref[...] = _rms(x1_ref[...] + y_ref[...], g_ref[...])


def _final(x1, y, gf, tm):
    n = x1.shape[0]
    row = lambda i: (i, 0)
    return pl.pallas_call(
        _final_kernel,
        grid=(n // tm,),
        in_specs=[pl.BlockSpec((tm, D_MODEL), row), pl.BlockSpec((tm, D_MODEL), row),
                  pl.BlockSpec((1, D_MODEL), lambda i: (0, 0))],
        out_specs=pl.BlockSpec((tm, D_MODEL), row),
        out_shape=jax.ShapeDtypeStruct((n, D_MODEL), F32),
        compiler_params=_params("parallel"),
    )(x1, y, gf)


def _stream(x, past, w, *, tm_in, tm_merge, tm_moe, tq, tt):
    batch, t, _ = x.shape
    n = batch * t
    x2d = x.reshape(n, D_MODEL)
    q, k, v, kb, vb, xr, gr, gates = _in_proj(x2d, w["g1"], w["w_in"], w["w_kt"], tm_in,
                                              seq=t if past is None else None)
    if past is None:
        attn = _attn_prompt(q, kb, vb, w["lams"], w["subln_g"], batch, t, tq)
        k = k.reshape(batch, N_HEADS, 2, QK_HEAD_DIM, t).transpose(0, 4, 1, 2, 3)
        conv_state = jnp.zeros((batch, CONV_WIDTH - 1, D_RNN), F32)
        h0 = jnp.zeros((batch, 1, D_RNN), F32)
    else:
        cache_k, cache_v, conv_state, h0 = past
        attn = _attn_sample(q, kb, vb, cache_k, cache_v, w["lams"], w["subln_g"], batch, t)
    y_rnn, new_conv, h_last = _rglru(xr, gr, conv_state, h0, w["conv_w"], w["conv_b"], w["w_a"], w["b_a"],
                                     w["w_i"], w["b_i"], w["rg_lambda"], batch, t, tt)
    x1, xn2, comb = _merge(attn, y_rnn, gates, x2d, w["w_attn"], w["w_rnn"], w["w_out"], w["b_gate"],
                           w["g2"], w["w_cat"], w["b_cat"], tm_merge)
    y = _final(x1, _moe(xn2, comb, w["w1"], w["w3"], w["w2"], tm_moe), w["gf"], tm_merge)
    return (y.reshape(batch, t, D_MODEL),
            k.reshape(1, batch, t, N_HEADS, 2, QK_HEAD_DIM),
            v.reshape(1, batch, t, N_HEADS, V_HEAD_DIM),
            new_conv.reshape(1, batch, CONV_WIDTH - 1, D_RNN),
            h_last.reshape(1, batch, D_RNN))


def kernel(x_prompt, x_sample, cache_k, cache_v, state_conv, state_rnn, norm1_g, w_in, lambda_q1, lambda_k1, lambda_q2, lambda_k2, subln_g, w_attn_proj, conv_w, conv_b, w_rg_a, b_rg_a, w_rg_i, b_rg_i, rg_lambda, w_rnn_proj, b_gate, w_out, norm2_g, w_group, b_group, w_router, b_router, w1, w3, w2, final_norm_g):
    assert norm1_g.shape[0] == 1, "single-layer model"
    dec_batch, past_len = cache_k.shape[1], cache_k.shape[2]
    pad = ROUTE_LANES - N_GROUPS - N_EXPERTS
    w = {
        "g1": norm1_g, "g2": norm2_g, "gf": final_norm_g.reshape(1, D_MODEL),
        "w_in": w_in[0].astype(BF16), "w_kt": w_in[0, :, D_MODEL:2 * D_MODEL].T.astype(BF16),
        "lams": (lambda_q1, lambda_k1, lambda_q2, lambda_k2),
        "subln_g": subln_g,
        "w_attn": w_attn_proj[0].astype(BF16), "w_rnn": w_rnn_proj[0].astype(BF16),
        "w_out": w_out[0].astype(BF16),
        "conv_w": conv_w[0], "conv_b": conv_b,
        "w_a": w_rg_a[0].astype(BF16), "b_a": b_rg_a, "w_i": w_rg_i[0].astype(BF16), "b_i": b_rg_i,
        "rg_lambda": rg_lambda, "b_gate": b_gate,
        "w_cat": jnp.pad(jnp.concatenate([w_group[0], w_router[0]], axis=1), ((0, 0), (0, pad))).astype(BF16),
        "b_cat": jnp.pad(jnp.concatenate([b_group[0], b_router[0]]), (0, pad)).reshape(1, ROUTE_LANES),
        "w1": w1[0].astype(BF16), "w3": w3[0].astype(BF16), "w2": w2[0].astype(BF16),
    }
    yp, kp, vp, cp, hp = _stream(x_prompt, None, w, tm_in=256, tm_merge=256, tm_moe=2048, tq=512, tt=256)
    past = (cache_k[0].transpose(0, 2, 3, 4, 1).reshape(dec_batch, D_MODEL, past_len),
            cache_v[0].reshape(dec_batch, past_len * N_HEADS, V_HEAD_DIM),
            state_conv[0], state_rnn[0].reshape(dec_batch, 1, D_RNN))
    dec_t = x_sample.shape[1]
    ys, ks, vs, cs, hs = _stream(x_sample, past, w, tm_in=256, tm_merge=256, tm_moe=256, tq=dec_t, tt=dec_t)
    return (yp, ys, kp, vp, cp, hp, ks, vs, cs, hs)
```

```python
import functools
import math

import jax
import jax.numpy as jnp
from jax import lax
from jax.experimental import pallas as pl
from jax.experimental.pallas import tpu as pltpu

F32 = jnp.float32
BF16 = jnp.bfloat16

D_MODEL = 1024
CHUNK = 64
N_HEADS = 8
QK_HEAD_DIM = 64
V_HEAD_DIM = 128
HEAD_COLS = 2 * QK_HEAD_DIM
D_RNN = 1024
N_RNN_BLOCKS = 8
RNN_BLOCK = D_RNN // N_RNN_BLOCKS
CONV_WIDTH = 4
RG_C = 8.0
N_GROUPS = 4
EXPERTS_PER_GROUP = 4
N_EXPERTS = N_GROUPS * EXPERTS_PER_GROUP
D_EXPERT = 512
EPS = 1e-6
LAMBDA_INIT = 0.8 - 0.6 * math.exp(-0.3 * 0)
Q_SCALE = QK_HEAD_DIM ** -0.5 * math.log2(math.e)
N_SEG = 7
IN_WIDTH = N_SEG * D_MODEL
ROUTE_LANES = 128
MOE_ROWS = 320
EXPERT_LANE0 = N_GROUPS
SUBLANES = 8
LANES = 128
HEADS_PER_STEP = 2
VMEM_LIMIT = 56 * 1024 * 1024
NEG = float(jnp.finfo(jnp.float32).min)


def _params(*sem):
    return pltpu.CompilerParams(dimension_semantics=sem, vmem_limit_bytes=VMEM_LIMIT)


def _rms(x, g):
    return x * lax.rsqrt(jnp.mean(x * x, axis=-1, keepdims=True) + EPS) * g


def _sigmoid(x):
    return 1.0 / (1.0 + jnp.exp(-x))


def _in_proj_kernel(x_ref, g_ref, w_ref, wkt_ref, q_ref, k_ref, v_ref, kb_ref, vb_ref, xr_ref, gr_ref, gt_ref,
                    *, k_transposed):
    xn = _rms(x_ref[...], g_ref[...]).astype(BF16)

    def seg(j):
        return jnp.dot(xn, w_ref[:, j * D_MODEL:(j + 1) * D_MODEL], preferred_element_type=F32)

    q_ref[...] = (seg(0) * Q_SCALE).astype(BF16)
    if k_transposed:
        kt = lax.dot_general(wkt_ref[...], xn, (((1,), (1,)), ((), ())), preferred_element_type=F32)
        k_ref[0] = kt
        kb_ref[0, 0] = kt.astype(BF16)
    else:
        k = seg(1)
        k_ref[...] = k
        kb_ref[...] = k.astype(BF16)
    v = seg(2)
    v_ref[...] = v
    vb_ref[...] = v.astype(BF16)
    xr_ref[...] = seg(3)
    gr_ref[...] = seg(4)
    gt_ref[:, :D_MODEL] = seg(5)
    gt_ref[:, D_MODEL:] = seg(6)


def _in_proj(x2d, g1, w_in_b, w_kt_b, tm, seq=None):
    n = x2d.shape[0]
    row = lambda i: (i, 0)
    const = lambda i: (0, 0)
    wide = lambda dt: jax.ShapeDtypeStruct((n, D_MODEL), dt)
    wide_spec = pl.BlockSpec((tm, D_MODEL), row)
    if seq is None:
        k_shape, k_spec, kb_shape, kb_spec = wide(F32), wide_spec, wide(BF16), wide_spec
    else:
        nt = seq // tm
        k_shape = jax.ShapeDtypeStruct((n // seq, D_MODEL, seq), F32)
        k_spec = pl.BlockSpec((1, D_MODEL, tm), lambda i: (i // nt, 0, i % nt))
        kb_shape = jax.ShapeDtypeStruct((n // seq, nt, D_MODEL, tm), BF16)
        kb_spec = pl.BlockSpec((1, 1, D_MODEL, tm), lambda i: (i // nt, i % nt, 0, 0))
    return pl.pallas_call(
        functools.partial(_in_proj_kernel, k_transposed=seq is not None),
        grid=(n // tm,),
        in_specs=[wide_spec,
                  pl.BlockSpec((1, D_MODEL), const),
                  pl.BlockSpec((D_MODEL, IN_WIDTH), const, pipeline_mode=pl.Buffered(1)),
                  pl.BlockSpec((D_MODEL, D_MODEL), const, pipeline_mode=pl.Buffered(1))],
        out_specs=[wide_spec, k_spec, wide_spec, kb_spec, wide_spec, wide_spec, wide_spec,
                   pl.BlockSpec((tm, 2 * D_MODEL), row)],
        out_shape=[wide(BF16), k_shape, wide(F32), kb_shape, wide(BF16), wide(F32), wide(F32),
                   jax.ShapeDtypeStruct((n, 2 * D_MODEL), F32)],
        compiler_params=_params("parallel"),
    )(x2d, g1, w_in_b, w_kt_b)


def _lam(lq1, lk1, lq2, lk2):
    return (jnp.exp(jnp.sum(lq1[...] * lk1[...], axis=-1, keepdims=True))
            - jnp.exp(jnp.sum(lq2[...] * lk2[...], axis=-1, keepdims=True)) + LAMBDA_INIT)


def _split_q(q):
    lane = lax.broadcasted_iota(jnp.int32, q.shape, 1)
    zero = jnp.zeros_like(q)
    return jnp.where(lane < QK_HEAD_DIM, q, zero), jnp.where(lane >= QK_HEAD_DIM, q, zero)


def _scores(qm, k):
    return lax.dot_general(qm, k, (((1,), (1,)), ((), ())), preferred_element_type=F32)


def _subln(o, g):
    return (_rms(o, g) * (1.0 - LAMBDA_INIT)).astype(BF16)


def _attn_prompt_kernel(lq1, lk1, lq2, lk2, sg_ref, q_ref, k_ref, v_ref, o_ref, m_ref, acc_ref, *, tq, kw):
    i = pl.program_id(2)
    m_ref[...] = jnp.full(m_ref.shape, -jnp.inf, F32)
    acc_ref[...] = jnp.zeros(acc_ref.shape, F32)
    ones = jnp.ones((tq, LANES), BF16)
    qq = []
    for h in range(HEADS_PER_STEP):
        q1, q2 = _split_q(q_ref[:, h * HEAD_COLS:(h + 1) * HEAD_COLS])
        qq.append(jnp.concatenate([q1, q2], axis=0))

    def tile(kt, h, masked):
        start = pl.multiple_of(kt * tq, tq)
        chunks = []
        for j in range(tq // kw):
            kt_blk = k_ref[0, kt * (tq // kw) + j, h * HEAD_COLS:(h + 1) * HEAD_COLS, :]
            s = jnp.dot(qq[h], kt_blk, preferred_element_type=F32)
            chunks += [s[:, c * LANES:(c + 1) * LANES] for c in range(kw // LANES)]
        if masked:
            row = lax.broadcasted_iota(jnp.int32, (2 * tq, LANES), 0)
            lane = lax.broadcasted_iota(jnp.int32, (2 * tq, LANES), 1)
            qchunk = (row % tq) // CHUNK
            chunks = [jnp.where((lane + c * LANES) // CHUNK <= qchunk, x, NEG) for c, x in enumerate(chunks)]
        part = chunks[0]
        for x in chunks[1:]:
            part = jnp.maximum(part, x)
        m_prev = m_ref[h]
        m_next = jnp.maximum(m_prev, jnp.max(part, axis=1, keepdims=True))
        p = jnp.concatenate([jnp.exp2(x - m_next).astype(BF16) for x in chunks], axis=1)
        alpha = jnp.exp2(m_prev - m_next)
        v1 = jnp.concatenate([v_ref[pl.ds(start, tq), h * V_HEAD_DIM:(h + 1) * V_HEAD_DIM], ones], axis=1)
        acc_ref[h] = (jnp.concatenate([alpha, alpha], axis=1) * acc_ref[h]
                      + jnp.dot(p, v1, preferred_element_type=F32))
        m_ref[h] = m_next

    def body(kt, c):
        for h in range(HEADS_PER_STEP):
            tile(kt, h, False)
        return c

    lax.fori_loop(0, i, body, 0)
    lam = _lam(lq1, lk1, lq2, lk2)
    for h in range(HEADS_PER_STEP):
        tile(i, h, True)
    for h in range(HEADS_PER_STEP):
        o1 = acc_ref[h, :tq, :V_HEAD_DIM] / acc_ref[h, :tq, V_HEAD_DIM:]
        o2 = acc_ref[h, tq:, :V_HEAD_DIM] / acc_ref[h, tq:, V_HEAD_DIM:]
        o_ref[:, h * V_HEAD_DIM:(h + 1) * V_HEAD_DIM] = _subln(o1 - lam * o2, sg_ref[...])


def _attn_prompt(q, kbt, vb, lams, subln_g, batch, seq, tq):
    n = q.shape[0]
    nq = seq // tq
    nkb, kw = kbt.shape[1], kbt.shape[3]
    small = lambda b, h, i: (0, 0)
    qmap = lambda b, h, i: (b * nq + i, h)
    kvmap = lambda b, h, i: (b, h)
    return pl.pallas_call(
        functools.partial(_attn_prompt_kernel, tq=tq, kw=kw),
        grid=(batch, N_HEADS // HEADS_PER_STEP, nq),
        in_specs=[pl.BlockSpec((1, QK_HEAD_DIM), small)] * 4 + [
            pl.BlockSpec((1, V_HEAD_DIM), small),
            pl.BlockSpec((tq, HEADS_PER_STEP * HEAD_COLS), qmap),
            pl.BlockSpec((1, nkb, HEADS_PER_STEP * HEAD_COLS, kw), lambda b, h, i: (b, 0, h, 0)),
            pl.BlockSpec((seq, HEADS_PER_STEP * V_HEAD_DIM), kvmap)],
        out_specs=pl.BlockSpec((tq, HEADS_PER_STEP * V_HEAD_DIM), qmap),
        out_shape=jax.ShapeDtypeStruct((n, N_HEADS * V_HEAD_DIM), BF16),
        scratch_shapes=[pltpu.VMEM((HEADS_PER_STEP, 2 * tq, LANES), F32),
                        pltpu.VMEM((HEADS_PER_STEP, 2 * tq, 2 * LANES), F32)],
        compiler_params=_params("parallel", "parallel", "arbitrary"),
    )(*lams, subln_g, q, kbt, vb)


def _attn_sample_kernel(lq1, lk1, lq2, lk2, sg_ref, q_ref, kn_ref, vn_ref, ck_ref, cv_ref, o_ref, *, past):
    lam = _lam(lq1, lk1, lq2, lk2)
    for h in range(N_HEADS):
        cols = slice(h * HEAD_COLS, (h + 1) * HEAD_COLS)
        q1, q2 = _split_q(q_ref[:, cols])
        kct = ck_ref[0, h * HEAD_COLS:(h + 1) * HEAD_COLS, :].astype(BF16)
        vc = cv_ref[0, pl.ds(h, past, stride=N_HEADS), :].astype(BF16)
        kn = kn_ref[:, cols]
        vn = vn_ref[:, cols]

        def branch(qm):
            sc = jnp.dot(qm, kct, preferred_element_type=F32)
            sn = _scores(qm, kn)
            m = jnp.maximum(jnp.max(sc, axis=-1, keepdims=True), jnp.max(sn, axis=-1, keepdims=True))
            pc = jnp.exp2(sc - m)
            pn = jnp.exp2(sn - m)
            l = jnp.sum(pc, axis=-1, keepdims=True) + jnp.sum(pn, axis=-1, keepdims=True)
            acc = (jnp.dot(pc.astype(BF16), vc, preferred_element_type=F32)
                   + jnp.dot(pn.astype(BF16), vn, preferred_element_type=F32))
            return acc / l

        o = branch(q1) - lam * branch(q2)
        o_ref[:, cols] = _subln(o, sg_ref[...])


def _attn_sample(q, kb, vb, cache_kt, cache_v, lams, subln_g, batch, t):
    past = cache_kt.shape[2]
    small = lambda b: (0, 0)
    row = lambda b: (b, 0)
    cache = lambda b: (b, 0, 0)
    return pl.pallas_call(
        functools.partial(_attn_sample_kernel, past=past),
        grid=(batch,),
        in_specs=[pl.BlockSpec((1, QK_HEAD_DIM), small)] * 4 + [
            pl.BlockSpec((1, V_HEAD_DIM), small),
            pl.BlockSpec((t, D_MODEL), row),
            pl.BlockSpec((t, D_MODEL), row),
            pl.BlockSpec((t, D_MODEL), row),
            pl.BlockSpec((1, D_MODEL, past), cache),
            pl.BlockSpec((1, past * N_HEADS, V_HEAD_DIM), cache)],
        out_specs=pl.BlockSpec((t, D_MODEL), row),
        out_shape=jax.ShapeDtypeStruct((batch * t, D_MODEL), BF16),
        compiler_params=_params("parallel"),
    )(*lams, subln_g, q, kb, vb, cache_kt, cache_v)


def _rglru_kernel(xr_ref, gr_ref, cs_ref, h0_ref, cw_ref, cb_ref, wa_ref, ba_ref, wi_ref, bi_ref, lam_ref,
                  y_ref, nc_ref, hl_ref, xpad, hcar, a_s, b_s, h_s, *, tt):
    t = pl.program_id(1)
    pad = SUBLANES

    @pl.when(t == 0)
    def _():
        xpad[0:pad, :] = jnp.zeros((pad, D_RNN), F32)
        xpad[pad - (CONV_WIDTH - 1):pad, :] = cs_ref[0]
        hcar[...] = h0_ref[0]

    xpad[pad:pad + tt, :] = xr_ref[...]
    xc = cb_ref[...]
    for j in range(CONV_WIDTH):
        off = pad - (CONV_WIDTH - 1) + j
        xc = xc + xpad[off:off + tt, :] * cw_ref[j:j + 1, :]
    xcb = xc.astype(BF16)

    z = -lam_ref[...]
    softplus = jnp.maximum(z, 0.0) + jnp.log1p(jnp.exp(-jnp.abs(z)))
    for n in range(N_RNN_BLOCKS):
        cols = slice(n * RNN_BLOCK, (n + 1) * RNN_BLOCK)
        xs = xcb[:, cols]
        r = _sigmoid(jnp.dot(xs, wa_ref[n], preferred_element_type=F32) + ba_ref[:, cols])
        i = _sigmoid(jnp.dot(xs, wi_ref[n], preferred_element_type=F32) + bi_ref[:, cols])
        log_a = -RG_C * r * softplus[:, cols]
        a_s[:, cols] = jnp.exp(log_a)
        th = jnp.tanh(log_a)
        b_s[:, cols] = jnp.sqrt(-2.0 * th / (1.0 - th)) * (i * xc[:, cols])

    def step(s, h):
        h = a_s[pl.ds(s, 1), :] * h + b_s[pl.ds(s, 1), :]
        h_s[pl.ds(s, 1), :] = h
        return h

    h = lax.fori_loop(0, tt, step, hcar[...], unroll=8)
    hcar[...] = h
    y_ref[...] = (h_s[...] * jax.nn.gelu(gr_ref[...])).astype(BF16)
    nc_ref[0] = xpad[pad + tt - (CONV_WIDTH - 1):pad + tt, :]
    hl_ref[0] = h
    xpad[0:pad, :] = xpad[tt:tt + pad, :]


def _rglru(xr, gr, conv_state, h0, conv_w, conv_b, w_a_b, b_a, w_i_b, b_i, rg_lambda, batch, t, tt):
    nt = t // tt
    row = lambda b, s: (b * nt + s, 0)
    const2 = lambda b, s: (0, 0)
    const3 = lambda b, s: (0, 0, 0)
    per_b = lambda b, s: (b, 0, 0)
    vec = pl.BlockSpec((1, D_RNN), const2)
    blk = pl.BlockSpec((N_RNN_BLOCKS, RNN_BLOCK, RNN_BLOCK), const3)
    big = pltpu.VMEM((tt, D_RNN), F32)
    return pl.pallas_call(
        functools.partial(_rglru_kernel, tt=tt),
        grid=(batch, nt),
        in_specs=[pl.BlockSpec((tt, D_RNN), row), pl.BlockSpec((tt, D_RNN), row),
                  pl.BlockSpec((1, CONV_WIDTH - 1, D_RNN), per_b), pl.BlockSpec((1, 1, D_RNN), per_b),
                  pl.BlockSpec((CONV_WIDTH, D_RNN), const2), vec, blk, vec, blk, vec, vec],
        out_specs=[pl.BlockSpec((tt, D_RNN), row),
                   pl.BlockSpec((1, CONV_WIDTH - 1, D_RNN), per_b),
                   pl.BlockSpec((1, 1, D_RNN), per_b)],
        out_shape=[jax.ShapeDtypeStruct((batch * t, D_RNN), BF16),
                   jax.ShapeDtypeStruct((batch, CONV_WIDTH - 1, D_RNN), F32),
                   jax.ShapeDtypeStruct((batch, 1, D_RNN), F32)],
        scratch_shapes=[pltpu.VMEM((tt + SUBLANES, D_RNN), F32), pltpu.VMEM((1, D_RNN), F32), big, big, big],
        compiler_params=_params("parallel", "arbitrary"),
    )(xr, gr, conv_state, h0, conv_w, conv_b, w_a_b, b_a, w_i_b, b_i, rg_lambda)


def _first_lane_of_max(vals, valid, lane):
    masked = jnp.where(valid, vals, -jnp.inf)
    mx = jnp.max(masked, axis=-1, keepdims=True)
    idx = jnp.min(jnp.where(valid & (masked == mx), lane, ROUTE_LANES), axis=-1, keepdims=True)
    return mx, idx


def _route(logits):
    lane = lax.broadcasted_iota(jnp.int32, logits.shape, 1)
    is_group = lane < N_GROUPS
    gmax, gsel = _first_lane_of_max(logits, is_group, lane)
    gsum = jnp.sum(jnp.where(is_group, jnp.exp(logits - gmax), 0.0), axis=-1, keepdims=True)
    g_w = 1.0 / gsum
    expert = lane - EXPERT_LANE0
    in_group = (expert >= 0) & (expert < N_EXPERTS) & ((expert // EXPERTS_PER_GROUP) == gsel)
    v1, i1 = _first_lane_of_max(logits, in_group, lane)
    v2, i2 = _first_lane_of_max(logits, in_group & (lane != i1), lane)
    e2 = jnp.exp(v2 - v1)
    den = 1.0 + e2
    return jnp.where(lane == i1, (1.0 / den) * g_w, 0.0) + jnp.where(lane == i2, (e2 / den) * g_w, 0.0)


def _merge_kernel(at_ref, y_ref, gt_ref, x_ref, wa_ref, wr_ref, wo_ref, bg_ref, g2_ref, wc_ref, bc_ref,
                  x1_ref, xn_ref, cm_ref):
    pa = jnp.dot(at_ref[...], wa_ref[...], preferred_element_type=F32)
    pr = jnp.dot(y_ref[...], wr_ref[...], preferred_element_type=F32)
    g = _sigmoid(gt_ref[...] + bg_ref[...])
    merged = g[:, :D_MODEL] * pa + g[:, D_MODEL:] * pr
    x1 = x_ref[...] + jnp.dot(merged.astype(BF16), wo_ref[...], preferred_element_type=F32)
    x1_ref[...] = x1
    xn = _rms(x1, g2_ref[...]).astype(BF16)
    xn_ref[...] = xn
    cm_ref[...] = _route(jnp.dot(xn, wc_ref[...], preferred_element_type=F32) + bc_ref[...])


def _merge(attn, y_rnn, gates, x2d, wa_b, wr_b, wo_b, b_gate, g2, w_cat_b, b_cat, tm):
    n = x2d.shape[0]
    row = lambda i: (i, 0)
    const = lambda i: (0, 0)
    sq = pl.BlockSpec((D_MODEL, D_MODEL), const)
    return pl.pallas_call(
        _merge_kernel,
        grid=(n // tm,),
        in_specs=[pl.BlockSpec((tm, D_MODEL), row), pl.BlockSpec((tm, D_MODEL), row),
                  pl.BlockSpec((tm, 2 * D_MODEL), row), pl.BlockSpec((tm, D_MODEL), row),
                  sq, sq, sq, pl.BlockSpec((1, 2 * D_MODEL), const), pl.BlockSpec((1, D_MODEL), const),
                  pl.BlockSpec((D_MODEL, ROUTE_LANES), const), pl.BlockSpec((1, ROUTE_LANES), const)],
        out_specs=[pl.BlockSpec((tm, D_MODEL), row), pl.BlockSpec((tm, D_MODEL), row),
                   pl.BlockSpec((tm, ROUTE_LANES), row)],
        out_shape=[jax.ShapeDtypeStruct((n, D_MODEL), F32), jax.ShapeDtypeStruct((n, D_MODEL), BF16),
                   jax.ShapeDtypeStruct((n, ROUTE_LANES), F32)],
        compiler_params=_params("parallel"),
    )(attn, y_rnn, gates, x2d, wa_b, wr_b, wo_b, b_gate, g2, w_cat_b, b_cat)


def _route_tables(comb, tile):
    n = comb.shape[0]
    nt = n // tile
    c = comb[:, EXPERT_LANE0:EXPERT_LANE0 + N_EXPERTS]
    sel = jnp.any(c.reshape(nt, tile, N_GROUPS, EXPERTS_PER_GROUP) != 0.0, axis=-1)
    csum = jnp.cumsum(sel.astype(jnp.int32), axis=1)
    rank = jnp.where(sel, csum - 1, -1).astype(F32)
    counts = csum[:, -1, :].reshape(nt * N_GROUPS)
    rank_t = rank.transpose(0, 2, 1).reshape(nt * N_GROUPS, 1, tile)
    rank = jnp.pad(rank.reshape(n, N_GROUPS), ((0, 0), (0, ROUTE_LANES - N_GROUPS)), constant_values=-1.0)
    hi = c.astype(BF16)
    rest = c - hi.astype(F32)
    mid = rest.astype(BF16)
    lo = (rest - mid.astype(F32)).astype(BF16)
    c3 = jnp.pad(jnp.concatenate([hi, mid, lo], axis=1), ((0, 0), (0, ROUTE_LANES - 3 * N_EXPERTS)))
    return counts, rank, rank_t, c3


def _moe_kernel(cnt_ref, x_ref, c3_ref, rk_ref, rkt_ref, w1_ref, w3_ref, w2_ref, y_ref, xg, cg, yg, *, tile):
    t = pl.program_id(0)
    e = pl.program_id(1)
    g = e // EXPERTS_PER_GROUP
    member = e % EXPERTS_PER_GROUP
    count = cnt_ref[t * N_GROUPS + g]
    nblk = (count + MOE_ROWS - 1) // MOE_ROWS

    @pl.when(e == 0)
    def _():
        y_ref[...] = jnp.zeros(y_ref.shape, F32)

    @pl.when(member == 0)
    def _():
        rank_row = rkt_ref[0]

        def gather(b, carry):
            slot = lax.broadcasted_iota(jnp.int32, (MOE_ROWS, tile), 0).astype(F32) + (b * MOE_ROWS).astype(F32)
            onehot = jnp.where(rank_row == slot, 1.0, 0.0).astype(BF16)
            xg[b] = jnp.dot(onehot, x_ref[...], preferred_element_type=F32).astype(BF16)
            cg[b] = jnp.dot(onehot, c3_ref[...], preferred_element_type=F32)
            yg[b] = jnp.zeros((MOE_ROWS, D_MODEL), F32)
            return carry

        lax.fori_loop(0, nblk, gather, 0)

    lane_r = lax.broadcasted_iota(jnp.int32, (MOE_ROWS, ROUTE_LANES), 1)
    mine = (lane_r % N_EXPERTS == e) & (lane_r < 3 * N_EXPERTS)

    def expert(b, carry):
        xc = xg[b]
        cw = jnp.sum(jnp.where(mine, cg[b], 0.0), axis=1, keepdims=True)
        a = jnp.dot(xc, w1_ref[0], preferred_element_type=F32)
        h = (a * _sigmoid(a)) * jnp.dot(xc, w3_ref[0], preferred_element_type=F32)
        yg[b] += cw * jnp.dot(h.astype(BF16), w2_ref[0], preferred_element_type=F32)
        return carry

    lax.fori_loop(0, nblk, expert, 0)

    @pl.when(member == EXPERTS_PER_GROUP - 1)
    def _():
        lane = lax.broadcasted_iota(jnp.int32, (tile, ROUTE_LANES), 1)
        rank_col = jnp.sum(jnp.where(lane == g, rk_ref[...], 0.0), axis=1, keepdims=True)

        def scatter(b, carry):
            slot_t = lax.broadcasted_iota(jnp.int32, (tile, MOE_ROWS), 1).astype(F32) + (b * MOE_ROWS).astype(F32)
            onehot_t = jnp.where(rank_col == slot_t, 1.0, 0.0).astype(BF16)
            y_ref[...] += jnp.dot(onehot_t, yg[b].astype(BF16), preferred_element_type=F32)
            return carry

        lax.fori_loop(0, nblk, scatter, 0)


def _moe(xn2, comb, w1_b, w3_b, w2_b, tile):
    n = xn2.shape[0]
    counts, rank, rank_t, c3 = _route_tables(comb, tile)
    max_blocks = -(-tile // MOE_ROWS)
    row = lambda t, e, cnt: (t, 0)
    per_e = lambda t, e, cnt: (e, 0, 0)
    return pl.pallas_call(
        functools.partial(_moe_kernel, tile=tile),
        grid_spec=pltpu.PrefetchScalarGridSpec(
            num_scalar_prefetch=1,
            grid=(n // tile, N_EXPERTS),
            in_specs=[pl.BlockSpec((tile, D_MODEL), row), pl.BlockSpec((tile, ROUTE_LANES), row),
                      pl.BlockSpec((tile, ROUTE_LANES), row),
                      pl.BlockSpec((1, 1, tile), lambda t, e, cnt: (t * N_GROUPS + e // EXPERTS_PER_GROUP, 0, 0)),
                      pl.BlockSpec((1, D_MODEL, D_EXPERT), per_e), pl.BlockSpec((1, D_MODEL, D_EXPERT), per_e),
                      pl.BlockSpec((1, D_EXPERT, D_MODEL), per_e)],
            out_specs=pl.BlockSpec((tile, D_MODEL), row),
            scratch_shapes=[pltpu.VMEM((max_blocks, MOE_ROWS, D_MODEL), BF16),
                            pltpu.VMEM((max_blocks, MOE_ROWS, ROUTE_LANES), F32),
                            pltpu.VMEM((max_blocks, MOE_ROWS, D_MODEL), F32)]),
        out_shape=jax.ShapeDtypeStruct((n, D_MODEL), F32),
        compiler_params=_params("parallel", "arbitrary"),
    )(counts, xn2, c3, rank, rank_t, w1_b, w3_b, w2_b)


def _final_kernel(x1_ref, y_ref, g_ref, o_ref):
    o_ref[...] = _rms(x1_ref[...] + y_ref[...], g_ref[...])


def _final(x1, y, gf, tm):
    n = x1.shape[0]
    row = lambda i: (i, 0)
    return pl.pallas_call(
        _final_kernel,
        grid=(n // tm,),
        in_specs=[pl.BlockSpec((tm, D_MODEL), row), pl.BlockSpec((tm, D_MODEL), row),
                  pl.BlockSpec((1, D_MODEL), lambda i: (0, 0))],
        out_specs=pl.BlockSpec((tm, D_MODEL), row),
        out_shape=jax.ShapeDtypeStruct((n, D_MODEL), F32),
        compiler_params=_params("parallel"),
    )(x1, y, gf)


def _stream(x, past, w, *, tm_in, tm_merge, tm_moe, tq, tt):
    batch, t, _ = x.shape
    n = batch * t
    x2d = x.reshape(n, D_MODEL)
    q, k, v, kb, vb, xr, gr, gates = _in_proj(x2d, w["g1"], w["w_in"], w["w_kt"], tm_in,
                                              seq=t if past is None else None)
    if past is None:
        attn = _attn_prompt(q, kb, vb, w["lams"], w["subln_g"], batch, t, tq)
        k = k.reshape(batch, N_HEADS, 2, QK_HEAD_DIM, t).transpose(0, 4, 1, 2, 3)
        conv_state = jnp.zeros((batch, CONV_WIDTH - 1, D_RNN), F32)
        h0 = jnp.zeros((batch, 1, D_RNN), F32)
    else:
        cache_k, cache_v, conv_state, h0 = past
        attn = _attn_sample(q, kb, vb, cache_k, cache_v, w["lams"], w["subln_g"], batch, t)
    y_rnn, new_conv, h_last = _rglru(xr, gr, conv_state, h0, w["conv_w"], w["conv_b"], w["w_a"], w["b_a"],
                                     w["w_i"], w["b_i"], w["rg_lambda"], batch, t, tt)
    x1, xn2, comb = _merge(attn, y_rnn, gates, x2d, w["w_attn"], w["w_rnn"], w["w_out"], w["b_gate"],
                           w["g2"], w["w_cat"], w["b_cat"], tm_merge)
    y = _final(x1, _moe(xn2, comb, w["w1"], w["w3"], w["w2"], tm_moe), w["gf"], tm_merge)
    return (y.reshape(batch, t, D_MODEL),
            k.reshape(1, batch, t, N_HEADS, 2, QK_HEAD_DIM),
            v.reshape(1, batch, t, N_HEADS, V_HEAD_DIM),
            new_conv.reshape(1, batch, CONV_WIDTH - 1, D_RNN),
            h_last.reshape(1, batch, D_RNN))


def kernel(x_prompt, x_sample, cache_k, cache_v, state_conv, state_rnn, norm1_g, w_in, lambda_q1, lambda_k1, lambda_q2, lambda_k2, subln_g, w_attn_proj, conv_w, conv_b, w_rg_a, b_rg_a, w_rg_i, b_rg_i, rg_lambda, w_rnn_proj, b_gate, w_out, norm2_g, w_group, b_group, w_router, b_router, w1, w3, w2, final_norm_g):
    assert norm1_g.shape[0] == 1, "single-layer model"
    dec_batch, past_len = cache_k.shape[1], cache_k.shape[2]
    pad = ROUTE_LANES - N_GROUPS - N_EXPERTS
    w = {
        "g1": norm1_g, "g2": norm2_g, "gf": final_norm_g.reshape(1, D_MODEL),
        "w_in": w_in[0].astype(BF16), "w_kt": w_in[0, :, D_MODEL:2 * D_MODEL].T.astype(BF16),
        "lams": (lambda_q1, lambda_k1, lambda_q2, lambda_k2),
        "subln_g": subln_g,
        "w_attn": w_attn_proj[0].astype(BF16), "w_rnn": w_rnn_proj[0].astype(BF16),
        "w_out": w_out[0].astype(BF16),
        "conv_w": conv_w[0], "conv_b": conv_b,
        "w_a": w_rg_a[0].astype(BF16), "b_a": b_rg_a, "w_i": w_rg_i[0].astype(BF16), "b_i": b_rg_i,
        "rg_lambda": rg_lambda, "b_gate": b_gate,
        "w_cat": jnp.pad(jnp.concatenate([w_group[0], w_router[0]], axis=1), ((0, 0), (0, pad))).astype(BF16),
        "b_cat": jnp.pad(jnp.concatenate([b_group[0], b_router[0]]), (0, pad)).reshape(1, ROUTE_LANES),
        "w1": w1[0].astype(BF16), "w3": w3[0].astype(BF16), "w2": w2[0].astype(BF16),
    }
    yp, kp, vp, cp, hp = _stream(x_prompt, None, w, tm_in=256, tm_merge=256, tm_moe=1024, tq=512, tt=256)
    past = (cache_k[0].transpose(0, 2, 3, 4, 1).reshape(dec_batch, D_MODEL, past_len),
            cache_v[0].reshape(dec_batch, past_len * N_HEADS, V_HEAD_DIM),
            state_conv[0], state_rnn[0].reshape(dec_batch, 1, D_RNN))
    dec_t = x_sample.shape[1]
    ys, ks, vs, cs, hs = _stream(x_sample, past, w, tm_in=256, tm_merge=256, tm_moe=256, tq=dec_t, tt=dec_t)
    return (yp, ys, kp, vp, cp, hp, ks, vs, cs, hs)
```

```python
import functools
import math

import jax
import jax.numpy as jnp
from jax import lax
from jax.experimental import pallas as pl
from jax.experimental.pallas import tpu as pltpu

F32 = jnp.float32
BF16 = jnp.bfloat16

D_MODEL = 1024
CHUNK = 64
N_HEADS = 8
QK_HEAD_DIM = 64
V_HEAD_DIM = 128
HEAD_COLS = 2 * QK_HEAD_DIM
D_RNN = 1024
N_RNN_BLOCKS = 8
RNN_BLOCK = D_RNN // N_RNN_BLOCKS
CONV_WIDTH = 4
RG_C = 8.0
N_GROUPS = 4
EXPERTS_PER_GROUP = 4
N_EXPERTS = N_GROUPS * EXPERTS_PER_GROUP
D_EXPERT = 512
EPS = 1e-6
LAMBDA_INIT = 0.8 - 0.6 * math.exp(-0.3 * 0)
Q_SCALE = QK_HEAD_DIM ** -0.5 * math.log2(math.e)
N_SEG = 7
IN_WIDTH = N_SEG * D_MODEL
ROUTE_LANES = 128
MOE_ROWS = 320
EXPERT_LANE0 = N_GROUPS
SUBLANES = 8
LANES = 128
HEADS_PER_STEP = 8
MERGE_PART_ROWS = 256
VMEM_LIMIT = 56 * 1024 * 1024
NEG = float(jnp.finfo(jnp.float32).min)


def _params(*sem):
    return pltpu.CompilerParams(dimension_semantics=sem, vmem_limit_bytes=VMEM_LIMIT)


def _rms(x, g):
    return x * lax.rsqrt(jnp.mean(x * x, axis=-1, keepdims=True) + EPS) * g


def _sigmoid(x):
    return 0.5 * jnp.tanh(0.5 * x) + 0.5


def _in_proj_kernel(x_ref, g_ref, w_ref, wkt_ref, q_ref, k_ref, v_ref, kb_ref, vb_ref, xr_ref, gr_ref, gt_ref,
                    *, k_transposed):
    xn = _rms(x_ref[...], g_ref[...]).astype(BF16)

    def seg(j):
        return jnp.dot(xn, w_ref[:, j * D_MODEL:(j + 1) * D_MODEL], preferred_element_type=F32)

    q_ref[...] = (seg(0) * Q_SCALE).astype(BF16)
    if k_transposed:
        kt = lax.dot_general(wkt_ref[...], xn, (((1,), (1,)), ((), ())), preferred_element_type=F32)
        k_ref[0] = kt
        kb_ref[0, 0] = kt.astype(BF16)
    else:
        k = seg(1)
        k_ref[...] = k
        kb_ref[...] = k.astype(BF16)
    v = seg(2)
    v_ref[...] = v
    vb_ref[...] = v.astype(BF16)
    xr_ref[...] = seg(3)
    gr_ref[...] = seg(4)
    gt_ref[:, :D_MODEL] = seg(5)
    gt_ref[:, D_MODEL:] = seg(6)


def _in_proj(x2d, g1, w_in_b, w_kt_b, tm, seq=None):
    n = x2d.shape[0]
    row = lambda i: (i, 0)
    const = lambda i: (0, 0)
    wide = lambda dt: jax.ShapeDtypeStruct((n, D_MODEL), dt)
    wide_spec = pl.BlockSpec((tm, D_MODEL), row)
    if seq is None:
        k_shape, k_spec, kb_shape, kb_spec = wide(F32), wide_spec, wide(BF16), wide_spec
    else:
        nt = seq // tm
        k_shape = jax.ShapeDtypeStruct((n // seq, D_MODEL, seq), F32)
        k_spec = pl.BlockSpec((1, D_MODEL, tm), lambda i: (i // nt, 0, i % nt))
        kb_shape = jax.ShapeDtypeStruct((n // seq, nt, D_MODEL, tm), BF16)
        kb_spec = pl.BlockSpec((1, 1, D_MODEL, tm), lambda i: (i // nt, i % nt, 0, 0))
    return pl.pallas_call(
        functools.partial(_in_proj_kernel, k_transposed=seq is not None),
        grid=(n // tm,),
        in_specs=[wide_spec,
                  pl.BlockSpec((1, D_MODEL), const),
                  pl.BlockSpec((D_MODEL, IN_WIDTH), const, pipeline_mode=pl.Buffered(1)),
                  pl.BlockSpec((D_MODEL, D_MODEL), const, pipeline_mode=pl.Buffered(1))],
        out_specs=[wide_spec, k_spec, wide_spec, kb_spec, wide_spec, wide_spec, wide_spec,
                   pl.BlockSpec((tm, 2 * D_MODEL), row)],
        out_shape=[wide(BF16), k_shape, wide(F32), kb_shape, wide(BF16), wide(F32), wide(F32),
                   jax.ShapeDtypeStruct((n, 2 * D_MODEL), F32)],
        compiler_params=_params("parallel"),
    )(x2d, g1, w_in_b, w_kt_b)


def _lam(lq1, lk1, lq2, lk2):
    return (jnp.exp(jnp.sum(lq1[...] * lk1[...], axis=-1, keepdims=True))
            - jnp.exp(jnp.sum(lq2[...] * lk2[...], axis=-1, keepdims=True)) + LAMBDA_INIT)


def _split_q(q):
    lane = lax.broadcasted_iota(jnp.int32, q.shape, 1)
    zero = jnp.zeros_like(q)
    return jnp.where(lane < QK_HEAD_DIM, q, zero), jnp.where(lane >= QK_HEAD_DIM, q, zero)


def _scores(qm, k):
    return lax.dot_general(qm, k, (((1,), (1,)), ((), ())), preferred_element_type=F32)


def _subln(o, g):
    return (_rms(o, g) * (1.0 - LAMBDA_INIT)).astype(BF16)


def _attn_prompt_kernel(lq1, lk1, lq2, lk2, sg_ref, q_ref, k_ref, v_ref, o_ref, m_ref, acc_ref, *, tq, kw):
    i = pl.program_id(2)
    m_ref[...] = jnp.full(m_ref.shape, -jnp.inf, F32)
    acc_ref[...] = jnp.zeros(acc_ref.shape, F32)
    ones = jnp.ones((tq, LANES), BF16)
    qq = []
    for h in range(HEADS_PER_STEP):
        q1, q2 = _split_q(q_ref[:, h * HEAD_COLS:(h + 1) * HEAD_COLS])
        qq.append(jnp.concatenate([q1, q2], axis=0))

    def tile(kt, h, masked):
        start = pl.multiple_of(kt * tq, tq)
        chunks = []
        for j in range(tq // kw):
            kt_blk = k_ref[0, kt * (tq // kw) + j, h * HEAD_COLS:(h + 1) * HEAD_COLS, :]
            s = jnp.dot(qq[h], kt_blk, preferred_element_type=F32)
            chunks += [s[:, c * LANES:(c + 1) * LANES] for c in range(kw // LANES)]
        if masked:
            row = lax.broadcasted_iota(jnp.int32, (2 * tq, LANES), 0)
            lane = lax.broadcasted_iota(jnp.int32, (2 * tq, LANES), 1)
            qchunk = (row % tq) // CHUNK
            chunks = [jnp.where((lane + c * LANES) // CHUNK <= qchunk, x, NEG) for c, x in enumerate(chunks)]
        part = chunks[0]
        for x in chunks[1:]:
            part = jnp.maximum(part, x)
        m_prev = m_ref[h]
        m_next = jnp.maximum(m_prev, jnp.max(part, axis=1, keepdims=True))
        p = jnp.concatenate([jnp.exp2(x - m_next).astype(BF16) for x in chunks], axis=1)
        alpha = jnp.exp2(m_prev - m_next)
        v1 = jnp.concatenate([v_ref[pl.ds(start, tq), h * V_HEAD_DIM:(h + 1) * V_HEAD_DIM], ones], axis=1)
        acc_ref[h] = (jnp.concatenate([alpha, alpha], axis=1) * acc_ref[h]
                      + jnp.dot(p, v1, preferred_element_type=F32))
        m_ref[h] = m_next

    def body(kt, c):
        for h in range(HEADS_PER_STEP):
            tile(kt, h, False)
        return c

    lax.fori_loop(0, i, body, 0)
    lam = _lam(lq1, lk1, lq2, lk2)
    for h in range(HEADS_PER_STEP):
        tile(i, h, True)
    for h in range(HEADS_PER_STEP):
        o1 = acc_ref[h, :tq, :V_HEAD_DIM] / acc_ref[h, :tq, V_HEAD_DIM:]
        o2 = acc_ref[h, tq:, :V_HEAD_DIM] / acc_ref[h, tq:, V_HEAD_DIM:]
        o_ref[:, h * V_HEAD_DIM:(h + 1) * V_HEAD_DIM] = _subln(o1 - lam * o2, sg_ref[...])


def _attn_prompt(q, kbt, vb, lams, subln_g, batch, seq, tq):
    n = q.shape[0]
    nq = seq // tq
    nkb, kw = kbt.shape[1], kbt.shape[3]
    small = lambda b, h, i: (0, 0)
    qmap = lambda b, h, i: (b * nq + i, h)
    kvmap = lambda b, h, i: (b, h)
    return pl.pallas_call(
        functools.partial(_attn_prompt_kernel, tq=tq, kw=kw),
        grid=(batch, N_HEADS // HEADS_PER_STEP, nq),
        in_specs=[pl.BlockSpec((1, QK_HEAD_DIM), small)] * 4 + [
            pl.BlockSpec((1, V_HEAD_DIM), small),
            pl.BlockSpec((tq, HEADS_PER_STEP * HEAD_COLS), qmap),
            pl.BlockSpec((1, nkb, HEADS_PER_STEP * HEAD_COLS, kw), lambda b, h, i: (b, 0, h, 0)),
            pl.BlockSpec((seq, HEADS_PER_STEP * V_HEAD_DIM), kvmap)],
        out_specs=pl.BlockSpec((tq, HEADS_PER_STEP * V_HEAD_DIM), qmap),
        out_shape=jax.ShapeDtypeStruct((n, N_HEADS * V_HEAD_DIM), BF16),
        scratch_shapes=[pltpu.VMEM((HEADS_PER_STEP, 2 * tq, LANES), F32),
                        pltpu.VMEM((HEADS_PER_STEP, 2 * tq, 2 * LANES), F32)],
        compiler_params=_params("parallel", "parallel", "arbitrary"),
    )(*lams, subln_g, q, kbt, vb)


def _attn_sample_kernel(lq1, lk1, lq2, lk2, sg_ref, q_ref, kn_ref, vn_ref, ck_ref, cv_ref, o_ref, *, past):
    lam = _lam(lq1, lk1, lq2, lk2)
    for h in range(N_HEADS):
        cols = slice(h * HEAD_COLS, (h + 1) * HEAD_COLS)
        q1, q2 = _split_q(q_ref[:, cols])
        kct = ck_ref[0, h * HEAD_COLS:(h + 1) * HEAD_COLS, :].astype(BF16)
        vc = cv_ref[0, pl.ds(h, past, stride=N_HEADS), :].astype(BF16)
        kn = kn_ref[:, cols]
        vn = vn_ref[:, cols]

        def branch(qm):
            sc = jnp.dot(qm, kct, preferred_element_type=F32)
            sn = _scores(qm, kn)
            m = jnp.maximum(jnp.max(sc, axis=-1, keepdims=True), jnp.max(sn, axis=-1, keepdims=True))
            pc = jnp.exp2(sc - m)
            pn = jnp.exp2(sn - m)
            l = jnp.sum(pc, axis=-1, keepdims=True) + jnp.sum(pn, axis=-1, keepdims=True)
            acc = (jnp.dot(pc.astype(BF16), vc, preferred_element_type=F32)
                   + jnp.dot(pn.astype(BF16), vn, preferred_element_type=F32))
            return acc / l

        o = branch(q1) - lam * branch(q2)
        o_ref[:, cols] = _subln(o, sg_ref[...])


def _attn_sample(q, kb, vb, cache_kt, cache_v, lams, subln_g, batch, t):
    past = cache_kt.shape[2]
    small = lambda b: (0, 0)
    row = lambda b: (b, 0)
    cache = lambda b: (b, 0, 0)
    return pl.pallas_call(
        functools.partial(_attn_sample_kernel, past=past),
        grid=(batch,),
        in_specs=[pl.BlockSpec((1, QK_HEAD_DIM), small)] * 4 + [
            pl.BlockSpec((1, V_HEAD_DIM), small),
            pl.BlockSpec((t, D_MODEL), row),
            pl.BlockSpec((t, D_MODEL), row),
            pl.BlockSpec((t, D_MODEL), row),
            pl.BlockSpec((1, D_MODEL, past), cache),
            pl.BlockSpec((1, past * N_HEADS, V_HEAD_DIM), cache)],
        out_specs=pl.BlockSpec((t, D_MODEL), row),
        out_shape=jax.ShapeDtypeStruct((batch * t, D_MODEL), BF16),
        compiler_params=_params("parallel"),
    )(*lams, subln_g, q, kb, vb, cache_kt, cache_v)


def _rglru_kernel(xr_ref, gr_ref, cs_ref, h0_ref, cw_ref, cb_ref, wa_ref, ba_ref, wi_ref, bi_ref, lam_ref,
                  y_ref, nc_ref, hl_ref, xpad, hcar, a_s, b_s, h_s, *, tt):
    t = pl.program_id(1)
    pad = SUBLANES

    @pl.when(t == 0)
    def _():
        xpad[0:pad, :] = jnp.zeros((pad, D_RNN), F32)
        xpad[pad - (CONV_WIDTH - 1):pad, :] = cs_ref[0]
        hcar[...] = h0_ref[0]

    xpad[pad:pad + tt, :] = xr_ref[...]
    xc = cb_ref[...]
    for j in range(CONV_WIDTH):
        off = pad - (CONV_WIDTH - 1) + j
        xc = xc + xpad[off:off + tt, :] * cw_ref[j:j + 1, :]
    xcb = xc.astype(BF16)

    z = -lam_ref[...]
    softplus = jnp.maximum(z, 0.0) + jnp.log1p(jnp.exp(-jnp.abs(z)))
    for n in range(N_RNN_BLOCKS):
        cols = slice(n * RNN_BLOCK, (n + 1) * RNN_BLOCK)
        xs = xcb[:, cols]
        r = _sigmoid(jnp.dot(xs, wa_ref[n], preferred_element_type=F32) + ba_ref[:, cols])
        i = _sigmoid(jnp.dot(xs, wi_ref[n], preferred_element_type=F32) + bi_ref[:, cols])
        log_a = -RG_C * r * softplus[:, cols]
        a_s[:, cols] = jnp.exp(log_a)
        th = jnp.tanh(log_a)
        b_s[:, cols] = jnp.sqrt(-2.0 * th / (1.0 - th)) * (i * xc[:, cols])

    def step(s, h):
        h = a_s[pl.ds(s, 1), :] * h + b_s[pl.ds(s, 1), :]
        h_s[pl.ds(s, 1), :] = h
        return h

    h = lax.fori_loop(0, tt, step, hcar[...], unroll=8)
    hcar[...] = h
    y_ref[...] = (h_s[...] * jax.nn.gelu(gr_ref[...])).astype(BF16)
    nc_ref[0] = xpad[pad + tt - (CONV_WIDTH - 1):pad + tt, :]
    hl_ref[0] = h
    xpad[0:pad, :] = xpad[tt:tt + pad, :]


def _rglru(xr, gr, conv_state, h0, conv_w, conv_b, w_a_b, b_a, w_i_b, b_i, rg_lambda, batch, t, tt):
    nt = t // tt
    row = lambda b, s: (b * nt + s, 0)
    const2 = lambda b, s: (0, 0)
    const3 = lambda b, s: (0, 0, 0)
    per_b = lambda b, s: (b, 0, 0)
    vec = pl.BlockSpec((1, D_RNN), const2)
    blk = pl.BlockSpec((N_RNN_BLOCKS, RNN_BLOCK, RNN_BLOCK), const3)
    big = pltpu.VMEM((tt, D_RNN), F32)
    return pl.pallas_call(
        functools.partial(_rglru_kernel, tt=tt),
        grid=(batch, nt),
        in_specs=[pl.BlockSpec((tt, D_RNN), row), pl.BlockSpec((tt, D_RNN), row),
                  pl.BlockSpec((1, CONV_WIDTH - 1, D_RNN), per_b), pl.BlockSpec((1, 1, D_RNN), per_b),
                  pl.BlockSpec((CONV_WIDTH, D_RNN), const2), vec, blk, vec, blk, vec, vec],
        out_specs=[pl.BlockSpec((tt, D_RNN), row),
                   pl.BlockSpec((1, CONV_WIDTH - 1, D_RNN), per_b),
                   pl.BlockSpec((1, 1, D_RNN), per_b)],
        out_shape=[jax.ShapeDtypeStruct((batch * t, D_RNN), BF16),
                   jax.ShapeDtypeStruct((batch, CONV_WIDTH - 1, D_RNN), F32),
                   jax.ShapeDtypeStruct((batch, 1, D_RNN), F32)],
        scratch_shapes=[pltpu.VMEM((tt + SUBLANES, D_RNN), F32), pltpu.VMEM((1, D_RNN), F32), big, big, big],
        compiler_params=_params("parallel", "arbitrary"),
    )(xr, gr, conv_state, h0, conv_w, conv_b, w_a_b, b_a, w_i_b, b_i, rg_lambda)


def _first_lane_of_max(vals, valid, lane):
    masked = jnp.where(valid, vals, -jnp.inf)
    mx = jnp.max(masked, axis=-1, keepdims=True)
    idx = jnp.min(jnp.where(valid & (masked == mx), lane, ROUTE_LANES), axis=-1, keepdims=True)
    return mx, idx


def _route(logits):
    lane = lax.broadcasted_iota(jnp.int32, logits.shape, 1)
    is_group = lane < N_GROUPS
    gmax, gsel = _first_lane_of_max(logits, is_group, lane)
    gsum = jnp.sum(jnp.where(is_group, jnp.exp(logits - gmax), 0.0), axis=-1, keepdims=True)
    g_w = 1.0 / gsum
    expert = lane - EXPERT_LANE0
    in_group = (expert >= 0) & (expert < N_EXPERTS) & ((expert // EXPERTS_PER_GROUP) == gsel)
    v1, i1 = _first_lane_of_max(logits, in_group, lane)
    v2, i2 = _first_lane_of_max(logits, in_group & (lane != i1), lane)
    e2 = jnp.exp(v2 - v1)
    den = 1.0 + e2
    return jnp.where(lane == i1, (1.0 / den) * g_w, 0.0) + jnp.where(lane == i2, (e2 / den) * g_w, 0.0)


def _merge_kernel(at_ref, y_ref, gt_ref, x_ref, wa_ref, wr_ref, wo_ref, bg_ref, g2_ref, wc_ref, bc_ref,
                  x1_ref, xn_ref, cm_ref, *, parts):
    rows_per_part = x_ref.shape[0] // parts
    for part in range(parts):
        rows = slice(part * rows_per_part, (part + 1) * rows_per_part)
        pa = jnp.dot(at_ref[rows, :], wa_ref[...], preferred_element_type=F32)
        pr = jnp.dot(y_ref[rows, :], wr_ref[...], preferred_element_type=F32)
        g = _sigmoid(gt_ref[rows, :] + bg_ref[...])
        merged = g[:, :D_MODEL] * pa + g[:, D_MODEL:] * pr
        x1 = x_ref[rows, :] + jnp.dot(merged.astype(BF16), wo_ref[...], preferred_element_type=F32)
        x1_ref[rows, :] = x1
        xn = _rms(x1, g2_ref[...]).astype(BF16)
        xn_ref[rows, :] = xn
        cm_ref[rows, :] = _route(jnp.dot(xn, wc_ref[...], preferred_element_type=F32) + bc_ref[...])


def _merge(attn, y_rnn, gates, x2d, wa_b, wr_b, wo_b, b_gate, g2, w_cat_b, b_cat, tm):
    n = x2d.shape[0]
    row = lambda i: (i, 0)
    const = lambda i: (0, 0)
    sq = pl.BlockSpec((D_MODEL, D_MODEL), const)
    return pl.pallas_call(
        functools.partial(_merge_kernel, parts=max(1, tm // MERGE_PART_ROWS)),
        grid=(n // tm,),
        in_specs=[pl.BlockSpec((tm, D_MODEL), row), pl.BlockSpec((tm, D_MODEL), row),
                  pl.BlockSpec((tm, 2 * D_MODEL), row), pl.BlockSpec((tm, D_MODEL), row),
                  sq, sq, sq, pl.BlockSpec((1, 2 * D_MODEL), const), pl.BlockSpec((1, D_MODEL), const),
                  pl.BlockSpec((D_MODEL, ROUTE_LANES), const), pl.BlockSpec((1, ROUTE_LANES), const)],
        out_specs=[pl.BlockSpec((tm, D_MODEL), row), pl.BlockSpec((tm, D_MODEL), row),
                   pl.BlockSpec((tm, ROUTE_LANES), row)],
        out_shape=[jax.ShapeDtypeStruct((n, D_MODEL), F32), jax.ShapeDtypeStruct((n, D_MODEL), BF16),
                   jax.ShapeDtypeStruct((n, ROUTE_LANES), F32)],
        compiler_params=_params("parallel"),
    )(attn, y_rnn, gates, x2d, wa_b, wr_b, wo_b, b_gate, g2, w_cat_b, b_cat)


def _route_tables(comb, tile):
    n = comb.shape[0]
    nt = n // tile
    c = comb[:, EXPERT_LANE0:EXPERT_LANE0 + N_EXPERTS]
    sel = jnp.any(c.reshape(nt, tile, N_GROUPS, EXPERTS_PER_GROUP) != 0.0, axis=-1)
    csum = jnp.cumsum(sel.astype(jnp.int32), axis=1)
    rank = jnp.where(sel, csum - 1, -1).astype(F32)
    counts = csum[:, -1, :].reshape(nt * N_GROUPS)
    rank_t = rank.transpose(0, 2, 1).reshape(nt * N_GROUPS, 1, tile)
    rank = jnp.pad(rank.reshape(n, N_GROUPS), ((0, 0), (0, ROUTE_LANES - N_GROUPS)), constant_values=-1.0)
    hi = c.astype(BF16)
    rest = c - hi.astype(F32)
    mid = rest.astype(BF16)
    lo = (rest - mid.astype(F32)).astype(BF16)
    c3 = jnp.pad(jnp.concatenate([hi, mid, lo], axis=1), ((0, 0), (0, ROUTE_LANES - 3 * N_EXPERTS)))
    return counts, rank, rank_t, c3


def _moe_kernel(cnt_ref, x_ref, c3_ref, rk_ref, rkt_ref, x1_ref, gf_ref, w1_ref, w3_ref, w2_ref, y_ref, xg, cg, yg,
                *, tile):
    t = pl.program_id(0)
    e = pl.program_id(1)
    g = e // EXPERTS_PER_GROUP
    member = e % EXPERTS_PER_GROUP
    count = cnt_ref[t * N_GROUPS + g]
    nblk = (count + MOE_ROWS - 1) // MOE_ROWS

    @pl.when(e == 0)
    def _():
        y_ref[...] = jnp.zeros(y_ref.shape, F32)

    @pl.when(member == 0)
    def _():
        rank_row = rkt_ref[0]

        def gather(b, carry):
            slot = lax.broadcasted_iota(jnp.int32, (MOE_ROWS, tile), 0).astype(F32) + (b * MOE_ROWS).astype(F32)
            onehot = jnp.where(rank_row == slot, 1.0, 0.0).astype(BF16)
            xg[b] = jnp.dot(onehot, x_ref[...], preferred_element_type=F32).astype(BF16)
            cg[b] = jnp.dot(onehot, c3_ref[...], preferred_element_type=F32)
            yg[b] = jnp.zeros((MOE_ROWS, D_MODEL), F32)
            return carry

        lax.fori_loop(0, nblk, gather, 0)

    lane_r = lax.broadcasted_iota(jnp.int32, (MOE_ROWS, ROUTE_LANES), 1)
    mine = (lane_r % N_EXPERTS == e) & (lane_r < 3 * N_EXPERTS)

    def expert(b, carry):
        xc = xg[b]
        cw = jnp.sum(jnp.where(mine, cg[b], 0.0), axis=1, keepdims=True)
        a = jnp.dot(xc, w1_ref[0], preferred_element_type=F32)
        h = (a * _sigmoid(a)) * jnp.dot(xc, w3_ref[0], preferred_element_type=F32)
        yg[b] += cw * jnp.dot(h.astype(BF16), w2_ref[0], preferred_element_type=F32)
        return carry

    lax.fori_loop(0, nblk, expert, 0)

    @pl.when(member == EXPERTS_PER_GROUP - 1)
    def _():
        lane = lax.broadcasted_iota(jnp.int32, (tile, ROUTE_LANES), 1)
        rank_col = jnp.sum(jnp.where(lane == g, rk_ref[...], 0.0), axis=1, keepdims=True)

        def scatter(b, carry):
            slot_t = lax.broadcasted_iota(jnp.int32, (tile, MOE_ROWS), 1).astype(F32) + (b * MOE_ROWS).astype(F32)
            onehot_t = jnp.where(rank_col == slot_t, 1.0, 0.0).astype(BF16)
            y_ref[...] += jnp.dot(onehot_t, yg[b].astype(BF16), preferred_element_type=F32)
            return carry

        lax.fori_loop(0, nblk, scatter, 0)

    @pl.when(e == N_EXPERTS - 1)
    def _():
        y_ref[...] = _rms(x1_ref[...] + y_ref[...], gf_ref[...])


def _moe(xn2, comb, x1, gf, w1_b, w3_b, w2_b, tile):
    n = xn2.shape[0]
    counts, rank, rank_t, c3 = _route_tables(comb, tile)
    max_blocks = -(-tile // MOE_ROWS)
    row = lambda t, e, cnt: (t, 0)
    per_e = lambda t, e, cnt: (e, 0, 0)
    return pl.pallas_call(
        functools.partial(_moe_kernel, tile=tile),
        grid_spec=pltpu.PrefetchScalarGridSpec(
            num_scalar_prefetch=1,
            grid=(n // tile, N_EXPERTS),
            in_specs=[pl.BlockSpec((tile, D_MODEL), row), pl.BlockSpec((tile, ROUTE_LANES), row),
                      pl.BlockSpec((tile, ROUTE_LANES), row),
                      pl.BlockSpec((1, 1, tile), lambda t, e, cnt: (t * N_GROUPS + e // EXPERTS_PER_GROUP, 0, 0)),
                      pl.BlockSpec((tile, D_MODEL), row), pl.BlockSpec((1, D_MODEL), lambda t, e, cnt: (0, 0)),
                      pl.BlockSpec((1, D_MODEL, D_EXPERT), per_e), pl.BlockSpec((1, D_MODEL, D_EXPERT), per_e),
                      pl.BlockSpec((1, D_EXPERT, D_MODEL), per_e)],
            out_specs=pl.BlockSpec((tile, D_MODEL), row),
            scratch_shapes=[pltpu.VMEM((max_blocks, MOE_ROWS, D_MODEL), BF16),
                            pltpu.VMEM((max_blocks, MOE_ROWS, ROUTE_LANES), F32),
                            pltpu.VMEM((max_blocks, MOE_ROWS, D_MODEL), F32)]),
        out_shape=jax.ShapeDtypeStruct((n, D_MODEL), F32),
        compiler_params=_params("parallel", "arbitrary"),
    )(counts, xn2, c3, rank, rank_t, x1, gf, w1_b, w3_b, w2_b)


def _stream(x, past, w, *, tm_in, tm_merge, tm_moe, tq, tt):
    batch, t, _ = x.shape
    n = batch * t
    x2d = x.reshape(n, D_MODEL)
    q, k, v, kb, vb, xr, gr, gates = _in_proj(x2d, w["g1"], w["w_in"], w["w_kt"], tm_in,
                                              seq=t if past is None else None)
    if past is None:
        attn = _attn_prompt(q, kb, vb, w["lams"], w["subln_g"], batch, t, tq)
        k = k.reshape(batch, N_HEADS, 2, QK_HEAD_DIM, t).transpose(0, 4, 1, 2, 3)
        conv_state = jnp.zeros((batch, CONV_WIDTH - 1, D_RNN), F32)
        h0 = jnp.zeros((batch, 1, D_RNN), F32)
    else:
        cache_k, cache_v, conv_state, h0 = past
        attn = _attn_sample(q, kb, vb, cache_k, cache_v, w["lams"], w["subln_g"], batch, t)
    y_rnn, new_conv, h_last = _rglru(xr, gr, conv_state, h0, w["conv_w"], w["conv_b"], w["w_a"], w["b_a"],
                                     w["w_i"], w["b_i"], w["rg_lambda"], batch, t, tt)
    x1, xn2, comb = _merge(attn, y_rnn, gates, x2d, w["w_attn"], w["w_rnn"], w["w_out"], w["b_gate"],
                           w["g2"], w["w_cat"], w["b_cat"], tm_merge)
    y = _moe(xn2, comb, x1, w["gf"], w["w1"], w["w3"], w["w2"], tm_moe)
    return (y.reshape(batch, t, D_MODEL),
            k.reshape(1, batch, t, N_HEADS, 2, QK_HEAD_DIM),
            v.reshape(1, batch, t, N_HEADS, V_HEAD_DIM),
            new_conv.reshape(1, batch, CONV_WIDTH - 1, D_RNN),
            h_last.reshape(1, batch, D_RNN))


def kernel(x_prompt, x_sample, cache_k, cache_v, state_conv, state_rnn, norm1_g, w_in, lambda_q1, lambda_k1, lambda_q2, lambda_k2, subln_g, w_attn_proj, conv_w, conv_b, w_rg_a, b_rg_a, w_rg_i, b_rg_i, rg_lambda, w_rnn_proj, b_gate, w_out, norm2_g, w_group, b_group, w_router, b_router, w1, w3, w2, final_norm_g):
    assert norm1_g.shape[0] == 1, "single-layer model"
    dec_batch, past_len = cache_k.shape[1], cache_k.shape[2]
    pad = ROUTE_LANES - N_GROUPS - N_EXPERTS
    w = {
        "g1": norm1_g, "g2": norm2_g, "gf": final_norm_g.reshape(1, D_MODEL),
        "w_in": w_in[0].astype(BF16), "w_kt": w_in[0, :, D_MODEL:2 * D_MODEL].T.astype(BF16),
        "lams": (lambda_q1, lambda_k1, lambda_q2, lambda_k2),
        "subln_g": subln_g,
        "w_attn": w_attn_proj[0].astype(BF16), "w_rnn": w_rnn_proj[0].astype(BF16),
        "w_out": w_out[0].astype(BF16),
        "conv_w": conv_w[0], "conv_b": conv_b,
        "w_a": w_rg_a[0].astype(BF16), "b_a": b_rg_a, "w_i": w_rg_i[0].astype(BF16), "b_i": b_rg_i,
        "rg_lambda": rg_lambda, "b_gate": b_gate,
        "w_cat": jnp.pad(jnp.concatenate([w_group[0], w_router[0]], axis=1), ((0, 0), (0, pad))).astype(BF16),
        "b_cat": jnp.pad(jnp.concatenate([b_group[0], b_router[0]]), (0, pad)).reshape(1, ROUTE_LANES),
        "w1": w1[0].astype(BF16), "w3": w3[0].astype(BF16), "w2": w2[0].astype(BF16),
    }
    yp, kp, vp, cp, hp = _stream(x_prompt, None, w, tm_in=256, tm_merge=512, tm_moe=1024, tq=512, tt=256)
    past = (cache_k[0].transpose(0, 2, 3, 4, 1).reshape(dec_batch, D_MODEL, past_len),
            cache_v[0].reshape(dec_batch, past_len * N_HEADS, V_HEAD_DIM),
            state_conv[0], state_rnn[0].reshape(dec_batch, 1, D_RNN))
    dec_t = x_sample.shape[1]
    ys, ks, vs, cs, hs = _stream(x_sample, past, w, tm_in=256, tm_merge=256, tm_moe=256, tq=dec_t, tt=dec_t)
    return (yp, ys, kp, vp, cp, hp, ks, vs, cs, hs)
```

```python
import functools
import math

import jax
import jax.numpy as jnp
from jax import lax
from jax.experimental import pallas as pl
from jax.experimental.pallas import tpu as pltpu

F32 = jnp.float32
BF16 = jnp.bfloat16

D_MODEL = 1024
CHUNK = 64
N_HEADS = 8
QK_HEAD_DIM = 64
V_HEAD_DIM = 128
HEAD_COLS = 2 * QK_HEAD_DIM
D_RNN = 1024
N_RNN_BLOCKS = 8
RNN_BLOCK = D_RNN // N_RNN_BLOCKS
CONV_WIDTH = 4
RG_C = 8.0
N_GROUPS = 4
EXPERTS_PER_GROUP = 4
N_EXPERTS = N_GROUPS * EXPERTS_PER_GROUP
D_EXPERT = 512
EPS = 1e-6
LAMBDA_INIT = 0.8 - 0.6 * math.exp(-0.3 * 0)
Q_SCALE = QK_HEAD_DIM ** -0.5 * math.log2(math.e)
N_SEG = 7
IN_WIDTH = N_SEG * D_MODEL
ROUTE_LANES = 128
EXPERTS_PER_STEP = 2
MOE_ROWS = 288
EXPERT_LANE0 = N_GROUPS
SUBLANES = 8
LANES = 128
HEADS_PER_STEP = 8
MERGE_PART_ROWS = 256
VMEM_LIMIT = 56 * 1024 * 1024
NEG = float(jnp.finfo(jnp.float32).min)


def _params(*sem):
    return pltpu.CompilerParams(dimension_semantics=sem, vmem_limit_bytes=VMEM_LIMIT)


def _rms(x, g):
    return x * lax.rsqrt(jnp.mean(x * x, axis=-1, keepdims=True) + EPS) * g


def _sigmoid(x):
    return 0.5 * jnp.tanh(0.5 * x) + 0.5


def _in_proj_kernel(x_ref, g_ref, w_ref, q_ref, k_ref, v_ref, kb_ref, vb_ref, xr_ref, gr_ref, gt_ref,
                    *, k_transposed):
    xn = _rms(x_ref[...], g_ref[...]).astype(BF16)

    def seg(j):
        return jnp.dot(xn, w_ref[:, j * D_MODEL:(j + 1) * D_MODEL], preferred_element_type=F32)

    q_ref[...] = (seg(0) * Q_SCALE).astype(BF16)
    if k_transposed:
        kt = seg(1).T
        k_ref[0] = kt
        kb_ref[0, 0] = kt.astype(BF16)
    else:
        k = seg(1)
        k_ref[...] = k
        kb_ref[...] = k.astype(BF16)
    v = seg(2)
    v_ref[...] = v
    vb_ref[...] = v.astype(BF16)
    xr_ref[...] = seg(3)
    gr_ref[...] = seg(4)
    gt_ref[:, :D_MODEL] = seg(5)
    gt_ref[:, D_MODEL:] = seg(6)


def _in_proj(x2d, g1, w_in_b, tm, seq=None):
    n = x2d.shape[0]
    row = lambda i: (i, 0)
    const = lambda i: (0, 0)
    wide = lambda dt: jax.ShapeDtypeStruct((n, D_MODEL), dt)
    wide_spec = pl.BlockSpec((tm, D_MODEL), row)
    if seq is None:
        k_shape, k_spec, kb_shape, kb_spec = wide(F32), wide_spec, wide(BF16), wide_spec
    else:
        nt = seq // tm
        k_shape = jax.ShapeDtypeStruct((n // seq, D_MODEL, seq), F32)
        k_spec = pl.BlockSpec((1, D_MODEL, tm), lambda i: (i // nt, 0, i % nt))
        kb_shape = jax.ShapeDtypeStruct((n // seq, nt, D_MODEL, tm), BF16)
        kb_spec = pl.BlockSpec((1, 1, D_MODEL, tm), lambda i: (i // nt, i % nt, 0, 0))
    return pl.pallas_call(
        functools.partial(_in_proj_kernel, k_transposed=seq is not None),
        grid=(n // tm,),
        in_specs=[wide_spec,
                  pl.BlockSpec((1, D_MODEL), const),
                  pl.BlockSpec((D_MODEL, IN_WIDTH), const, pipeline_mode=pl.Buffered(1))],
        out_specs=[wide_spec, k_spec, wide_spec, kb_spec, wide_spec, wide_spec, wide_spec,
                   pl.BlockSpec((tm, 2 * D_MODEL), row)],
        out_shape=[wide(BF16), k_shape, wide(F32), kb_shape, wide(BF16), wide(F32), wide(F32),
                   jax.ShapeDtypeStruct((n, 2 * D_MODEL), F32)],
        compiler_params=_params("parallel"),
    )(x2d, g1, w_in_b)


def _lam(lq1, lk1, lq2, lk2):
    return (jnp.exp(jnp.sum(lq1[...] * lk1[...], axis=-1, keepdims=True))
            - jnp.exp(jnp.sum(lq2[...] * lk2[...], axis=-1, keepdims=True)) + LAMBDA_INIT)


def _split_q(q):
    lane = lax.broadcasted_iota(jnp.int32, q.shape, 1)
    zero = jnp.zeros_like(q)
    return jnp.where(lane < QK_HEAD_DIM, q, zero), jnp.where(lane >= QK_HEAD_DIM, q, zero)


def _scores(qm, k):
    return lax.dot_general(qm, k, (((1,), (1,)), ((), ())), preferred_element_type=F32)


def _subln(o, g):
    return (_rms(o, g) * (1.0 - LAMBDA_INIT)).astype(BF16)


def _attn_prompt_kernel(lq1, lk1, lq2, lk2, sg_ref, q_ref, k_ref, v_ref, o_ref, m_ref, acc_ref, *, tq, kw):
    i = pl.program_id(2)
    ones = jnp.ones((tq, LANES), BF16)
    qq = []
    for h in range(HEADS_PER_STEP):
        q1, q2 = _split_q(q_ref[:, h * HEAD_COLS:(h + 1) * HEAD_COLS])
        qq.append(jnp.concatenate([q1, q2], axis=0))

    def tile(kt, h, diagonal):
        start = pl.multiple_of(kt * tq, tq)
        chunks = []
        for j in range(tq // kw):
            kt_blk = k_ref[0, kt * (tq // kw) + j, h * HEAD_COLS:(h + 1) * HEAD_COLS, :]
            s = jnp.dot(qq[h], kt_blk, preferred_element_type=F32)
            chunks += [s[:, c * LANES:(c + 1) * LANES] for c in range(kw // LANES)]
        if diagonal:
            row = lax.broadcasted_iota(jnp.int32, (2 * tq, LANES), 0)
            lane = lax.broadcasted_iota(jnp.int32, (2 * tq, LANES), 1)
            qchunk = (row % tq) // CHUNK
            chunks = [jnp.where((lane + c * LANES) // CHUNK <= qchunk, x, NEG) for c, x in enumerate(chunks)]
        part = chunks[0]
        for x in chunks[1:]:
            part = jnp.maximum(part, x)
        m_tile = jnp.max(part, axis=1, keepdims=True)
        m_next = jnp.broadcast_to(m_tile, (2 * tq, LANES)) if diagonal else jnp.maximum(m_ref[h], m_tile)
        p = jnp.concatenate([jnp.exp2(x - m_next).astype(BF16) for x in chunks], axis=1)
        v1 = jnp.concatenate([v_ref[pl.ds(start, tq), h * V_HEAD_DIM:(h + 1) * V_HEAD_DIM], ones], axis=1)
        pv = jnp.dot(p, v1, preferred_element_type=F32)
        if diagonal:
            acc_ref[h] = pv
        else:
            alpha = jnp.exp2(m_ref[h] - m_next)
            acc_ref[h] = jnp.concatenate([alpha, alpha], axis=1) * acc_ref[h] + pv
        m_ref[h] = m_next

    def body(kt, c):
        for h in range(HEADS_PER_STEP):
            tile(kt, h, False)
        return c

    for h in range(HEADS_PER_STEP):
        tile(i, h, True)
    lax.fori_loop(0, i, body, 0)
    lam = _lam(lq1, lk1, lq2, lk2)
    for h in range(HEADS_PER_STEP):
        o1 = acc_ref[h, :tq, :V_HEAD_DIM] / acc_ref[h, :tq, V_HEAD_DIM:]
        o2 = acc_ref[h, tq:, :V_HEAD_DIM] / acc_ref[h, tq:, V_HEAD_DIM:]
        o_ref[:, h * V_HEAD_DIM:(h + 1) * V_HEAD_DIM] = _subln(o1 - lam * o2, sg_ref[...])


def _attn_prompt(q, kbt, vb, lams, subln_g, batch, seq, tq):
    n = q.shape[0]
    nq = seq // tq
    nkb, kw = kbt.shape[1], kbt.shape[3]
    small = lambda b, h, i: (0, 0)
    qmap = lambda b, h, i: (b * nq + i, h)
    kvmap = lambda b, h, i: (b, h)
    return pl.pallas_call(
        functools.partial(_attn_prompt_kernel, tq=tq, kw=kw),
        grid=(batch, N_HEADS // HEADS_PER_STEP, nq),
        in_specs=[pl.BlockSpec((1, QK_HEAD_DIM), small)] * 4 + [
            pl.BlockSpec((1, V_HEAD_DIM), small),
            pl.BlockSpec((tq, HEADS_PER_STEP * HEAD_COLS), qmap),
            pl.BlockSpec((1, nkb, HEADS_PER_STEP * HEAD_COLS, kw), lambda b, h, i: (b, 0, h, 0)),
            pl.BlockSpec((seq, HEADS_PER_STEP * V_HEAD_DIM), kvmap)],
        out_specs=pl.BlockSpec((tq, HEADS_PER_STEP * V_HEAD_DIM), qmap),
        out_shape=jax.ShapeDtypeStruct((n, N_HEADS * V_HEAD_DIM), BF16),
        scratch_shapes=[pltpu.VMEM((HEADS_PER_STEP, 2 * tq, LANES), F32),
                        pltpu.VMEM((HEADS_PER_STEP, 2 * tq, 2 * LANES), F32)],
        compiler_params=_params("parallel", "parallel", "arbitrary"),
    )(*lams, subln_g, q, kbt, vb)


def _attn_sample_kernel(lq1, lk1, lq2, lk2, sg_ref, q_ref, kn_ref, vn_ref, ck_ref, cv_ref, o_ref, *, past):
    lam = _lam(lq1, lk1, lq2, lk2)
    for h in range(N_HEADS):
        cols = slice(h * HEAD_COLS, (h + 1) * HEAD_COLS)
        q1, q2 = _split_q(q_ref[:, cols])
        kct = ck_ref[0, h * HEAD_COLS:(h + 1) * HEAD_COLS, :].astype(BF16)
        vc = cv_ref[0, pl.ds(h, past, stride=N_HEADS), :].astype(BF16)
        kn = kn_ref[:, cols]
        vn = vn_ref[:, cols]

        def branch(qm):
            sc = jnp.dot(qm, kct, preferred_element_type=F32)
            sn = _scores(qm, kn)
            m = jnp.maximum(jnp.max(sc, axis=-1, keepdims=True), jnp.max(sn, axis=-1, keepdims=True))
            pc = jnp.exp2(sc - m)
            pn = jnp.exp2(sn - m)
            l = jnp.sum(pc, axis=-1, keepdims=True) + jnp.sum(pn, axis=-1, keepdims=True)
            acc = (jnp.dot(pc.astype(BF16), vc, preferred_element_type=F32)
                   + jnp.dot(pn.astype(BF16), vn, preferred_element_type=F32))
            return acc / l

        o = branch(q1) - lam * branch(q2)
        o_ref[:, cols] = _subln(o, sg_ref[...])


def _attn_sample(q, kb, vb, cache_kt, cache_v, lams, subln_g, batch, t):
    past = cache_kt.shape[2]
    small = lambda b: (0, 0)
    row = lambda b: (b, 0)
    cache = lambda b: (b, 0, 0)
    return pl.pallas_call(
        functools.partial(_attn_sample_kernel, past=past),
        grid=(batch,),
        in_specs=[pl.BlockSpec((1, QK_HEAD_DIM), small)] * 4 + [
            pl.BlockSpec((1, V_HEAD_DIM), small),
            pl.BlockSpec((t, D_MODEL), row),
            pl.BlockSpec((t, D_MODEL), row),
            pl.BlockSpec((t, D_MODEL), row),
            pl.BlockSpec((1, D_MODEL, past), cache),
            pl.BlockSpec((1, past * N_HEADS, V_HEAD_DIM), cache)],
        out_specs=pl.BlockSpec((t, D_MODEL), row),
        out_shape=jax.ShapeDtypeStruct((batch * t, D_MODEL), BF16),
        compiler_params=_params("parallel"),
    )(*lams, subln_g, q, kb, vb, cache_kt, cache_v)


def _rglru_kernel(xr_ref, gr_ref, cs_ref, h0_ref, cw_ref, cb_ref, wa_ref, ba_ref, wi_ref, bi_ref, lam_ref,
                  y_ref, nc_ref, hl_ref, xpad, hcar, a_s, b_s, h_s, *, tt):
    t = pl.program_id(1)
    pad = SUBLANES

    @pl.when(t == 0)
    def _():
        xpad[0:pad, :] = jnp.zeros((pad, D_RNN), F32)
        xpad[pad - (CONV_WIDTH - 1):pad, :] = cs_ref[0]
        hcar[...] = h0_ref[0]

    xpad[pad:pad + tt, :] = xr_ref[...]
    xc = cb_ref[...]
    for j in range(CONV_WIDTH):
        off = pad - (CONV_WIDTH - 1) + j
        xc = xc + xpad[off:off + tt, :] * cw_ref[j:j + 1, :]
    xcb = xc.astype(BF16)

    z = -lam_ref[...]
    softplus = jnp.maximum(z, 0.0) + jnp.log1p(jnp.exp(-jnp.abs(z)))
    for n in range(N_RNN_BLOCKS):
        cols = slice(n * RNN_BLOCK, (n + 1) * RNN_BLOCK)
        xs = xcb[:, cols]
        r = _sigmoid(jnp.dot(xs, wa_ref[n], preferred_element_type=F32) + ba_ref[:, cols])
        i = _sigmoid(jnp.dot(xs, wi_ref[n], preferred_element_type=F32) + bi_ref[:, cols])
        log_a = -RG_C * r * softplus[:, cols]
        a_s[:, cols] = jnp.exp(log_a)
        th = jnp.tanh(log_a)
        b_s[:, cols] = jnp.sqrt(-2.0 * th / (1.0 - th)) * (i * xc[:, cols])

    def step(s, h):
        h = a_s[pl.ds(s, 1), :] * h + b_s[pl.ds(s, 1), :]
        h_s[pl.ds(s, 1), :] = h
        return h

    h = lax.fori_loop(0, tt, step, hcar[...], unroll=8)
    hcar[...] = h
    y_ref[...] = (h_s[...] * jax.nn.gelu(gr_ref[...])).astype(BF16)
    nc_ref[0] = xpad[pad + tt - (CONV_WIDTH - 1):pad + tt, :]
    hl_ref[0] = h
    xpad[0:pad, :] = xpad[tt:tt + pad, :]


def _rglru(xr, gr, conv_state, h0, conv_w, conv_b, w_a_b, b_a, w_i_b, b_i, rg_lambda, batch, t, tt):
    nt = t // tt
    row = lambda b, s: (b * nt + s, 0)
    const2 = lambda b, s: (0, 0)
    const3 = lambda b, s: (0, 0, 0)
    per_b = lambda b, s: (b, 0, 0)
    vec = pl.BlockSpec((1, D_RNN), const2)
    blk = pl.BlockSpec((N_RNN_BLOCKS, RNN_BLOCK, RNN_BLOCK), const3)
    big = pltpu.VMEM((tt, D_RNN), F32)
    return pl.pallas_call(
        functools.partial(_rglru_kernel, tt=tt),
        grid=(batch, nt),
        in_specs=[pl.BlockSpec((tt, D_RNN), row), pl.BlockSpec((tt, D_RNN), row),
                  pl.BlockSpec((1, CONV_WIDTH - 1, D_RNN), per_b), pl.BlockSpec((1, 1, D_RNN), per_b),
                  pl.BlockSpec((CONV_WIDTH, D_RNN), const2), vec, blk, vec, blk, vec, vec],
        out_specs=[pl.BlockSpec((tt, D_RNN), row),
                   pl.BlockSpec((1, CONV_WIDTH - 1, D_RNN), per_b),
                   pl.BlockSpec((1, 1, D_RNN), per_b)],
        out_shape=[jax.ShapeDtypeStruct((batch * t, D_RNN), BF16),
                   jax.ShapeDtypeStruct((batch, CONV_WIDTH - 1, D_RNN), F32),
                   jax.ShapeDtypeStruct((batch, 1, D_RNN), F32)],
        scratch_shapes=[pltpu.VMEM((tt + SUBLANES, D_RNN), F32), pltpu.VMEM((1, D_RNN), F32), big, big, big],
        compiler_params=_params("parallel", "arbitrary"),
    )(xr, gr, conv_state, h0, conv_w, conv_b, w_a_b, b_a, w_i_b, b_i, rg_lambda)


def _first_lane_of_max(vals, valid, lane):
    masked = jnp.where(valid, vals, -jnp.inf)
    mx = jnp.max(masked, axis=-1, keepdims=True)
    idx = jnp.min(jnp.where(valid & (masked == mx), lane, ROUTE_LANES), axis=-1, keepdims=True)
    return mx, idx


def _route(logits):
    lane = lax.broadcasted_iota(jnp.int32, logits.shape, 1)
    is_group = lane < N_GROUPS
    gmax, gsel = _first_lane_of_max(logits, is_group, lane)
    gsum = jnp.sum(jnp.where(is_group, jnp.exp(logits - gmax), 0.0), axis=-1, keepdims=True)
    g_w = 1.0 / gsum
    expert = lane - EXPERT_LANE0
    in_group = (expert >= 0) & (expert < N_EXPERTS) & ((expert // EXPERTS_PER_GROUP) == gsel)
    v1, i1 = _first_lane_of_max(logits, in_group, lane)
    v2, i2 = _first_lane_of_max(logits, in_group & (lane != i1), lane)
    e2 = jnp.exp(v2 - v1)
    den = 1.0 + e2
    return jnp.where(lane == i1, (1.0 / den) * g_w, 0.0) + jnp.where(lane == i2, (e2 / den) * g_w, 0.0)


def _merge_kernel(at_ref, y_ref, gt_ref, x_ref, wa_ref, wr_ref, wo_ref, bg_ref, g2_ref, wc_ref, bc_ref,
                  x1_ref, xn_ref, cm_ref, *, parts):
    rows_per_part = x_ref.shape[0] // parts
    for part in range(parts):
        rows = slice(part * rows_per_part, (part + 1) * rows_per_part)
        pa = jnp.dot(at_ref[rows, :], wa_ref[...], preferred_element_type=F32)
        pr = jnp.dot(y_ref[rows, :], wr_ref[...], preferred_element_type=F32)
        g = _sigmoid(gt_ref[rows, :] + bg_ref[...])
        merged = g[:, :D_MODEL] * pa + g[:, D_MODEL:] * pr
        x1 = x_ref[rows, :] + jnp.dot(merged.astype(BF16), wo_ref[...], preferred_element_type=F32)
        x1_ref[rows, :] = x1
        xn = _rms(x1, g2_ref[...]).astype(BF16)
        xn_ref[rows, :] = xn
        cm_ref[rows, :] = _route(jnp.dot(xn, wc_ref[...], preferred_element_type=F32) + bc_ref[...])


def _merge(attn, y_rnn, gates, x2d, wa_b, wr_b, wo_b, b_gate, g2, w_cat_b, b_cat, tm):
    n = x2d.shape[0]
    row = lambda i: (i, 0)
    const = lambda i: (0, 0)
    sq = pl.BlockSpec((D_MODEL, D_MODEL), const)
    return pl.pallas_call(
        functools.partial(_merge_kernel, parts=max(1, tm // MERGE_PART_ROWS)),
        grid=(n // tm,),
        in_specs=[pl.BlockSpec((tm, D_MODEL), row), pl.BlockSpec((tm, D_MODEL), row),
                  pl.BlockSpec((tm, 2 * D_MODEL), row), pl.BlockSpec((tm, D_MODEL), row),
                  sq, sq, sq, pl.BlockSpec((1, 2 * D_MODEL), const), pl.BlockSpec((1, D_MODEL), const),
                  pl.BlockSpec((D_MODEL, ROUTE_LANES), const), pl.BlockSpec((1, ROUTE_LANES), const)],
        out_specs=[pl.BlockSpec((tm, D_MODEL), row), pl.BlockSpec((tm, D_MODEL), row),
                   pl.BlockSpec((tm, ROUTE_LANES), row)],
        out_shape=[jax.ShapeDtypeStruct((n, D_MODEL), F32), jax.ShapeDtypeStruct((n, D_MODEL), BF16),
                   jax.ShapeDtypeStruct((n, ROUTE_LANES), F32)],
        compiler_params=_params("parallel"),
    )(attn, y_rnn, gates, x2d, wa_b, wr_b, wo_b, b_gate, g2, w_cat_b, b_cat)


def _route_tables(comb, tile):
    n = comb.shape[0]
    nt = n // tile
    c = comb[:, EXPERT_LANE0:EXPERT_LANE0 + N_EXPERTS]
    sel = jnp.any(c.reshape(nt, tile, N_GROUPS, EXPERTS_PER_GROUP) != 0.0, axis=-1)
    csum = jnp.cumsum(sel.astype(jnp.int32), axis=1)
    rank = jnp.where(sel, csum - 1, -1).astype(F32)
    counts = csum[:, -1, :].reshape(nt * N_GROUPS)
    rank_t = rank.transpose(0, 2, 1).reshape(nt * N_GROUPS, 1, tile)
    rank = jnp.pad(rank.reshape(n, N_GROUPS), ((0, 0), (0, ROUTE_LANES - N_GROUPS)), constant_values=-1.0)
    hi = c.astype(BF16)
    rest = c - hi.astype(F32)
    mid = rest.astype(BF16)
    lo = (rest - mid.astype(F32)).astype(BF16)
    c3 = jnp.pad(jnp.concatenate([hi, mid, lo], axis=1), ((0, 0), (0, ROUTE_LANES - 3 * N_EXPERTS)))
    return counts, rank, rank_t, c3


def _moe_kernel(cnt_ref, x_ref, c3_ref, rk_ref, rkt_ref, x1_ref, gf_ref, w1_ref, w3_ref, w2_ref, y_ref, xg, cg, yg,
                *, tile):
    t = pl.program_id(0)
    step = pl.program_id(1)
    steps_per_group = EXPERTS_PER_GROUP // EXPERTS_PER_STEP
    g = step // steps_per_group
    member = step % steps_per_group
    count = cnt_ref[t * N_GROUPS + g]
    nblk = (count + MOE_ROWS - 1) // MOE_ROWS

    @pl.when(step == 0)
    def _():
        y_ref[...] = jnp.zeros(y_ref.shape, F32)

    @pl.when(member == 0)
    def _():
        rank_row = rkt_ref[0]

        def gather(b, carry):
            slot = lax.broadcasted_iota(jnp.int32, (MOE_ROWS, tile), 0).astype(F32) + (b * MOE_ROWS).astype(F32)
            onehot = jnp.where(rank_row == slot, 1.0, 0.0).astype(BF16)
            xg[b] = jnp.dot(onehot, x_ref[...], preferred_element_type=F32).astype(BF16)
            cg[b] = jnp.dot(onehot, c3_ref[...], preferred_element_type=F32)
            yg[b] = jnp.zeros((MOE_ROWS, D_MODEL), F32)
            return carry

        lax.fori_loop(0, nblk, gather, 0)

    lane_r = lax.broadcasted_iota(jnp.int32, (MOE_ROWS, ROUTE_LANES), 1)

    def experts(b, carry):
        xc = xg[b]
        out = yg[b]
        for j in range(EXPERTS_PER_STEP):
            mine = (lane_r % N_EXPERTS == step * EXPERTS_PER_STEP + j) & (lane_r < 3 * N_EXPERTS)
            cw = jnp.sum(jnp.where(mine, cg[b], 0.0), axis=1, keepdims=True)
            a = jnp.dot(xc, w1_ref[j], preferred_element_type=F32)
            h = (a * _sigmoid(a)) * jnp.dot(xc, w3_ref[j], preferred_element_type=F32)
            out = out + cw * jnp.dot(h.astype(BF16), w2_ref[j], preferred_element_type=F32)
        yg[b] = out
        return carry

    lax.fori_loop(0, nblk, experts, 0)

    @pl.when(member == steps_per_group - 1)
    def _():
        lane = lax.broadcasted_iota(jnp.int32, (tile, ROUTE_LANES), 1)
        rank_col = jnp.sum(jnp.where(lane == g, rk_ref[...], 0.0), axis=1, keepdims=True)

        def scatter(b, carry):
            slot_t = lax.broadcasted_iota(jnp.int32, (tile, MOE_ROWS), 1).astype(F32) + (b * MOE_ROWS).astype(F32)
            onehot_t = jnp.where(rank_col == slot_t, 1.0, 0.0).astype(BF16)
            y_ref[...] += jnp.dot(onehot_t, yg[b].astype(BF16), preferred_element_type=F32)
            return carry

        lax.fori_loop(0, nblk, scatter, 0)

    @pl.when(step == pl.num_programs(1) - 1)
    def _():
        y_ref[...] = _rms(x1_ref[...] + y_ref[...], gf_ref[...])


def _moe(xn2, comb, x1, gf, w1_b, w3_b, w2_b, tile):
    n = xn2.shape[0]
    counts, rank, rank_t, c3 = _route_tables(comb, tile)
    max_blocks = -(-tile // MOE_ROWS)
    row = lambda t, s, cnt: (t, 0)
    per_step = lambda t, s, cnt: (s, 0, 0)
    steps_per_group = EXPERTS_PER_GROUP // EXPERTS_PER_STEP
    return pl.pallas_call(
        functools.partial(_moe_kernel, tile=tile),
        grid_spec=pltpu.PrefetchScalarGridSpec(
            num_scalar_prefetch=1,
            grid=(n // tile, N_EXPERTS // EXPERTS_PER_STEP),
            in_specs=[pl.BlockSpec((tile, D_MODEL), row), pl.BlockSpec((tile, ROUTE_LANES), row),
                      pl.BlockSpec((tile, ROUTE_LANES), row),
                      pl.BlockSpec((1, 1, tile), lambda t, s, cnt: (t * N_GROUPS + s // steps_per_group, 0, 0)),
                      pl.BlockSpec((tile, D_MODEL), row), pl.BlockSpec((1, D_MODEL), lambda t, s, cnt: (0, 0)),
                      pl.BlockSpec((EXPERTS_PER_STEP, D_MODEL, D_EXPERT), per_step),
                      pl.BlockSpec((EXPERTS_PER_STEP, D_MODEL, D_EXPERT), per_step),
                      pl.BlockSpec((EXPERTS_PER_STEP, D_EXPERT, D_MODEL), per_step)],
            out_specs=pl.BlockSpec((tile, D_MODEL), row),
            scratch_shapes=[pltpu.VMEM((max_blocks, MOE_ROWS, D_MODEL), BF16),
                            pltpu.VMEM((max_blocks, MOE_ROWS, ROUTE_LANES), F32),
                            pltpu.VMEM((max_blocks, MOE_ROWS, D_MODEL), F32)]),
        out_shape=jax.ShapeDtypeStruct((n, D_MODEL), F32),
        compiler_params=_params("parallel", "arbitrary"),
    )(counts, xn2, c3, rank, rank_t, x1, gf, w1_b, w3_b, w2_b)


def _stream(x, past, w, *, tm_in, tm_merge, tm_moe, tq, tt):
    batch, t, _ = x.shape
    n = batch * t
    x2d = x.reshape(n, D_MODEL)
    q, k, v, kb, vb, xr, gr, gates = _in_proj(x2d, w["g1"], w["w_in"], tm_in,
                                              seq=t if past is None else None)
    if past is None:
        attn = _attn_prompt(q, kb, vb, w["lams"], w["subln_g"], batch, t, tq)
        k = k.reshape(batch, N_HEADS, 2, QK_HEAD_DIM, t).transpose(0, 4, 1, 2, 3)
        conv_state = jnp.zeros((batch, CONV_WIDTH - 1, D_RNN), F32)
        h0 = jnp.zeros((batch, 1, D_RNN), F32)
    else:
        cache_k, cache_v, conv_state, h0 = past
        attn = _attn_sample(q, kb, vb, cache_k, cache_v, w["lams"], w["subln_g"], batch, t)
    y_rnn, new_conv, h_last = _rglru(xr, gr, conv_state, h0, w["conv_w"], w["conv_b"], w["w_a"], w["b_a"],
                                     w["w_i"], w["b_i"], w["rg_lambda"], batch, t, tt)
    x1, xn2, comb = _merge(attn, y_rnn, gates, x2d, w["w_attn"], w["w_rnn"], w["w_out"], w["b_gate"],
                           w["g2"], w["w_cat"], w["b_cat"], tm_merge)
    y = _moe(xn2, comb, x1, w["gf"], w["w1"], w["w3"], w["w2"], tm_moe)
    return (y.reshape(batch, t, D_MODEL),
            k.reshape(1, batch, t, N_HEADS, 2, QK_HEAD_DIM),
            v.reshape(1, batch, t, N_HEADS, V_HEAD_DIM),
            new_conv.reshape(1, batch, CONV_WIDTH - 1, D_RNN),
            h_last.reshape(1, batch, D_RNN))


def kernel(x_prompt, x_sample, cache_k, cache_v, state_conv, state_rnn, norm1_g, w_in, lambda_q1, lambda_k1, lambda_q2, lambda_k2, subln_g, w_attn_proj, conv_w, conv_b, w_rg_a, b_rg_a, w_rg_i, b_rg_i, rg_lambda, w_rnn_proj, b_gate, w_out, norm2_g, w_group, b_group, w_router, b_router, w1, w3, w2, final_norm_g):
    assert norm1_g.shape[0] == 1, "single-layer model"
    dec_batch, past_len = cache_k.shape[1], cache_k.shape[2]
    pad = ROUTE_LANES - N_GROUPS - N_EXPERTS
    w = {
        "g1": norm1_g, "g2": norm2_g, "gf": final_norm_g.reshape(1, D_MODEL),
        "w_in": w_in[0].astype(BF16),
        "lams": (lambda_q1, lambda_k1, lambda_q2, lambda_k2),
        "subln_g": subln_g,
        "w_attn": w_attn_proj[0].astype(BF16), "w_rnn": w_rnn_proj[0].astype(BF16),
        "w_out": w_out[0].astype(BF16),
        "conv_w": conv_w[0], "conv_b": conv_b,
        "w_a": w_rg_a[0].astype(BF16), "b_a": b_rg_a, "w_i": w_rg_i[0].astype(BF16), "b_i": b_rg_i,
        "rg_lambda": rg_lambda, "b_gate": b_gate,
        "w_cat": jnp.pad(jnp.concatenate([w_group[0], w_router[0]], axis=1), ((0, 0), (0, pad))).astype(BF16),
        "b_cat": jnp.pad(jnp.concatenate([b_group[0], b_router[0]]), (0, pad)).reshape(1, ROUTE_LANES),
        "w1": w1[0].astype(BF16), "w3": w3[0].astype(BF16), "w2": w2[0].astype(BF16),
    }
    yp, kp, vp, cp, hp = _stream(x_prompt, None, w, tm_in=256, tm_merge=512, tm_moe=1024, tq=512, tt=256)
    past = (cache_k[0].transpose(0, 2, 3, 4, 1).reshape(dec_batch, D_MODEL, past_len),
            cache_v[0].reshape(dec_batch, past_len * N_HEADS, V_HEAD_DIM),
            state_conv[0], state_rnn[0].reshape(dec_batch, 1, D_RNN))
    dec_t = x_sample.shape[1]
    ys, ks, vs, cs, hs = _stream(x_sample, past, w, tm_in=256, tm_merge=256, tm_moe=256, tq=dec_t, tt=dec_t)
    return (yp, ys, kp, vp, cp, hp, ks, vs, cs, hs)
```

```python
import functools
import math

import jax
import jax.numpy as jnp
from jax import lax
from jax.experimental import pallas as pl
from jax.experimental.pallas import tpu as pltpu

F32 = jnp.float32
BF16 = jnp.bfloat16

D_MODEL = 1024
CHUNK = 64
N_HEADS = 8
QK_HEAD_DIM = 64
V_HEAD_DIM = 128
HEAD_COLS = 2 * QK_HEAD_DIM
D_RNN = 1024
N_RNN_BLOCKS = 8
RNN_BLOCK = D_RNN // N_RNN_BLOCKS
CONV_WIDTH = 4
RG_C = 8.0
N_GROUPS = 4
EXPERTS_PER_GROUP = 4
N_EXPERTS = N_GROUPS * EXPERTS_PER_GROUP
D_EXPERT = 512
EPS = 1e-6
LAMBDA_INIT = 0.8 - 0.6 * math.exp(-0.3 * 0)
Q_SCALE = QK_HEAD_DIM ** -0.5 * math.log2(math.e)
N_SEG = 7
IN_WIDTH = N_SEG * D_MODEL
ROUTE_LANES = 128
EXPERTS_PER_STEP = 2
MOE_ROWS = 288
EXPERT_LANE0 = N_GROUPS
SUBLANES = 8
LANES = 128
HEADS_PER_STEP = 8
MERGE_PART_ROWS = 256
VMEM_LIMIT = 56 * 1024 * 1024
NEG = float(jnp.finfo(jnp.float32).min)


def _params(*sem):
    return pltpu.CompilerParams(dimension_semantics=sem, vmem_limit_bytes=VMEM_LIMIT)


def _rms(x, g):
    return x * lax.rsqrt(jnp.mean(x * x, axis=-1, keepdims=True) + EPS) * g


def _sigmoid(x):
    return 0.5 * jnp.tanh(0.5 * x) + 0.5


def _in_proj_kernel(x_ref, g_ref, w_ref, q_ref, k_ref, v_ref, kb_ref, vb_ref, xr_ref, gr_ref, gt_ref,
                    *, k_transposed):
    xn = _rms(x_ref[...], g_ref[...]).astype(BF16)

    def seg(j):
        return jnp.dot(xn, w_ref[:, j * D_MODEL:(j + 1) * D_MODEL], preferred_element_type=F32)

    q_ref[...] = (seg(0) * Q_SCALE).astype(BF16)
    if k_transposed:
        kt = seg(1).T
        k_ref[0] = kt
        kb_ref[0, 0] = kt.astype(BF16)
    else:
        k = seg(1)
        k_ref[...] = k
        kb_ref[...] = k.astype(BF16)
    v = seg(2)
    v_ref[...] = v
    vb_ref[...] = v.astype(BF16)
    xr_ref[...] = seg(3)
    gr_ref[...] = seg(4)
    gt_ref[:, :D_MODEL] = seg(5)
    gt_ref[:, D_MODEL:] = seg(6)


def _in_proj(x2d, g1, w_in_b, tm, seq=None):
    n = x2d.shape[0]
    row = lambda i: (i, 0)
    const = lambda i: (0, 0)
    wide = lambda dt: jax.ShapeDtypeStruct((n, D_MODEL), dt)
    wide_spec = pl.BlockSpec((tm, D_MODEL), row)
    if seq is None:
        k_shape, k_spec, kb_shape, kb_spec = wide(F32), wide_spec, wide(BF16), wide_spec
    else:
        nt = seq // tm
        k_shape = jax.ShapeDtypeStruct((n // seq, D_MODEL, seq), F32)
        k_spec = pl.BlockSpec((1, D_MODEL, tm), lambda i: (i // nt, 0, i % nt))
        kb_shape = jax.ShapeDtypeStruct((n // seq, nt, D_MODEL, tm), BF16)
        kb_spec = pl.BlockSpec((1, 1, D_MODEL, tm), lambda i: (i // nt, i % nt, 0, 0))
    return pl.pallas_call(
        functools.partial(_in_proj_kernel, k_transposed=seq is not None),
        grid=(n // tm,),
        in_specs=[wide_spec,
                  pl.BlockSpec((1, D_MODEL), const),
                  pl.BlockSpec((D_MODEL, IN_WIDTH), const, pipeline_mode=pl.Buffered(1))],
        out_specs=[wide_spec, k_spec, wide_spec, kb_spec, wide_spec, wide_spec, wide_spec,
                   pl.BlockSpec((tm, 2 * D_MODEL), row)],
        out_shape=[wide(BF16), k_shape, wide(F32), kb_shape, wide(BF16), wide(F32), wide(F32),
                   jax.ShapeDtypeStruct((n, 2 * D_MODEL), F32)],
        compiler_params=_params("parallel"),
    )(x2d, g1, w_in_b)


def _lam(lq1, lk1, lq2, lk2):
    return (jnp.exp(jnp.sum(lq1[...] * lk1[...], axis=-1, keepdims=True))
            - jnp.exp(jnp.sum(lq2[...] * lk2[...], axis=-1, keepdims=True)) + LAMBDA_INIT)


def _split_q(q):
    lane = lax.broadcasted_iota(jnp.int32, q.shape, 1)
    zero = jnp.zeros_like(q)
    return jnp.where(lane < QK_HEAD_DIM, q, zero), jnp.where(lane >= QK_HEAD_DIM, q, zero)


def _scores(qm, k):
    return lax.dot_general(qm, k, (((1,), (1,)), ((), ())), preferred_element_type=F32)


def _subln(o, g):
    return (_rms(o, g) * (1.0 - LAMBDA_INIT)).astype(BF16)


def _attn_prompt_kernel(lq1, lk1, lq2, lk2, sg_ref, q_ref, k_ref, v_ref, o_ref, m_ref, acc_ref, *, tq, kw):
    i = pl.program_id(2)
    ones = jnp.ones((tq, LANES), BF16)
    qq = []
    for h in range(HEADS_PER_STEP):
        q1, q2 = _split_q(q_ref[:, h * HEAD_COLS:(h + 1) * HEAD_COLS])
        qq.append(jnp.concatenate([q1, q2], axis=0))

    def row_max(chunks):
        part = chunks[0]
        for x in chunks[1:]:
            part = jnp.maximum(part, x)
        return jnp.max(part, axis=1, keepdims=True)

    def split(s):
        return [s[:, c * LANES:(c + 1) * LANES] for c in range(s.shape[1] // LANES)]

    def head_cols(h, width):
        return slice(h * width, (h + 1) * width)

    def diagonal_tile(h):
        row = lax.broadcasted_iota(jnp.int32, (2 * kw, LANES), 0)
        lane = lax.broadcasted_iota(jnp.int32, (2 * kw, LANES), 1)
        qchunk = (row % kw) // CHUNK
        for r in range(tq // kw):
            rows1, rows2 = slice(r * kw, (r + 1) * kw), slice(tq + r * kw, tq + (r + 1) * kw)
            qr = jnp.concatenate([qq[h][rows1], qq[h][rows2]], axis=0)
            chunks = []
            for j in range(r + 1):
                s = split(jnp.dot(qr, k_ref[0, i * (tq // kw) + j, head_cols(h, HEAD_COLS), :],
                                  preferred_element_type=F32))
                if j == r:
                    s = [jnp.where((lane + c * LANES) // CHUNK <= qchunk, x, NEG) for c, x in enumerate(s)]
                chunks += s
            m_new = jnp.broadcast_to(row_max(chunks), (2 * kw, LANES))
            p = jnp.concatenate([jnp.exp2(x - m_new).astype(BF16) for x in chunks], axis=1)
            keys = (r + 1) * kw
            start = pl.multiple_of(i * tq, tq)
            v1 = jnp.concatenate([v_ref[pl.ds(start, keys), head_cols(h, V_HEAD_DIM)],
                                  jnp.ones((keys, LANES), BF16)], axis=1)
            pv = jnp.dot(p, v1, preferred_element_type=F32)
            for rows, part in ((rows1, slice(0, kw)), (rows2, slice(kw, 2 * kw))):
                acc_ref[h, rows, :] = pv[part]
                m_ref[h, rows, :] = m_new[part]

    def tile(kt, h):
        start = pl.multiple_of(kt * tq, tq)
        chunks = []
        for j in range(tq // kw):
            chunks += split(jnp.dot(qq[h], k_ref[0, kt * (tq // kw) + j, head_cols(h, HEAD_COLS), :],
                                    preferred_element_type=F32))
        m_next = jnp.maximum(m_ref[h], row_max(chunks))
        p = jnp.concatenate([jnp.exp2(x - m_next).astype(BF16) for x in chunks], axis=1)
        v1 = jnp.concatenate([v_ref[pl.ds(start, tq), head_cols(h, V_HEAD_DIM)], ones], axis=1)
        alpha = jnp.exp2(m_ref[h] - m_next)
        acc_ref[h] = (jnp.concatenate([alpha, alpha], axis=1) * acc_ref[h]
                      + jnp.dot(p, v1, preferred_element_type=F32))
        m_ref[h] = m_next

    def body(kt, c):
        for h in range(HEADS_PER_STEP):
            tile(kt, h)
        return c

    for h in range(HEADS_PER_STEP):
        diagonal_tile(h)
    lax.fori_loop(0, i, body, 0)
    lam = _lam(lq1, lk1, lq2, lk2)
    for h in range(HEADS_PER_STEP):
        o1 = acc_ref[h, :tq, :V_HEAD_DIM] / acc_ref[h, :tq, V_HEAD_DIM:]
        o2 = acc_ref[h, tq:, :V_HEAD_DIM] / acc_ref[h, tq:, V_HEAD_DIM:]
        o_ref[:, h * V_HEAD_DIM:(h + 1) * V_HEAD_DIM] = _subln(o1 - lam * o2, sg_ref[...])


def _attn_prompt(q, kbt, vb, lams, subln_g, batch, seq, tq):
    n = q.shape[0]
    nq = seq // tq
    nkb, kw = kbt.shape[1], kbt.shape[3]
    small = lambda b, h, i: (0, 0)
    qmap = lambda b, h, i: (b * nq + i, h)
    kvmap = lambda b, h, i: (b, h)
    return pl.pallas_call(
        functools.partial(_attn_prompt_kernel, tq=tq, kw=kw),
        grid=(batch, N_HEADS // HEADS_PER_STEP, nq),
        in_specs=[pl.BlockSpec((1, QK_HEAD_DIM), small)] * 4 + [
            pl.BlockSpec((1, V_HEAD_DIM), small),
            pl.BlockSpec((tq, HEADS_PER_STEP * HEAD_COLS), qmap),
            pl.BlockSpec((1, nkb, HEADS_PER_STEP * HEAD_COLS, kw), lambda b, h, i: (b, 0, h, 0)),
            pl.BlockSpec((seq, HEADS_PER_STEP * V_HEAD_DIM), kvmap)],
        out_specs=pl.BlockSpec((tq, HEADS_PER_STEP * V_HEAD_DIM), qmap),
        out_shape=jax.ShapeDtypeStruct((n, N_HEADS * V_HEAD_DIM), BF16),
        scratch_shapes=[pltpu.VMEM((HEADS_PER_STEP, 2 * tq, LANES), F32),
                        pltpu.VMEM((HEADS_PER_STEP, 2 * tq, 2 * LANES), F32)],
        compiler_params=_params("parallel", "parallel", "arbitrary"),
    )(*lams, subln_g, q, kbt, vb)


def _attn_sample_kernel(lq1, lk1, lq2, lk2, sg_ref, q_ref, kn_ref, vn_ref, ck_ref, cv_ref, o_ref, *, past):
    lam = _lam(lq1, lk1, lq2, lk2)
    for h in range(N_HEADS):
        cols = slice(h * HEAD_COLS, (h + 1) * HEAD_COLS)
        qq = jnp.concatenate(_split_q(q_ref[:, cols]), axis=0)
        kct = ck_ref[0, h * HEAD_COLS:(h + 1) * HEAD_COLS, :].astype(BF16)
        vc = cv_ref[0, pl.ds(h, past, stride=N_HEADS), :].astype(BF16)
        sc = jnp.dot(qq, kct, preferred_element_type=F32)
        sn = _scores(qq, kn_ref[:, cols])
        m = jnp.maximum(jnp.max(sc, axis=-1, keepdims=True), jnp.max(sn, axis=-1, keepdims=True))
        pc = jnp.exp2(sc - m)
        pn = jnp.exp2(sn - m)
        l = jnp.sum(pc, axis=-1, keepdims=True) + jnp.sum(pn, axis=-1, keepdims=True)
        acc = (jnp.dot(pc.astype(BF16), vc, preferred_element_type=F32)
               + jnp.dot(pn.astype(BF16), vn_ref[:, cols], preferred_element_type=F32))
        o = acc / l
        t = q_ref.shape[0]
        o_ref[:, cols] = _subln(o[:t] - lam * o[t:], sg_ref[...])


def _attn_sample(q, kb, vb, cache_kt, cache_v, lams, subln_g, batch, t):
    past = cache_kt.shape[2]
    small = lambda b: (0, 0)
    row = lambda b: (b, 0)
    cache = lambda b: (b, 0, 0)
    return pl.pallas_call(
        functools.partial(_attn_sample_kernel, past=past),
        grid=(batch,),
        in_specs=[pl.BlockSpec((1, QK_HEAD_DIM), small)] * 4 + [
            pl.BlockSpec((1, V_HEAD_DIM), small),
            pl.BlockSpec((t, D_MODEL), row),
            pl.BlockSpec((t, D_MODEL), row),
            pl.BlockSpec((t, D_MODEL), row),
            pl.BlockSpec((1, D_MODEL, past), cache),
            pl.BlockSpec((1, past * N_HEADS, V_HEAD_DIM), cache)],
        out_specs=pl.BlockSpec((t, D_MODEL), row),
        out_shape=jax.ShapeDtypeStruct((batch * t, D_MODEL), BF16),
        compiler_params=_params("parallel"),
    )(*lams, subln_g, q, kb, vb, cache_kt, cache_v)


def _rglru_kernel(xr_ref, gr_ref, cs_ref, h0_ref, cw_ref, cb_ref, wa_ref, ba_ref, wi_ref, bi_ref, lam_ref,
                  y_ref, nc_ref, hl_ref, xpad, hcar, a_s, b_s, h_s, *, tt):
    t = pl.program_id(1)
    pad = SUBLANES

    @pl.when(t == 0)
    def _():
        xpad[0:pad, :] = jnp.zeros((pad, D_RNN), F32)
        xpad[pad - (CONV_WIDTH - 1):pad, :] = cs_ref[0]
        hcar[...] = h0_ref[0]

    xpad[pad:pad + tt, :] = xr_ref[...]
    xc = cb_ref[...]
    for j in range(CONV_WIDTH):
        off = pad - (CONV_WIDTH - 1) + j
        xc = xc + xpad[off:off + tt, :] * cw_ref[j:j + 1, :]
    xcb = xc.astype(BF16)

    z = -lam_ref[...]
    softplus = jnp.maximum(z, 0.0) + jnp.log1p(jnp.exp(-jnp.abs(z)))
    for n in range(N_RNN_BLOCKS):
        cols = slice(n * RNN_BLOCK, (n + 1) * RNN_BLOCK)
        xs = xcb[:, cols]
        r = _sigmoid(jnp.dot(xs, wa_ref[n], preferred_element_type=F32) + ba_ref[:, cols])
        i = _sigmoid(jnp.dot(xs, wi_ref[n], preferred_element_type=F32) + bi_ref[:, cols])
        log_a = -RG_C * r * softplus[:, cols]
        a_s[:, cols] = jnp.exp(log_a)
        th = jnp.tanh(log_a)
        b_s[:, cols] = jnp.sqrt(-2.0 * th / (1.0 - th)) * (i * xc[:, cols])

    def step(s, h):
        h = a_s[pl.ds(s, 1), :] * h + b_s[pl.ds(s, 1), :]
        h_s[pl.ds(s, 1), :] = h
        return h

    h = lax.fori_loop(0, tt, step, hcar[...], unroll=8)
    hcar[...] = h
    y_ref[...] = (h_s[...] * jax.nn.gelu(gr_ref[...])).astype(BF16)
    nc_ref[0] = xpad[pad + tt - (CONV_WIDTH - 1):pad + tt, :]
    hl_ref[0] = h
    xpad[0:pad, :] = xpad[tt:tt + pad, :]


def _rglru(xr, gr, conv_state, h0, conv_w, conv_b, w_a_b, b_a, w_i_b, b_i, rg_lambda, batch, t, tt):
    nt = t // tt
    row = lambda b, s: (b * nt + s, 0)
    const2 = lambda b, s: (0, 0)
    const3 = lambda b, s: (0, 0, 0)
    per_b = lambda b, s: (b, 0, 0)
    vec = pl.BlockSpec((1, D_RNN), const2)
    blk = pl.BlockSpec((N_RNN_BLOCKS, RNN_BLOCK, RNN_BLOCK), const3)
    big = pltpu.VMEM((tt, D_RNN), F32)
    return pl.pallas_call(
        functools.partial(_rglru_kernel, tt=tt),
        grid=(batch, nt),
        in_specs=[pl.BlockSpec((tt, D_RNN), row), pl.BlockSpec((tt, D_RNN), row),
                  pl.BlockSpec((1, CONV_WIDTH - 1, D_RNN), per_b), pl.BlockSpec((1, 1, D_RNN), per_b),
                  pl.BlockSpec((CONV_WIDTH, D_RNN), const2), vec, blk, vec, blk, vec, vec],
        out_specs=[pl.BlockSpec((tt, D_RNN), row),
                   pl.BlockSpec((1, CONV_WIDTH - 1, D_RNN), per_b),
                   pl.BlockSpec((1, 1, D_RNN), per_b)],
        out_shape=[jax.ShapeDtypeStruct((batch * t, D_RNN), BF16),
                   jax.ShapeDtypeStruct((batch, CONV_WIDTH - 1, D_RNN), F32),
                   jax.ShapeDtypeStruct((batch, 1, D_RNN), F32)],
        scratch_shapes=[pltpu.VMEM((tt + SUBLANES, D_RNN), F32), pltpu.VMEM((1, D_RNN), F32), big, big, big],
        compiler_params=_params("parallel", "arbitrary"),
    )(xr, gr, conv_state, h0, conv_w, conv_b, w_a_b, b_a, w_i_b, b_i, rg_lambda)


def _first_lane_of_max(vals, valid, lane):
    masked = jnp.where(valid, vals, -jnp.inf)
    mx = jnp.max(masked, axis=-1, keepdims=True)
    idx = jnp.min(jnp.where(valid & (masked == mx), lane, ROUTE_LANES), axis=-1, keepdims=True)
    return mx, idx


def _route(logits):
    lane = lax.broadcasted_iota(jnp.int32, logits.shape, 1)
    is_group = lane < N_GROUPS
    gmax, gsel = _first_lane_of_max(logits, is_group, lane)
    gsum = jnp.sum(jnp.where(is_group, jnp.exp(logits - gmax), 0.0), axis=-1, keepdims=True)
    g_w = 1.0 / gsum
    expert = lane - EXPERT_LANE0
    in_group = (expert >= 0) & (expert < N_EXPERTS) & ((expert // EXPERTS_PER_GROUP) == gsel)
    v1, i1 = _first_lane_of_max(logits, in_group, lane)
    v2, i2 = _first_lane_of_max(logits, in_group & (lane != i1), lane)
    e2 = jnp.exp(v2 - v1)
    den = 1.0 + e2
    return jnp.where(lane == i1, (1.0 / den) * g_w, 0.0) + jnp.where(lane == i2, (e2 / den) * g_w, 0.0)


def _merge_kernel(at_ref, y_ref, gt_ref, x_ref, wa_ref, wr_ref, wo_ref, bg_ref, g2_ref, wc_ref, bc_ref,
                  x1_ref, xn_ref, cm_ref, *, parts):
    rows_per_part = x_ref.shape[0] // parts
    for part in range(parts):
        rows = slice(part * rows_per_part, (part + 1) * rows_per_part)
        pa = jnp.dot(at_ref[rows, :], wa_ref[...], preferred_element_type=F32)
        pr = jnp.dot(y_ref[rows, :], wr_ref[...], preferred_element_type=F32)
        g = _sigmoid(gt_ref[rows, :] + bg_ref[...])
        merged = g[:, :D_MODEL] * pa + g[:, D_MODEL:] * pr
        x1 = x_ref[rows, :] + jnp.dot(merged.astype(BF16), wo_ref[...], preferred_element_type=F32)
        x1_ref[rows, :] = x1
        xn = _rms(x1, g2_ref[...]).astype(BF16)
        xn_ref[rows, :] = xn
        cm_ref[rows, :] = _route(jnp.dot(xn, wc_ref[...], preferred_element_type=F32) + bc_ref[...])


def _merge(attn, y_rnn, gates, x2d, wa_b, wr_b, wo_b, b_gate, g2, w_cat_b, b_cat, tm):
    n = x2d.shape[0]
    row = lambda i: (i, 0)
    const = lambda i: (0, 0)
    sq = pl.BlockSpec((D_MODEL, D_MODEL), const)
    return pl.pallas_call(
        functools.partial(_merge_kernel, parts=max(1, tm // MERGE_PART_ROWS)),
        grid=(n // tm,),
        in_specs=[pl.BlockSpec((tm, D_MODEL), row), pl.BlockSpec((tm, D_MODEL), row),
                  pl.BlockSpec((tm, 2 * D_MODEL), row), pl.BlockSpec((tm, D_MODEL), row),
                  sq, sq, sq, pl.BlockSpec((1, 2 * D_MODEL), const), pl.BlockSpec((1, D_MODEL), const),
                  pl.BlockSpec((D_MODEL, ROUTE_LANES), const), pl.BlockSpec((1, ROUTE_LANES), const)],
        out_specs=[pl.BlockSpec((tm, D_MODEL), row), pl.BlockSpec((tm, D_MODEL), row),
                   pl.BlockSpec((tm, ROUTE_LANES), row)],
        out_shape=[jax.ShapeDtypeStruct((n, D_MODEL), F32), jax.ShapeDtypeStruct((n, D_MODEL), BF16),
                   jax.ShapeDtypeStruct((n, ROUTE_LANES), F32)],
        compiler_params=_params("parallel"),
    )(attn, y_rnn, gates, x2d, wa_b, wr_b, wo_b, b_gate, g2, w_cat_b, b_cat)


def _route_tables(comb, tile):
    n = comb.shape[0]
    nt = n // tile
    c = comb[:, EXPERT_LANE0:EXPERT_LANE0 + N_EXPERTS]
    sel = jnp.any(c.reshape(nt, tile, N_GROUPS, EXPERTS_PER_GROUP) != 0.0, axis=-1)
    csum = jnp.cumsum(sel.astype(jnp.int32), axis=1)
    rank = jnp.where(sel, csum - 1, -1).astype(F32)
    counts = csum[:, -1, :].reshape(nt * N_GROUPS)
    rank_t = rank.transpose(0, 2, 1).reshape(nt * N_GROUPS, 1, tile)
    rank = jnp.pad(rank.reshape(n, N_GROUPS), ((0, 0), (0, ROUTE_LANES - N_GROUPS)), constant_values=-1.0)
    hi = c.astype(BF16)
    rest = c - hi.astype(F32)
    mid = rest.astype(BF16)
    lo = (rest - mid.astype(F32)).astype(BF16)
    c3 = jnp.pad(jnp.concatenate([hi, mid, lo], axis=1), ((0, 0), (0, ROUTE_LANES - 3 * N_EXPERTS)))
    return counts, rank, rank_t, c3


def _moe_kernel(cnt_ref, x_ref, c3_ref, rk_ref, rkt_ref, x1_ref, gf_ref, *refs, tile):
    w1_refs, w3_refs, w2_refs = (refs[k * EXPERTS_PER_STEP:(k + 1) * EXPERTS_PER_STEP] for k in range(3))
    y_ref, xg, cg, yg = refs[3 * EXPERTS_PER_STEP:]
    t = pl.program_id(0)
    step = pl.program_id(1)
    steps_per_group = EXPERTS_PER_GROUP // EXPERTS_PER_STEP
    g = step // steps_per_group
    member = step % steps_per_group
    count = cnt_ref[t * N_GROUPS + g]
    nblk = (count + MOE_ROWS - 1) // MOE_ROWS

    @pl.when(step == 0)
    def _():
        y_ref[...] = jnp.zeros(y_ref.shape, F32)

    @pl.when(member == 0)
    def _():
        rank_row = rkt_ref[0]

        def gather(b, carry):
            slot = lax.broadcasted_iota(jnp.int32, (MOE_ROWS, tile), 0).astype(F32) + (b * MOE_ROWS).astype(F32)
            onehot = jnp.where(rank_row == slot, 1.0, 0.0).astype(BF16)
            xg[b] = jnp.dot(onehot, x_ref[...], preferred_element_type=F32).astype(BF16)
            cg[b] = jnp.dot(onehot, c3_ref[...], preferred_element_type=F32)
            yg[b] = jnp.zeros((MOE_ROWS, D_MODEL), F32)
            return carry

        lax.fori_loop(0, nblk, gather, 0)

    lane_r = lax.broadcasted_iota(jnp.int32, (MOE_ROWS, ROUTE_LANES), 1)

    def experts(b, carry):
        xc = xg[b]
        out = yg[b]
        for j in range(EXPERTS_PER_STEP):
            mine = (lane_r % N_EXPERTS == step * EXPERTS_PER_STEP + j) & (lane_r < 3 * N_EXPERTS)
            cw = jnp.sum(jnp.where(mine, cg[b], 0.0), axis=1, keepdims=True)
            a = jnp.dot(xc, w1_refs[j][0], preferred_element_type=F32)
            h = (a * _sigmoid(a)) * jnp.dot(xc, w3_refs[j][0], preferred_element_type=F32)
            out = out + cw * jnp.dot(h.astype(BF16), w2_refs[j][0], preferred_element_type=F32)
        yg[b] = out
        return carry

    lax.fori_loop(0, nblk, experts, 0)

    @pl.when(member == steps_per_group - 1)
    def _():
        lane = lax.broadcasted_iota(jnp.int32, (tile, ROUTE_LANES), 1)
        rank_col = jnp.sum(jnp.where(lane == g, rk_ref[...], 0.0), axis=1, keepdims=True)

        def scatter(b, carry):
            slot_t = lax.broadcasted_iota(jnp.int32, (tile, MOE_ROWS), 1).astype(F32) + (b * MOE_ROWS).astype(F32)
            onehot_t = jnp.where(rank_col == slot_t, 1.0, 0.0).astype(BF16)
            y_ref[...] += jnp.dot(onehot_t, yg[b].astype(BF16), preferred_element_type=F32)
            return carry

        lax.fori_loop(0, nblk, scatter, 0)

    @pl.when(step == pl.num_programs(1) - 1)
    def _():
        y_ref[...] = _rms(x1_ref[...] + y_ref[...], gf_ref[...])


def _moe(xn2, comb, x1, gf, w1_b, w3_b, w2_b, tile):
    n = xn2.shape[0]
    counts, rank, rank_t, c3 = _route_tables(comb, tile)
    max_blocks = -(-tile // MOE_ROWS)
    row = lambda t, s, cnt: (t, 0)
    expert_specs = lambda shape: [pl.BlockSpec((1,) + shape, lambda t, s, cnt, j=j: (s * EXPERTS_PER_STEP + j, 0, 0))
                                  for j in range(EXPERTS_PER_STEP)]
    steps_per_group = EXPERTS_PER_GROUP // EXPERTS_PER_STEP
    return pl.pallas_call(
        functools.partial(_moe_kernel, tile=tile),
        grid_spec=pltpu.PrefetchScalarGridSpec(
            num_scalar_prefetch=1,
            grid=(n // tile, N_EXPERTS // EXPERTS_PER_STEP),
            in_specs=[pl.BlockSpec((tile, D_MODEL), row), pl.BlockSpec((tile, ROUTE_LANES), row),
                      pl.BlockSpec((tile, ROUTE_LANES), row),
                      pl.BlockSpec((1, 1, tile), lambda t, s, cnt: (t * N_GROUPS + s // steps_per_group, 0, 0)),
                      pl.BlockSpec((tile, D_MODEL), row), pl.BlockSpec((1, D_MODEL), lambda t, s, cnt: (0, 0))]
            + expert_specs((D_MODEL, D_EXPERT)) + expert_specs((D_MODEL, D_EXPERT))
            + expert_specs((D_EXPERT, D_MODEL)),
            out_specs=pl.BlockSpec((tile, D_MODEL), row),
            scratch_shapes=[pltpu.VMEM((max_blocks, MOE_ROWS, D_MODEL), BF16),
                            pltpu.VMEM((max_blocks, MOE_ROWS, ROUTE_LANES), F32),
                            pltpu.VMEM((max_blocks, MOE_ROWS, D_MODEL), F32)]),
        out_shape=jax.ShapeDtypeStruct((n, D_MODEL), F32),
        compiler_params=_params("parallel", "arbitrary"),
    )(counts, xn2, c3, rank, rank_t, x1, gf, *([w1_b] * EXPERTS_PER_STEP + [w3_b] * EXPERTS_PER_STEP
                                                + [w2_b] * EXPERTS_PER_STEP))


def _stream(x, past, w, *, tm_in, tm_merge, tm_moe, tq, tt):
    batch, t, _ = x.shape
    n = batch * t
    x2d = x.reshape(n, D_MODEL)
    q, k, v, kb, vb, xr, gr, gates = _in_proj(x2d, w["g1"], w["w_in"], tm_in,
                                              seq=t if past is None else None)
    if past is None:
        attn = _attn_prompt(q, kb, vb, w["lams"], w["subln_g"], batch, t, tq)
        k = k.reshape(batch, N_HEADS, 2, QK_HEAD_DIM, t).transpose(0, 4, 1, 2, 3)
        conv_state = jnp.zeros((batch, CONV_WIDTH - 1, D_RNN), F32)
        h0 = jnp.zeros((batch, 1, D_RNN), F32)
    else:
        cache_k, cache_v, conv_state, h0 = past
        attn = _attn_sample(q, kb, vb, cache_k, cache_v, w["lams"], w["subln_g"], batch, t)
    y_rnn, new_conv, h_last = _rglru(xr, gr, conv_state, h0, w["conv_w"], w["conv_b"], w["w_a"], w["b_a"],
                                     w["w_i"], w["b_i"], w["rg_lambda"], batch, t, tt)
    x1, xn2, comb = _merge(attn, y_rnn, gates, x2d, w["w_attn"], w["w_rnn"], w["w_out"], w["b_gate"],
                           w["g2"], w["w_cat"], w["b_cat"], tm_merge)
    y = _moe(xn2, comb, x1, w["gf"], w["w1"], w["w3"], w["w2"], tm_moe)
    return (y.reshape(batch, t, D_MODEL),
            k.reshape(1, batch, t, N_HEADS, 2, QK_HEAD_DIM),
            v.reshape(1, batch, t, N_HEADS, V_HEAD_DIM),
            new_conv.reshape(1, batch, CONV_WIDTH - 1, D_RNN),
            h_last.reshape(1, batch, D_RNN))


def kernel(x_prompt, x_sample, cache_k, cache_v, state_conv, state_rnn, norm1_g, w_in, lambda_q1, lambda_k1, lambda_q2, lambda_k2, subln_g, w_attn_proj, conv_w, conv_b, w_rg_a, b_rg_a, w_rg_i, b_rg_i, rg_lambda, w_rnn_proj, b_gate, w_out, norm2_g, w_group, b_group, w_router, b_router, w1, w3, w2, final_norm_g):
    assert norm1_g.shape[0] == 1, "single-layer model"
    dec_batch, past_len = cache_k.shape[1], cache_k.shape[2]
    pad = ROUTE_LANES - N_GROUPS - N_EXPERTS
    w = {
        "g1": norm1_g, "g2": norm2_g, "gf": final_norm_g.reshape(1, D_MODEL),
        "w_in": w_in[0].astype(BF16),
        "lams": (lambda_q1, lambda_k1, lambda_q2, lambda_k2),
        "subln_g": subln_g,
        "w_attn": w_attn_proj[0].astype(BF16), "w_rnn": w_rnn_proj[0].astype(BF16),
        "w_out": w_out[0].astype(BF16),
        "conv_w": conv_w[0], "conv_b": conv_b,
        "w_a": w_rg_a[0].astype(BF16), "b_a": b_rg_a, "w_i": w_rg_i[0].astype(BF16), "b_i": b_rg_i,
        "rg_lambda": rg_lambda, "b_gate": b_gate,
        "w_cat": jnp.pad(jnp.concatenate([w_group[0], w_router[0]], axis=1), ((0, 0), (0, pad))).astype(BF16),
        "b_cat": jnp.pad(jnp.concatenate([b_group[0], b_router[0]]), (0, pad)).reshape(1, ROUTE_LANES),
        "w1": w1[0].astype(BF16), "w3": w3[0].astype(BF16), "w2": w2[0].astype(BF16),
    }
    yp, kp, vp, cp, hp = _stream(x_prompt, None, w, tm_in=256, tm_merge=512, tm_moe=1024, tq=512, tt=256)
    past = (cache_k[0].transpose(0, 2, 3, 4, 1).reshape(dec_batch, D_MODEL, past_len),
            cache_v[0].reshape(dec_batch, past_len * N_HEADS, V_HEAD_DIM),
            state_conv[0], state_rnn[0].reshape(dec_batch, 1, D_RNN))
    dec_t = x_sample.shape[1]
    ys, ks, vs, cs, hs = _stream(x_sample, past, w, tm_in=256, tm_merge=256, tm_moe=256, tq=dec_t, tt=dec_t)
    return (yp, ys, kp, vp, cp, hp, ks, vs, cs, hs)
```

```python
import functools
import math

import jax
import jax.numpy as jnp
from jax import lax
from jax.experimental import pallas as pl
from jax.experimental.pallas import tpu as pltpu

F32 = jnp.float32
BF16 = jnp.bfloat16

D_MODEL = 1024
CHUNK = 64
N_HEADS = 8
QK_HEAD_DIM = 64
V_HEAD_DIM = 128
HEAD_COLS = 2 * QK_HEAD_DIM
D_RNN = 1024
N_RNN_BLOCKS = 8
RNN_BLOCK = D_RNN // N_RNN_BLOCKS
CONV_WIDTH = 4
RG_C = 8.0
N_GROUPS = 4
EXPERTS_PER_GROUP = 4
N_EXPERTS = N_GROUPS * EXPERTS_PER_GROUP
D_EXPERT = 512
EPS = 1e-6
LAMBDA_INIT = 0.8 - 0.6 * math.exp(-0.3 * 0)
Q_SCALE = QK_HEAD_DIM ** -0.5 * math.log2(math.e)
N_SEG = 7
IN_WIDTH = N_SEG * D_MODEL
ROUTE_LANES = 128
EXPERTS_PER_STEP = 4
MOE_ROWS = 288
EXPERT_LANE0 = N_GROUPS
SUBLANES = 8
LANES = 128
HEADS_PER_STEP = 8
MERGE_PART_ROWS = 256
VMEM_LIMIT = 56 * 1024 * 1024
NEG = float(jnp.finfo(jnp.float32).min)


def _params(*sem):
    return pltpu.CompilerParams(dimension_semantics=sem, vmem_limit_bytes=VMEM_LIMIT)


def _rms(x, g):
    return x * lax.rsqrt(jnp.mean(x * x, axis=-1, keepdims=True) + EPS) * g


def _sigmoid(x):
    return 0.5 * jnp.tanh(0.5 * x) + 0.5


def _in_proj_kernel(x_ref, g_ref, w_ref, q_ref, k_ref, v_ref, kb_ref, vb_ref, xr_ref, gr_ref, gt_ref,
                    *, k_transposed):
    xn = _rms(x_ref[...], g_ref[...]).astype(BF16)

    def seg(j):
        return jnp.dot(xn, w_ref[:, j * D_MODEL:(j + 1) * D_MODEL], preferred_element_type=F32)

    q_ref[...] = (seg(0) * Q_SCALE).astype(BF16)
    if k_transposed:
        kt = seg(1).T
        k_ref[0] = kt
        kb_ref[0, 0] = kt.astype(BF16)
    else:
        k = seg(1)
        k_ref[...] = k
        kb_ref[...] = k.astype(BF16)
    v = seg(2)
    v_ref[...] = v
    vb_ref[...] = v.astype(BF16)
    xr_ref[...] = seg(3)
    gr_ref[...] = seg(4)
    gt_ref[:, :D_MODEL] = seg(5)
    gt_ref[:, D_MODEL:] = seg(6)


def _in_proj(x2d, g1, w_in_b, tm, seq=None):
    n = x2d.shape[0]
    row = lambda i: (i, 0)
    const = lambda i: (0, 0)
    wide = lambda dt: jax.ShapeDtypeStruct((n, D_MODEL), dt)
    wide_spec = pl.BlockSpec((tm, D_MODEL), row)
    if seq is None:
        k_shape, k_spec, kb_shape, kb_spec = wide(F32), wide_spec, wide(BF16), wide_spec
    else:
        nt = seq // tm
        k_shape = jax.ShapeDtypeStruct((n // seq, D_MODEL, seq), F32)
        k_spec = pl.BlockSpec((1, D_MODEL, tm), lambda i: (i // nt, 0, i % nt))
        kb_shape = jax.ShapeDtypeStruct((n // seq, nt, D_MODEL, tm), BF16)
        kb_spec = pl.BlockSpec((1, 1, D_MODEL, tm), lambda i: (i // nt, i % nt, 0, 0))
    return pl.pallas_call(
        functools.partial(_in_proj_kernel, k_transposed=seq is not None),
        grid=(n // tm,),
        in_specs=[wide_spec,
                  pl.BlockSpec((1, D_MODEL), const),
                  pl.BlockSpec((D_MODEL, IN_WIDTH), const, pipeline_mode=pl.Buffered(1))],
        out_specs=[wide_spec, k_spec, wide_spec, kb_spec, wide_spec, wide_spec, wide_spec,
                   pl.BlockSpec((tm, 2 * D_MODEL), row)],
        out_shape=[wide(BF16), k_shape, wide(F32), kb_shape, wide(BF16), wide(F32), wide(F32),
                   jax.ShapeDtypeStruct((n, 2 * D_MODEL), F32)],
        compiler_params=_params("parallel"),
    )(x2d, g1, w_in_b)


def _lam(lq1, lk1, lq2, lk2):
    return (jnp.exp(jnp.sum(lq1[...] * lk1[...], axis=-1, keepdims=True))
            - jnp.exp(jnp.sum(lq2[...] * lk2[...], axis=-1, keepdims=True)) + LAMBDA_INIT)


def _split_q(q):
    lane = lax.broadcasted_iota(jnp.int32, q.shape, 1)
    zero = jnp.zeros_like(q)
    return jnp.where(lane < QK_HEAD_DIM, q, zero), jnp.where(lane >= QK_HEAD_DIM, q, zero)


def _scores(qm, k):
    return lax.dot_general(qm, k, (((1,), (1,)), ((), ())), preferred_element_type=F32)


def _subln(o, g):
    return (_rms(o, g) * (1.0 - LAMBDA_INIT)).astype(BF16)


def _attn_prompt_kernel(lq1, lk1, lq2, lk2, sg_ref, q_ref, k_ref, v_ref, o_ref, m_ref, acc_ref, *, tq, kw):
    i = pl.program_id(2)
    ones = jnp.ones((tq, LANES), BF16)
    qq = []
    for h in range(HEADS_PER_STEP):
        q1, q2 = _split_q(q_ref[:, h * HEAD_COLS:(h + 1) * HEAD_COLS])
        qq.append(jnp.concatenate([q1, q2], axis=0))

    def row_max(chunks):
        part = chunks[0]
        for x in chunks[1:]:
            part = jnp.maximum(part, x)
        return jnp.max(part, axis=1, keepdims=True)

    def split(s):
        return [s[:, c * LANES:(c + 1) * LANES] for c in range(s.shape[1] // LANES)]

    def head_cols(h, width):
        return slice(h * width, (h + 1) * width)

    def diagonal_tile(h):
        row = lax.broadcasted_iota(jnp.int32, (2 * kw, LANES), 0)
        lane = lax.broadcasted_iota(jnp.int32, (2 * kw, LANES), 1)
        qchunk = (row % kw) // CHUNK
        for r in range(tq // kw):
            rows1, rows2 = slice(r * kw, (r + 1) * kw), slice(tq + r * kw, tq + (r + 1) * kw)
            qr = jnp.concatenate([qq[h][rows1], qq[h][rows2]], axis=0)
            chunks = []
            for j in range(r + 1):
                s = split(jnp.dot(qr, k_ref[0, i * (tq // kw) + j, head_cols(h, HEAD_COLS), :],
                                  preferred_element_type=F32))
                if j == r:
                    s = [jnp.where((lane + c * LANES) // CHUNK <= qchunk, x, NEG) for c, x in enumerate(s)]
                chunks += s
            m_new = jnp.broadcast_to(row_max(chunks), (2 * kw, LANES))
            p = jnp.concatenate([jnp.exp2(x - m_new).astype(BF16) for x in chunks], axis=1)
            keys = (r + 1) * kw
            start = pl.multiple_of(i * tq, tq)
            v1 = jnp.concatenate([v_ref[pl.ds(start, keys), head_cols(h, V_HEAD_DIM)],
                                  jnp.ones((keys, LANES), BF16)], axis=1)
            pv = jnp.dot(p, v1, preferred_element_type=F32)
            for rows, part in ((rows1, slice(0, kw)), (rows2, slice(kw, 2 * kw))):
                acc_ref[h, rows, :] = pv[part]
                m_ref[h, rows, :] = m_new[part]

    def tile(kt, h):
        start = pl.multiple_of(kt * tq, tq)
        chunks = []
        for j in range(tq // kw):
            chunks += split(jnp.dot(qq[h], k_ref[0, kt * (tq // kw) + j, head_cols(h, HEAD_COLS), :],
                                    preferred_element_type=F32))
        m_next = jnp.maximum(m_ref[h], row_max(chunks))
        p = jnp.concatenate([jnp.exp2(x - m_next).astype(BF16) for x in chunks], axis=1)
        v1 = jnp.concatenate([v_ref[pl.ds(start, tq), head_cols(h, V_HEAD_DIM)], ones], axis=1)
        alpha = jnp.exp2(m_ref[h] - m_next)
        acc_ref[h] = (jnp.concatenate([alpha, alpha], axis=1) * acc_ref[h]
                      + jnp.dot(p, v1, preferred_element_type=F32))
        m_ref[h] = m_next

    def body(kt, c):
        for h in range(HEADS_PER_STEP):
            tile(kt, h)
        return c

    for h in range(HEADS_PER_STEP):
        diagonal_tile(h)
    lax.fori_loop(0, i, body, 0)
    lam = _lam(lq1, lk1, lq2, lk2)
    for h in range(HEADS_PER_STEP):
        o1 = acc_ref[h, :tq, :V_HEAD_DIM] / acc_ref[h, :tq, V_HEAD_DIM:]
        o2 = acc_ref[h, tq:, :V_HEAD_DIM] / acc_ref[h, tq:, V_HEAD_DIM:]
        o_ref[:, h * V_HEAD_DIM:(h + 1) * V_HEAD_DIM] = _subln(o1 - lam * o2, sg_ref[...])


def _attn_prompt(q, kbt, vb, lams, subln_g, batch, seq, tq):
    n = q.shape[0]
    nq = seq // tq
    nkb, kw = kbt.shape[1], kbt.shape[3]
    small = lambda b, h, i: (0, 0)
    qmap = lambda b, h, i: (b * nq + i, h)
    kvmap = lambda b, h, i: (b, h)
    return pl.pallas_call(
        functools.partial(_attn_prompt_kernel, tq=tq, kw=kw),
        grid=(batch, N_HEADS // HEADS_PER_STEP, nq),
        in_specs=[pl.BlockSpec((1, QK_HEAD_DIM), small)] * 4 + [
            pl.BlockSpec((1, V_HEAD_DIM), small),
            pl.BlockSpec((tq, HEADS_PER_STEP * HEAD_COLS), qmap),
            pl.BlockSpec((1, nkb, HEADS_PER_STEP * HEAD_COLS, kw), lambda b, h, i: (b, 0, h, 0)),
            pl.BlockSpec((seq, HEADS_PER_STEP * V_HEAD_DIM), kvmap)],
        out_specs=pl.BlockSpec((tq, HEADS_PER_STEP * V_HEAD_DIM), qmap),
        out_shape=jax.ShapeDtypeStruct((n, N_HEADS * V_HEAD_DIM), BF16),
        scratch_shapes=[pltpu.VMEM((HEADS_PER_STEP, 2 * tq, LANES), F32),
                        pltpu.VMEM((HEADS_PER_STEP, 2 * tq, 2 * LANES), F32)],
        compiler_params=_params("parallel", "parallel", "arbitrary"),
    )(*lams, subln_g, q, kbt, vb)


def _attn_sample_kernel(lq1, lk1, lq2, lk2, sg_ref, q_ref, kn_ref, vn_ref, ck_ref, cv_ref, o_ref, *, past):
    lam = _lam(lq1, lk1, lq2, lk2)
    for h in range(N_HEADS):
        cols = slice(h * HEAD_COLS, (h + 1) * HEAD_COLS)
        qq = jnp.concatenate(_split_q(q_ref[:, cols]), axis=0)
        kct = ck_ref[0, h * HEAD_COLS:(h + 1) * HEAD_COLS, :].astype(BF16)
        vc = cv_ref[0, pl.ds(h, past, stride=N_HEADS), :].astype(BF16)
        sc = jnp.dot(qq, kct, preferred_element_type=F32)
        sn = _scores(qq, kn_ref[:, cols])
        m = jnp.maximum(jnp.max(sc, axis=-1, keepdims=True), jnp.max(sn, axis=-1, keepdims=True))
        pc = jnp.exp2(sc - m)
        pn = jnp.exp2(sn - m)
        l = jnp.sum(pc, axis=-1, keepdims=True) + jnp.sum(pn, axis=-1, keepdims=True)
        acc = (jnp.dot(pc.astype(BF16), vc, preferred_element_type=F32)
               + jnp.dot(pn.astype(BF16), vn_ref[:, cols], preferred_element_type=F32))
        o = acc / l
        t = q_ref.shape[0]
        o_ref[:, cols] = _subln(o[:t] - lam * o[t:], sg_ref[...])


def _attn_sample(q, kb, vb, cache_kt, cache_v, lams, subln_g, batch, t):
    past = cache_kt.shape[2]
    small = lambda b: (0, 0)
    row = lambda b: (b, 0)
    cache = lambda b: (b, 0, 0)
    return pl.pallas_call(
        functools.partial(_attn_sample_kernel, past=past),
        grid=(batch,),
        in_specs=[pl.BlockSpec((1, QK_HEAD_DIM), small)] * 4 + [
            pl.BlockSpec((1, V_HEAD_DIM), small),
            pl.BlockSpec((t, D_MODEL), row),
            pl.BlockSpec((t, D_MODEL), row),
            pl.BlockSpec((t, D_MODEL), row),
            pl.BlockSpec((1, D_MODEL, past), cache),
            pl.BlockSpec((1, past * N_HEADS, V_HEAD_DIM), cache)],
        out_specs=pl.BlockSpec((t, D_MODEL), row),
        out_shape=jax.ShapeDtypeStruct((batch * t, D_MODEL), BF16),
        compiler_params=_params("parallel"),
    )(*lams, subln_g, q, kb, vb, cache_kt, cache_v)


def _rglru_kernel(xr_ref, gr_ref, cs_ref, h0_ref, cw_ref, cb_ref, wa_ref, ba_ref, wi_ref, bi_ref, lam_ref,
                  y_ref, nc_ref, hl_ref, xpad, hcar, a_s, b_s, h_s, *, tt):
    t = pl.program_id(1)
    pad = SUBLANES

    @pl.when(t == 0)
    def _():
        xpad[0:pad, :] = jnp.zeros((pad, D_RNN), F32)
        xpad[pad - (CONV_WIDTH - 1):pad, :] = cs_ref[0]
        hcar[...] = h0_ref[0]

    xpad[pad:pad + tt, :] = xr_ref[...]
    xc = cb_ref[...]
    for j in range(CONV_WIDTH):
        off = pad - (CONV_WIDTH - 1) + j
        xc = xc + xpad[off:off + tt, :] * cw_ref[j:j + 1, :]
    xcb = xc.astype(BF16)

    z = -lam_ref[...]
    softplus = jnp.maximum(z, 0.0) + jnp.log1p(jnp.exp(-jnp.abs(z)))
    for n in range(N_RNN_BLOCKS):
        cols = slice(n * RNN_BLOCK, (n + 1) * RNN_BLOCK)
        xs = xcb[:, cols]
        r = _sigmoid(jnp.dot(xs, wa_ref[n], preferred_element_type=F32) + ba_ref[:, cols])
        i = _sigmoid(jnp.dot(xs, wi_ref[n], preferred_element_type=F32) + bi_ref[:, cols])
        log_a = -RG_C * r * softplus[:, cols]
        a_s[:, cols] = jnp.exp(log_a)
        th = jnp.tanh(log_a)
        b_s[:, cols] = jnp.sqrt(-2.0 * th / (1.0 - th)) * (i * xc[:, cols])

    h = hcar[...]
    for s in range(tt):
        h = a_s[s:s + 1, :] * h + b_s[s:s + 1, :]
        h_s[s:s + 1, :] = h
    hcar[...] = h
    y_ref[...] = (h_s[...] * jax.nn.gelu(gr_ref[...])).astype(BF16)
    nc_ref[0] = xpad[pad + tt - (CONV_WIDTH - 1):pad + tt, :]
    hl_ref[0] = h
    xpad[0:pad, :] = xpad[tt:tt + pad, :]


def _rglru(xr, gr, conv_state, h0, conv_w, conv_b, w_a_b, b_a, w_i_b, b_i, rg_lambda, batch, t, tt):
    nt = t // tt
    row = lambda b, s: (b * nt + s, 0)
    const2 = lambda b, s: (0, 0)
    const3 = lambda b, s: (0, 0, 0)
    per_b = lambda b, s: (b, 0, 0)
    vec = pl.BlockSpec((1, D_RNN), const2)
    blk = pl.BlockSpec((N_RNN_BLOCKS, RNN_BLOCK, RNN_BLOCK), const3)
    big = pltpu.VMEM((tt, D_RNN), F32)
    return pl.pallas_call(
        functools.partial(_rglru_kernel, tt=tt),
        grid=(batch, nt),
        in_specs=[pl.BlockSpec((tt, D_RNN), row), pl.BlockSpec((tt, D_RNN), row),
                  pl.BlockSpec((1, CONV_WIDTH - 1, D_RNN), per_b), pl.BlockSpec((1, 1, D_RNN), per_b),
                  pl.BlockSpec((CONV_WIDTH, D_RNN), const2), vec, blk, vec, blk, vec, vec],
        out_specs=[pl.BlockSpec((tt, D_RNN), row),
                   pl.BlockSpec((1, CONV_WIDTH - 1, D_RNN), per_b),
                   pl.BlockSpec((1, 1, D_RNN), per_b)],
        out_shape=[jax.ShapeDtypeStruct((batch * t, D_RNN), BF16),
                   jax.ShapeDtypeStruct((batch, CONV_WIDTH - 1, D_RNN), F32),
                   jax.ShapeDtypeStruct((batch, 1, D_RNN), F32)],
        scratch_shapes=[pltpu.VMEM((tt + SUBLANES, D_RNN), F32), pltpu.VMEM((1, D_RNN), F32), big, big, big],
        compiler_params=_params("parallel", "arbitrary"),
    )(xr, gr, conv_state, h0, conv_w, conv_b, w_a_b, b_a, w_i_b, b_i, rg_lambda)


def _first_lane_of_max(vals, valid, lane):
    masked = jnp.where(valid, vals, -jnp.inf)
    mx = jnp.max(masked, axis=-1, keepdims=True)
    idx = jnp.min(jnp.where(valid & (masked == mx), lane, ROUTE_LANES), axis=-1, keepdims=True)
    return mx, idx


def _route(logits):
    lane = lax.broadcasted_iota(jnp.int32, logits.shape, 1)
    is_group = lane < N_GROUPS
    gmax, gsel = _first_lane_of_max(logits, is_group, lane)
    gsum = jnp.sum(jnp.where(is_group, jnp.exp(logits - gmax), 0.0), axis=-1, keepdims=True)
    g_w = 1.0 / gsum
    expert = lane - EXPERT_LANE0
    in_group = (expert >= 0) & (expert < N_EXPERTS) & ((expert // EXPERTS_PER_GROUP) == gsel)
    v1, i1 = _first_lane_of_max(logits, in_group, lane)
    v2, i2 = _first_lane_of_max(logits, in_group & (lane != i1), lane)
    e2 = jnp.exp(v2 - v1)
    den = 1.0 + e2
    return jnp.where(lane == i1, (1.0 / den) * g_w, 0.0) + jnp.where(lane == i2, (e2 / den) * g_w, 0.0)


def _merge_kernel(at_ref, y_ref, gt_ref, x_ref, wa_ref, wr_ref, wo_ref, bg_ref, g2_ref, wc_ref, bc_ref,
                  x1_ref, xn_ref, cm_ref, *, parts):
    rows_per_part = x_ref.shape[0] // parts
    for part in range(parts):
        rows = slice(part * rows_per_part, (part + 1) * rows_per_part)
        pa = jnp.dot(at_ref[rows, :], wa_ref[...], preferred_element_type=F32)
        pr = jnp.dot(y_ref[rows, :], wr_ref[...], preferred_element_type=F32)
        g = _sigmoid(gt_ref[rows, :] + bg_ref[...])
        merged = g[:, :D_MODEL] * pa + g[:, D_MODEL:] * pr
        x1 = x_ref[rows, :] + jnp.dot(merged.astype(BF16), wo_ref[...], preferred_element_type=F32)
        x1_ref[rows, :] = x1
        xn = _rms(x1, g2_ref[...]).astype(BF16)
        xn_ref[rows, :] = xn
        cm_ref[rows, :] = _route(jnp.dot(xn, wc_ref[...], preferred_element_type=F32) + bc_ref[...])


def _merge(attn, y_rnn, gates, x2d, wa_b, wr_b, wo_b, b_gate, g2, w_cat_b, b_cat, tm):
    n = x2d.shape[0]
    row = lambda i: (i, 0)
    const = lambda i: (0, 0)
    sq = pl.BlockSpec((D_MODEL, D_MODEL), const)
    return pl.pallas_call(
        functools.partial(_merge_kernel, parts=max(1, tm // MERGE_PART_ROWS)),
        grid=(n // tm,),
        in_specs=[pl.BlockSpec((tm, D_MODEL), row), pl.BlockSpec((tm, D_MODEL), row),
                  pl.BlockSpec((tm, 2 * D_MODEL), row), pl.BlockSpec((tm, D_MODEL), row),
                  sq, sq, sq, pl.BlockSpec((1, 2 * D_MODEL), const), pl.BlockSpec((1, D_MODEL), const),
                  pl.BlockSpec((D_MODEL, ROUTE_LANES), const), pl.BlockSpec((1, ROUTE_LANES), const)],
        out_specs=[pl.BlockSpec((tm, D_MODEL), row), pl.BlockSpec((tm, D_MODEL), row),
                   pl.BlockSpec((tm, ROUTE_LANES), row)],
        out_shape=[jax.ShapeDtypeStruct((n, D_MODEL), F32), jax.ShapeDtypeStruct((n, D_MODEL), BF16),
                   jax.ShapeDtypeStruct((n, ROUTE_LANES), F32)],
        compiler_params=_params("parallel"),
    )(attn, y_rnn, gates, x2d, wa_b, wr_b, wo_b, b_gate, g2, w_cat_b, b_cat)


def _route_tables(comb, tile):
    n = comb.shape[0]
    nt = n // tile
    c = comb[:, EXPERT_LANE0:EXPERT_LANE0 + N_EXPERTS]
    sel = jnp.any(c.reshape(nt, tile, N_GROUPS, EXPERTS_PER_GROUP) != 0.0, axis=-1)
    csum = jnp.cumsum(sel.astype(jnp.int32), axis=1)
    rank = jnp.where(sel, csum - 1, -1).astype(F32)
    counts = csum[:, -1, :].reshape(nt * N_GROUPS)
    rank_t = rank.transpose(0, 2, 1).reshape(nt * N_GROUPS, 1, tile)
    rank = jnp.pad(rank.reshape(n, N_GROUPS), ((0, 0), (0, ROUTE_LANES - N_GROUPS)), constant_values=-1.0)
    hi = c.astype(BF16)
    rest = c - hi.astype(F32)
    mid = rest.astype(BF16)
    lo = (rest - mid.astype(F32)).astype(BF16)
    c3 = jnp.pad(jnp.concatenate([hi, mid, lo], axis=1), ((0, 0), (0, ROUTE_LANES - 3 * N_EXPERTS)))
    return counts, rank, rank_t, c3


def _moe_kernel(cnt_ref, x_ref, c3_ref, rk_ref, rkt_ref, x1_ref, gf_ref, *refs, tile):
    w1_refs, w3_refs, w2_refs = (refs[k * EXPERTS_PER_STEP:(k + 1) * EXPERTS_PER_STEP] for k in range(3))
    y_ref, xg, cg, yg = refs[3 * EXPERTS_PER_STEP:]
    t = pl.program_id(0)
    step = pl.program_id(1)
    steps_per_group = EXPERTS_PER_GROUP // EXPERTS_PER_STEP
    g = step // steps_per_group
    member = step % steps_per_group
    count = cnt_ref[t * N_GROUPS + g]
    nblk = (count + MOE_ROWS - 1) // MOE_ROWS

    @pl.when(step == 0)
    def _():
        y_ref[...] = jnp.zeros(y_ref.shape, F32)

    @pl.when(member == 0)
    def _():
        rank_row = rkt_ref[0]

        def gather(b, carry):
            slot = lax.broadcasted_iota(jnp.int32, (MOE_ROWS, tile), 0).astype(F32) + (b * MOE_ROWS).astype(F32)
            onehot = jnp.where(rank_row == slot, 1.0, 0.0).astype(BF16)
            xg[b] = jnp.dot(onehot, x_ref[...], preferred_element_type=F32).astype(BF16)
            cg[b] = jnp.dot(onehot, c3_ref[...], preferred_element_type=F32)
            yg[b] = jnp.zeros((MOE_ROWS, D_MODEL), F32)
            return carry

        lax.fori_loop(0, nblk, gather, 0)

    lane_r = lax.broadcasted_iota(jnp.int32, (MOE_ROWS, ROUTE_LANES), 1)

    def experts(b, carry):
        xc = xg[b]
        out = yg[b]
        for j in range(EXPERTS_PER_STEP):
            mine = (lane_r % N_EXPERTS == step * EXPERTS_PER_STEP + j) & (lane_r < 3 * N_EXPERTS)
            cw = jnp.sum(jnp.where(mine, cg[b], 0.0), axis=1, keepdims=True)
            a = jnp.dot(xc, w1_refs[j][0], preferred_element_type=F32)
            h = (a * _sigmoid(a)) * jnp.dot(xc, w3_refs[j][0], preferred_element_type=F32)
            out = out + cw * jnp.dot(h.astype(BF16), w2_refs[j][0], preferred_element_type=F32)
        yg[b] = out
        return carry

    lax.fori_loop(0, nblk, experts, 0)

    @pl.when(member == steps_per_group - 1)
    def _():
        lane = lax.broadcasted_iota(jnp.int32, (tile, ROUTE_LANES), 1)
        rank_col = jnp.sum(jnp.where(lane == g, rk_ref[...], 0.0), axis=1, keepdims=True)

        def scatter(b, carry):
            slot_t = lax.broadcasted_iota(jnp.int32, (tile, MOE_ROWS), 1).astype(F32) + (b * MOE_ROWS).astype(F32)
            onehot_t = jnp.where(rank_col == slot_t, 1.0, 0.0).astype(BF16)
            y_ref[...] += jnp.dot(onehot_t, yg[b].astype(BF16), preferred_element_type=F32)
            return carry

        lax.fori_loop(0, nblk, scatter, 0)

    @pl.when(step == pl.num_programs(1) - 1)
    def _():
        y_ref[...] = _rms(x1_ref[...] + y_ref[...], gf_ref[...])


def _moe(xn2, comb, x1, gf, w1_b, w3_b, w2_b, tile):
    n = xn2.shape[0]
    counts, rank, rank_t, c3 = _route_tables(comb, tile)
    max_blocks = -(-tile // MOE_ROWS)
    row = lambda t, s, cnt: (t, 0)
    expert_specs = lambda shape: [pl.BlockSpec((1,) + shape, lambda t, s, cnt, j=j: (s * EXPERTS_PER_STEP + j, 0, 0))
                                  for j in range(EXPERTS_PER_STEP)]
    steps_per_group = EXPERTS_PER_GROUP // EXPERTS_PER_STEP
    return pl.pallas_call(
        functools.partial(_moe_kernel, tile=tile),
        grid_spec=pltpu.PrefetchScalarGridSpec(
            num_scalar_prefetch=1,
            grid=(n // tile, N_EXPERTS // EXPERTS_PER_STEP),
            in_specs=[pl.BlockSpec((tile, D_MODEL), row), pl.BlockSpec((tile, ROUTE_LANES), row),
                      pl.BlockSpec((tile, ROUTE_LANES), row),
                      pl.BlockSpec((1, 1, tile), lambda t, s, cnt: (t * N_GROUPS + s // steps_per_group, 0, 0)),
                      pl.BlockSpec((tile, D_MODEL), row), pl.BlockSpec((1, D_MODEL), lambda t, s, cnt: (0, 0))]
            + expert_specs((D_MODEL, D_EXPERT)) + expert_specs((D_MODEL, D_EXPERT))
            + expert_specs((D_EXPERT, D_MODEL)),
            out_specs=pl.BlockSpec((tile, D_MODEL), row),
            scratch_shapes=[pltpu.VMEM((max_blocks, MOE_ROWS, D_MODEL), BF16),
                            pltpu.VMEM((max_blocks, MOE_ROWS, ROUTE_LANES), F32),
                            pltpu.VMEM((max_blocks, MOE_ROWS, D_MODEL), F32)]),
        out_shape=jax.ShapeDtypeStruct((n, D_MODEL), F32),
        compiler_params=_params("parallel", "arbitrary"),
    )(counts, xn2, c3, rank, rank_t, x1, gf, *([w1_b] * EXPERTS_PER_STEP + [w3_b] * EXPERTS_PER_STEP
                                                + [w2_b] * EXPERTS_PER_STEP))


def _stream(x, past, w, *, tm_in, tm_merge, tm_moe, tq, tt):
    batch, t, _ = x.shape
    n = batch * t
    x2d = x.reshape(n, D_MODEL)
    q, k, v, kb, vb, xr, gr, gates = _in_proj(x2d, w["g1"], w["w_in"], tm_in,
                                              seq=t if past is None else None)
    if past is None:
        attn = _attn_prompt(q, kb, vb, w["lams"], w["subln_g"], batch, t, tq)
        k = k.reshape(batch, N_HEADS, 2, QK_HEAD_DIM, t).transpose(0, 4, 1, 2, 3)
        conv_state = jnp.zeros((batch, CONV_WIDTH - 1, D_RNN), F32)
        h0 = jnp.zeros((batch, 1, D_RNN), F32)
    else:
        cache_k, cache_v, conv_state, h0 = past
        attn = _attn_sample(q, kb, vb, cache_k, cache_v, w["lams"], w["subln_g"], batch, t)
    y_rnn, new_conv, h_last = _rglru(xr, gr, conv_state, h0, w["conv_w"], w["conv_b"], w["w_a"], w["b_a"],
                                     w["w_i"], w["b_i"], w["rg_lambda"], batch, t, tt)
    x1, xn2, comb = _merge(attn, y_rnn, gates, x2d, w["w_attn"], w["w_rnn"], w["w_out"], w["b_gate"],
                           w["g2"], w["w_cat"], w["b_cat"], tm_merge)
    y = _moe(xn2, comb, x1, w["gf"], w["w1"], w["w3"], w["w2"], tm_moe)
    return (y.reshape(batch, t, D_MODEL),
            k.reshape(1, batch, t, N_HEADS, 2, QK_HEAD_DIM),
            v.reshape(1, batch, t, N_HEADS, V_HEAD_DIM),
            new_conv.reshape(1, batch, CONV_WIDTH - 1, D_RNN),
            h_last.reshape(1, batch, D_RNN))


def kernel(x_prompt, x_sample, cache_k, cache_v, state_conv, state_rnn, norm1_g, w_in, lambda_q1, lambda_k1, lambda_q2, lambda_k2, subln_g, w_attn_proj, conv_w, conv_b, w_rg_a, b_rg_a, w_rg_i, b_rg_i, rg_lambda, w_rnn_proj, b_gate, w_out, norm2_g, w_group, b_group, w_router, b_router, w1, w3, w2, final_norm_g):
    assert norm1_g.shape[0] == 1, "single-layer model"
    dec_batch, past_len = cache_k.shape[1], cache_k.shape[2]
    pad = ROUTE_LANES - N_GROUPS - N_EXPERTS
    w = {
        "g1": norm1_g, "g2": norm2_g, "gf": final_norm_g.reshape(1, D_MODEL),
        "w_in": w_in[0].astype(BF16),
        "lams": (lambda_q1, lambda_k1, lambda_q2, lambda_k2),
        "subln_g": subln_g,
        "w_attn": w_attn_proj[0].astype(BF16), "w_rnn": w_rnn_proj[0].astype(BF16),
        "w_out": w_out[0].astype(BF16),
        "conv_w": conv_w[0], "conv_b": conv_b,
        "w_a": w_rg_a[0].astype(BF16), "b_a": b_rg_a, "w_i": w_rg_i[0].astype(BF16), "b_i": b_rg_i,
        "rg_lambda": rg_lambda, "b_gate": b_gate,
        "w_cat": jnp.pad(jnp.concatenate([w_group[0], w_router[0]], axis=1), ((0, 0), (0, pad))).astype(BF16),
        "b_cat": jnp.pad(jnp.concatenate([b_group[0], b_router[0]]), (0, pad)).reshape(1, ROUTE_LANES),
        "w1": w1[0].astype(BF16), "w3": w3[0].astype(BF16), "w2": w2[0].astype(BF16),
    }
    yp, kp, vp, cp, hp = _stream(x_prompt, None, w, tm_in=256, tm_merge=512, tm_moe=1024, tq=512, tt=256)
    past = (cache_k[0].transpose(0, 2, 3, 4, 1).reshape(dec_batch, D_MODEL, past_len),
            cache_v[0].reshape(dec_batch, past_len * N_HEADS, V_HEAD_DIM),
            state_conv[0], state_rnn[0].reshape(dec_batch, 1, D_RNN))
    dec_t = x_sample.shape[1]
    ys, ks, vs, cs, hs = _stream(x_sample, past, w, tm_in=256, tm_merge=256, tm_moe=256, tq=dec_t, tt=dec_t)
    return (yp, ys, kp, vp, cp, hp, ks, vs, cs, hs)
```

```python
import functools
import math

import jax
import jax.numpy as jnp
from jax import lax
from jax.experimental import pallas as pl
from jax.experimental.pallas import tpu as pltpu

F32 = jnp.float32
BF16 = jnp.bfloat16

D_MODEL = 1024
CHUNK = 64
N_HEADS = 8
QK_HEAD_DIM = 64
V_HEAD_DIM = 128
HEAD_COLS = 2 * QK_HEAD_DIM
D_RNN = 1024
N_RNN_BLOCKS = 8
RNN_BLOCK = D_RNN // N_RNN_BLOCKS
CONV_WIDTH = 4
RG_C = 8.0
N_GROUPS = 4
EXPERTS_PER_GROUP = 4
N_EXPERTS = N_GROUPS * EXPERTS_PER_GROUP
D_EXPERT = 512
EPS = 1e-6
LAMBDA_INIT = 0.8 - 0.6 * math.exp(-0.3 * 0)
Q_SCALE = QK_HEAD_DIM ** -0.5 * math.log2(math.e)
N_SEG = 7
IN_WIDTH = N_SEG * D_MODEL
ROUTE_LANES = 128
EXPERTS_PER_STEP = 4
MOE_ROWS = 288
EXPERT_LANE0 = N_GROUPS
SUBLANES = 8
LANES = 128
HEADS_PER_STEP = 8
MERGE_PART_ROWS = 256
VMEM_LIMIT = 56 * 1024 * 1024
NEG = float(jnp.finfo(jnp.float32).min)


def _params(*sem):
    return pltpu.CompilerParams(dimension_semantics=sem, vmem_limit_bytes=VMEM_LIMIT)


def _rms(x, g):
    return x * lax.rsqrt(jnp.mean(x * x, axis=-1, keepdims=True) + EPS) * g


def _sigmoid(x):
    return 0.5 * jnp.tanh(0.5 * x) + 0.5


def _in_proj_kernel(x_ref, g_ref, w_ref, q_ref, k_ref, v_ref, kb_ref, vb_ref, xr_ref, gr_ref, gt_ref,
                    *, k_transposed):
    xn = _rms(x_ref[...], g_ref[...]).astype(BF16)

    def seg(j):
        return jnp.dot(xn, w_ref[:, j * D_MODEL:(j + 1) * D_MODEL], preferred_element_type=F32)

    q_ref[...] = (seg(0) * Q_SCALE).astype(BF16)
    if k_transposed:
        kt = seg(1).T
        k_ref[0] = kt
        kb_ref[0, 0] = kt.astype(BF16)
    else:
        k = seg(1)
        k_ref[...] = k
        kb_ref[...] = k.astype(BF16)
    v = seg(2)
    v_ref[...] = v
    vb_ref[...] = v.astype(BF16)
    xr_ref[...] = seg(3)
    gr_ref[...] = seg(4)
    gt_ref[:, :D_MODEL] = seg(5)
    gt_ref[:, D_MODEL:] = seg(6)


def _in_proj(x2d, g1, w_in_b, tm, seq=None):
    n = x2d.shape[0]
    row = lambda i: (i, 0)
    const = lambda i: (0, 0)
    wide = lambda dt: jax.ShapeDtypeStruct((n, D_MODEL), dt)
    wide_spec = pl.BlockSpec((tm, D_MODEL), row)
    if seq is None:
        k_shape, k_spec, kb_shape, kb_spec = wide(F32), wide_spec, wide(BF16), wide_spec
    else:
        nt = seq // tm
        k_shape = jax.ShapeDtypeStruct((n // seq, D_MODEL, seq), F32)
        k_spec = pl.BlockSpec((1, D_MODEL, tm), lambda i: (i // nt, 0, i % nt))
        kb_shape = jax.ShapeDtypeStruct((n // seq, nt, D_MODEL, tm), BF16)
        kb_spec = pl.BlockSpec((1, 1, D_MODEL, tm), lambda i: (i // nt, i % nt, 0, 0))
    return pl.pallas_call(
        functools.partial(_in_proj_kernel, k_transposed=seq is not None),
        grid=(n // tm,),
        in_specs=[wide_spec,
                  pl.BlockSpec((1, D_MODEL), const),
                  pl.BlockSpec((D_MODEL, IN_WIDTH), const, pipeline_mode=pl.Buffered(1))],
        out_specs=[wide_spec, k_spec, wide_spec, kb_spec, wide_spec, wide_spec, wide_spec,
                   pl.BlockSpec((tm, 2 * D_MODEL), row)],
        out_shape=[wide(BF16), k_shape, wide(F32), kb_shape, wide(BF16), wide(F32), wide(F32),
                   jax.ShapeDtypeStruct((n, 2 * D_MODEL), F32)],
        compiler_params=_params("parallel"),
    )(x2d, g1, w_in_b)


def _lam(lq1, lk1, lq2, lk2):
    return (jnp.exp(jnp.sum(lq1[...] * lk1[...], axis=-1, keepdims=True))
            - jnp.exp(jnp.sum(lq2[...] * lk2[...], axis=-1, keepdims=True)) + LAMBDA_INIT)


def _split_q(q):
    lane = lax.broadcasted_iota(jnp.int32, q.shape, 1)
    zero = jnp.zeros_like(q)
    return jnp.where(lane < QK_HEAD_DIM, q, zero), jnp.where(lane >= QK_HEAD_DIM, q, zero)


def _scores(qm, k):
    return lax.dot_general(qm, k, (((1,), (1,)), ((), ())), preferred_element_type=F32)


def _subln(o, g):
    return (_rms(o, g) * (1.0 - LAMBDA_INIT)).astype(BF16)


def _attn_prompt_kernel(lq1, lk1, lq2, lk2, sg_ref, q_ref, k_ref, v_ref, o_ref, m_ref, acc_ref, *, tq, kw):
    i = pl.program_id(2)
    ones = jnp.ones((tq, LANES), BF16)
    qq = []
    for h in range(HEADS_PER_STEP):
        q1, q2 = _split_q(q_ref[:, h * HEAD_COLS:(h + 1) * HEAD_COLS])
        qq.append(jnp.concatenate([q1, q2], axis=0))

    def row_max(chunks):
        part = chunks[0]
        for x in chunks[1:]:
            part = jnp.maximum(part, x)
        return jnp.max(part, axis=1, keepdims=True)

    def split(s):
        return [s[:, c * LANES:(c + 1) * LANES] for c in range(s.shape[1] // LANES)]

    def head_cols(h, width):
        return slice(h * width, (h + 1) * width)

    def diagonal_tile(h):
        row = lax.broadcasted_iota(jnp.int32, (2 * kw, LANES), 0)
        lane = lax.broadcasted_iota(jnp.int32, (2 * kw, LANES), 1)
        qchunk = (row % kw) // CHUNK
        for r in range(tq // kw):
            rows1, rows2 = slice(r * kw, (r + 1) * kw), slice(tq + r * kw, tq + (r + 1) * kw)
            qr = jnp.concatenate([qq[h][rows1], qq[h][rows2]], axis=0)
            chunks = []
            for j in range(r + 1):
                s = split(jnp.dot(qr, k_ref[0, i * (tq // kw) + j, head_cols(h, HEAD_COLS), :],
                                  preferred_element_type=F32))
                if j == r:
                    s = [jnp.where((lane + c * LANES) // CHUNK <= qchunk, x, NEG) for c, x in enumerate(s)]
                chunks += s
            m_new = jnp.broadcast_to(row_max(chunks), (2 * kw, LANES))
            p = jnp.concatenate([jnp.exp2(x - m_new).astype(BF16) for x in chunks], axis=1)
            keys = (r + 1) * kw
            start = pl.multiple_of(i * tq, tq)
            v1 = jnp.concatenate([v_ref[pl.ds(start, keys), head_cols(h, V_HEAD_DIM)],
                                  jnp.ones((keys, LANES), BF16)], axis=1)
            pv = jnp.dot(p, v1, preferred_element_type=F32)
            for rows, part in ((rows1, slice(0, kw)), (rows2, slice(kw, 2 * kw))):
                acc_ref[h, rows, :] = pv[part]
                m_ref[h, rows, :] = m_new[part]

    def tile(kt, h):
        start = pl.multiple_of(kt * tq, tq)
        chunks = []
        for j in range(tq // kw):
            chunks += split(jnp.dot(qq[h], k_ref[0, kt * (tq // kw) + j, head_cols(h, HEAD_COLS), :],
                                    preferred_element_type=F32))
        m_next = jnp.maximum(m_ref[h], row_max(chunks))
        p = jnp.concatenate([jnp.exp2(x - m_next).astype(BF16) for x in chunks], axis=1)
        v1 = jnp.concatenate([v_ref[pl.ds(start, tq), head_cols(h, V_HEAD_DIM)], ones], axis=1)
        alpha = jnp.exp2(m_ref[h] - m_next)
        acc_ref[h] = (jnp.concatenate([alpha, alpha], axis=1) * acc_ref[h]
                      + jnp.dot(p, v1, preferred_element_type=F32))
        m_ref[h] = m_next

    def body(kt, c):
        for h in range(HEADS_PER_STEP):
            tile(kt, h)
        return c

    for h in range(HEADS_PER_STEP):
        diagonal_tile(h)
    lax.fori_loop(0, i, body, 0)
    lam = _lam(lq1, lk1, lq2, lk2)
    for h in range(HEADS_PER_STEP):
        o1 = acc_ref[h, :tq, :V_HEAD_DIM] / acc_ref[h, :tq, V_HEAD_DIM:]
        o2 = acc_ref[h, tq:, :V_HEAD_DIM] / acc_ref[h, tq:, V_HEAD_DIM:]
        o_ref[:, h * V_HEAD_DIM:(h + 1) * V_HEAD_DIM] = _subln(o1 - lam * o2, sg_ref[...])


def _attn_prompt(q, kbt, vb, lams, subln_g, batch, seq, tq):
    n = q.shape[0]
    nq = seq // tq
    nkb, kw = kbt.shape[1], kbt.shape[3]
    small = lambda b, h, i: (0, 0)
    qmap = lambda b, h, i: (b * nq + i, h)
    kvmap = lambda b, h, i: (b, h)
    return pl.pallas_call(
        functools.partial(_attn_prompt_kernel, tq=tq, kw=kw),
        grid=(batch, N_HEADS // HEADS_PER_STEP, nq),
        in_specs=[pl.BlockSpec((1, QK_HEAD_DIM), small)] * 4 + [
            pl.BlockSpec((1, V_HEAD_DIM), small),
            pl.BlockSpec((tq, HEADS_PER_STEP * HEAD_COLS), qmap),
            pl.BlockSpec((1, nkb, HEADS_PER_STEP * HEAD_COLS, kw), lambda b, h, i: (b, 0, h, 0)),
            pl.BlockSpec((seq, HEADS_PER_STEP * V_HEAD_DIM), kvmap)],
        out_specs=pl.BlockSpec((tq, HEADS_PER_STEP * V_HEAD_DIM), qmap),
        out_shape=jax.ShapeDtypeStruct((n, N_HEADS * V_HEAD_DIM), BF16),
        scratch_shapes=[pltpu.VMEM((HEADS_PER_STEP, 2 * tq, LANES), F32),
                        pltpu.VMEM((HEADS_PER_STEP, 2 * tq, 2 * LANES), F32)],
        compiler_params=_params("parallel", "parallel", "arbitrary"),
    )(*lams, subln_g, q, kbt, vb)


def _attn_sample_kernel(lq1, lk1, lq2, lk2, sg_ref, q_ref, kn_ref, vn_ref, ck_ref, cv_ref, o_ref, *, past):
    lam = _lam(lq1, lk1, lq2, lk2)
    for h in range(N_HEADS):
        cols = slice(h * HEAD_COLS, (h + 1) * HEAD_COLS)
        qq = jnp.concatenate(_split_q(q_ref[:, cols]), axis=0)
        kct = ck_ref[0, h * HEAD_COLS:(h + 1) * HEAD_COLS, :].astype(BF16)
        vc = cv_ref[0, pl.ds(h, past, stride=N_HEADS), :].astype(BF16)
        sc = jnp.dot(qq, kct, preferred_element_type=F32)
        sn = _scores(qq, kn_ref[:, cols])
        m = jnp.maximum(jnp.max(sc, axis=-1, keepdims=True), jnp.max(sn, axis=-1, keepdims=True))
        pc = jnp.exp2(sc - m)
        pn = jnp.exp2(sn - m)
        l = jnp.sum(pc, axis=-1, keepdims=True) + jnp.sum(pn, axis=-1, keepdims=True)
        acc = (jnp.dot(pc.astype(BF16), vc, preferred_element_type=F32)
               + jnp.dot(pn.astype(BF16), vn_ref[:, cols], preferred_element_type=F32))
        o = acc / l
        t = q_ref.shape[0]
        o_ref[:, cols] = _subln(o[:t] - lam * o[t:], sg_ref[...])


def _attn_sample(q, kb, vb, cache_kt, cache_v, lams, subln_g, batch, t):
    past = cache_kt.shape[2]
    small = lambda b: (0, 0)
    row = lambda b: (b, 0)
    cache = lambda b: (b, 0, 0)
    return pl.pallas_call(
        functools.partial(_attn_sample_kernel, past=past),
        grid=(batch,),
        in_specs=[pl.BlockSpec((1, QK_HEAD_DIM), small)] * 4 + [
            pl.BlockSpec((1, V_HEAD_DIM), small),
            pl.BlockSpec((t, D_MODEL), row),
            pl.BlockSpec((t, D_MODEL), row),
            pl.BlockSpec((t, D_MODEL), row),
            pl.BlockSpec((1, D_MODEL, past), cache),
            pl.BlockSpec((1, past * N_HEADS, V_HEAD_DIM), cache)],
        out_specs=pl.BlockSpec((t, D_MODEL), row),
        out_shape=jax.ShapeDtypeStruct((batch * t, D_MODEL), BF16),
        compiler_params=_params("parallel"),
    )(*lams, subln_g, q, kb, vb, cache_kt, cache_v)


def _rglru_kernel(xr_ref, gr_ref, cs_ref, h0_ref, cw_ref, cb_ref, wa_ref, ba_ref, wi_ref, bi_ref, lam_ref,
                  y_ref, nc_ref, hl_ref, xpad, hcar, a_s, b_s, h_s, *, tt):
    t = pl.program_id(1)
    pad = SUBLANES

    @pl.when(t == 0)
    def _():
        xpad[0:pad, :] = jnp.zeros((pad, D_RNN), F32)
        xpad[pad - (CONV_WIDTH - 1):pad, :] = cs_ref[0]
        hcar[...] = h0_ref[0]

    xpad[pad:pad + tt, :] = xr_ref[...]
    xc = cb_ref[...]
    for j in range(CONV_WIDTH):
        off = pad - (CONV_WIDTH - 1) + j
        xc = xc + xpad[off:off + tt, :] * cw_ref[j:j + 1, :]
    xcb = xc.astype(BF16)

    z = -lam_ref[...]
    softplus = jnp.maximum(z, 0.0) + jnp.log1p(jnp.exp(-jnp.abs(z)))
    for n in range(N_RNN_BLOCKS):
        cols = slice(n * RNN_BLOCK, (n + 1) * RNN_BLOCK)
        xs = xcb[:, cols]
        r = _sigmoid(jnp.dot(xs, wa_ref[n], preferred_element_type=F32) + ba_ref[:, cols])
        i = _sigmoid(jnp.dot(xs, wi_ref[n], preferred_element_type=F32) + bi_ref[:, cols])
        log_a = -RG_C * r * softplus[:, cols]
        a_s[:, cols] = jnp.exp(log_a)
        th = jnp.tanh(log_a)
        b_s[:, cols] = jnp.sqrt(-2.0 * th / (1.0 - th)) * (i * xc[:, cols])

    h = hcar[...]
    for s in range(tt):
        h = a_s[s:s + 1, :] * h + b_s[s:s + 1, :]
        h_s[s:s + 1, :] = h
    hcar[...] = h
    y_ref[...] = (h_s[...] * jax.nn.gelu(gr_ref[...])).astype(BF16)
    nc_ref[0] = xpad[pad + tt - (CONV_WIDTH - 1):pad + tt, :]
    hl_ref[0] = h
    xpad[0:pad, :] = xpad[tt:tt + pad, :]


def _rglru(xr, gr, conv_state, h0, conv_w, conv_b, w_a_b, b_a, w_i_b, b_i, rg_lambda, batch, t, tt):
    nt = t // tt
    row = lambda b, s: (b * nt + s, 0)
    const2 = lambda b, s: (0, 0)
    const3 = lambda b, s: (0, 0, 0)
    per_b = lambda b, s: (b, 0, 0)
    vec = pl.BlockSpec((1, D_RNN), const2)
    blk = pl.BlockSpec((N_RNN_BLOCKS, RNN_BLOCK, RNN_BLOCK), const3)
    big = pltpu.VMEM((tt, D_RNN), F32)
    return pl.pallas_call(
        functools.partial(_rglru_kernel, tt=tt),
        grid=(batch, nt),
        in_specs=[pl.BlockSpec((tt, D_RNN), row), pl.BlockSpec((tt, D_RNN), row),
                  pl.BlockSpec((1, CONV_WIDTH - 1, D_RNN), per_b), pl.BlockSpec((1, 1, D_RNN), per_b),
                  pl.BlockSpec((CONV_WIDTH, D_RNN), const2), vec, blk, vec, blk, vec, vec],
        out_specs=[pl.BlockSpec((tt, D_RNN), row),
                   pl.BlockSpec((1, CONV_WIDTH - 1, D_RNN), per_b),
                   pl.BlockSpec((1, 1, D_RNN), per_b)],
        out_shape=[jax.ShapeDtypeStruct((batch * t, D_RNN), BF16),
                   jax.ShapeDtypeStruct((batch, CONV_WIDTH - 1, D_RNN), F32),
                   jax.ShapeDtypeStruct((batch, 1, D_RNN), F32)],
        scratch_shapes=[pltpu.VMEM((tt + SUBLANES, D_RNN), F32), pltpu.VMEM((1, D_RNN), F32), big, big, big],
        compiler_params=_params("parallel", "arbitrary"),
    )(xr, gr, conv_state, h0, conv_w, conv_b, w_a_b, b_a, w_i_b, b_i, rg_lambda)


def _first_lane_of_max(vals, valid, lane):
    masked = jnp.where(valid, vals, -jnp.inf)
    mx = jnp.max(masked, axis=-1, keepdims=True)
    idx = jnp.min(jnp.where(valid & (masked == mx), lane, ROUTE_LANES), axis=-1, keepdims=True)
    return mx, idx


def _route(logits):
    lane = lax.broadcasted_iota(jnp.int32, logits.shape, 1)
    is_group = lane < N_GROUPS
    gmax, gsel = _first_lane_of_max(logits, is_group, lane)
    gsum = jnp.sum(jnp.where(is_group, jnp.exp(logits - gmax), 0.0), axis=-1, keepdims=True)
    g_w = 1.0 / gsum
    expert = lane - EXPERT_LANE0
    in_group = (expert >= 0) & (expert < N_EXPERTS) & ((expert // EXPERTS_PER_GROUP) == gsel)
    v1, i1 = _first_lane_of_max(logits, in_group, lane)
    v2, i2 = _first_lane_of_max(logits, in_group & (lane != i1), lane)
    e2 = jnp.exp(v2 - v1)
    den = 1.0 + e2
    return jnp.where(lane == i1, (1.0 / den) * g_w, 0.0) + jnp.where(lane == i2, (e2 / den) * g_w, 0.0)


def _merge_kernel(at_ref, y_ref, gt_ref, x_ref, wa_ref, wr_ref, wo_ref, bg_ref, g2_ref, wc_ref, bc_ref,
                  x1_ref, xn_ref, cm_ref, *, parts):
    rows_per_part = x_ref.shape[0] // parts
    for part in range(parts):
        rows = slice(part * rows_per_part, (part + 1) * rows_per_part)
        pa = jnp.dot(at_ref[rows, :], wa_ref[...], preferred_element_type=F32)
        pr = jnp.dot(y_ref[rows, :], wr_ref[...], preferred_element_type=F32)
        g = _sigmoid(gt_ref[rows, :] + bg_ref[...])
        merged = g[:, :D_MODEL] * pa + g[:, D_MODEL:] * pr
        x1 = x_ref[rows, :] + jnp.dot(merged.astype(BF16), wo_ref[...], preferred_element_type=F32)
        x1_ref[rows, :] = x1
        xn = _rms(x1, g2_ref[...]).astype(BF16)
        xn_ref[rows, :] = xn
        cm_ref[rows, :] = _route(jnp.dot(xn, wc_ref[...], preferred_element_type=F32) + bc_ref[...])


def _merge(attn, y_rnn, gates, x2d, wa_b, wr_b, wo_b, b_gate, g2, w_cat_b, b_cat, tm):
    n = x2d.shape[0]
    row = lambda i: (i, 0)
    const = lambda i: (0, 0)
    sq = pl.BlockSpec((D_MODEL, D_MODEL), const)
    return pl.pallas_call(
        functools.partial(_merge_kernel, parts=max(1, tm // MERGE_PART_ROWS)),
        grid=(n // tm,),
        in_specs=[pl.BlockSpec((tm, D_MODEL), row), pl.BlockSpec((tm, D_MODEL), row),
                  pl.BlockSpec((tm, 2 * D_MODEL), row), pl.BlockSpec((tm, D_MODEL), row),
                  sq, sq, sq, pl.BlockSpec((1, 2 * D_MODEL), const), pl.BlockSpec((1, D_MODEL), const),
                  pl.BlockSpec((D_MODEL, ROUTE_LANES), const), pl.BlockSpec((1, ROUTE_LANES), const)],
        out_specs=[pl.BlockSpec((tm, D_MODEL), row), pl.BlockSpec((tm, D_MODEL), row),
                   pl.BlockSpec((tm, ROUTE_LANES), row)],
        out_shape=[jax.ShapeDtypeStruct((n, D_MODEL), F32), jax.ShapeDtypeStruct((n, D_MODEL), BF16),
                   jax.ShapeDtypeStruct((n, ROUTE_LANES), F32)],
        compiler_params=_params("parallel"),
    )(attn, y_rnn, gates, x2d, wa_b, wr_b, wo_b, b_gate, g2, w_cat_b, b_cat)


def _route_tables(comb, tile, batch):
    n = comb.shape[0]
    seq, chunk = n // batch, tile // batch
    nt = seq // chunk
    c = comb[:, EXPERT_LANE0:EXPERT_LANE0 + N_EXPERTS]
    sel = jnp.any(c.reshape(batch, nt, chunk, N_GROUPS, EXPERTS_PER_GROUP) != 0.0, axis=-1)
    sel = sel.transpose(1, 0, 2, 3).reshape(nt, tile, N_GROUPS)
    csum = jnp.cumsum(sel.astype(jnp.int32), axis=1)
    rank = jnp.where(sel, csum - 1, -1).astype(F32)
    counts = csum[:, -1, :].reshape(nt * N_GROUPS)
    rank_t = rank.transpose(0, 2, 1).reshape(nt * N_GROUPS, 1, tile)
    rank = rank.reshape(nt, batch, chunk, N_GROUPS).transpose(1, 0, 2, 3).reshape(batch, seq, N_GROUPS)
    rank = jnp.pad(rank, ((0, 0), (0, 0), (0, ROUTE_LANES - N_GROUPS)), constant_values=-1.0)
    hi = c.astype(BF16)
    rest = c - hi.astype(F32)
    mid = rest.astype(BF16)
    lo = (rest - mid.astype(F32)).astype(BF16)
    c3 = jnp.pad(jnp.concatenate([hi, mid, lo], axis=1), ((0, 0), (0, ROUTE_LANES - 3 * N_EXPERTS)))
    return counts, rank, rank_t, c3.reshape(batch, seq, ROUTE_LANES)


def _moe_kernel(cnt_ref, x_ref, c3_ref, rk_ref, rkt_ref, x1_ref, gf_ref, *refs, tile):
    flat = lambda ref: ref[...].reshape(tile, ref.shape[-1])
    w1_refs, w3_refs, w2_refs = (refs[k * EXPERTS_PER_STEP:(k + 1) * EXPERTS_PER_STEP] for k in range(3))
    y_ref, xg, cg, yg = refs[3 * EXPERTS_PER_STEP:]
    t = pl.program_id(0)
    step = pl.program_id(1)
    steps_per_group = EXPERTS_PER_GROUP // EXPERTS_PER_STEP
    g = step // steps_per_group
    member = step % steps_per_group
    count = cnt_ref[t * N_GROUPS + g]
    nblk = (count + MOE_ROWS - 1) // MOE_ROWS

    @pl.when(step == 0)
    def _():
        y_ref[...] = jnp.zeros(y_ref.shape, F32)

    @pl.when(member == 0)
    def _():
        rank_row = rkt_ref[0]

        def gather(b, carry):
            slot = lax.broadcasted_iota(jnp.int32, (MOE_ROWS, tile), 0).astype(F32) + (b * MOE_ROWS).astype(F32)
            onehot = jnp.where(rank_row == slot, 1.0, 0.0).astype(BF16)
            xg[b] = jnp.dot(onehot, flat(x_ref), preferred_element_type=F32).astype(BF16)
            cg[b] = jnp.dot(onehot, flat(c3_ref), preferred_element_type=F32)
            yg[b] = jnp.zeros((MOE_ROWS, D_MODEL), F32)
            return carry

        lax.fori_loop(0, nblk, gather, 0)

    lane_r = lax.broadcasted_iota(jnp.int32, (MOE_ROWS, ROUTE_LANES), 1)

    def experts(b, carry):
        xc = xg[b]
        out = yg[b]
        for j in range(EXPERTS_PER_STEP):
            mine = (lane_r % N_EXPERTS == step * EXPERTS_PER_STEP + j) & (lane_r < 3 * N_EXPERTS)
            cw = jnp.sum(jnp.where(mine, cg[b], 0.0), axis=1, keepdims=True)
            a = jnp.dot(xc, w1_refs[j][0], preferred_element_type=F32)
            h = (a * _sigmoid(a)) * jnp.dot(xc, w3_refs[j][0], preferred_element_type=F32)
            out = out + cw * jnp.dot(h.astype(BF16), w2_refs[j][0], preferred_element_type=F32)
        yg[b] = out
        return carry

    lax.fori_loop(0, nblk, experts, 0)

    @pl.when(member == steps_per_group - 1)
    def _():
        lane = lax.broadcasted_iota(jnp.int32, (tile, ROUTE_LANES), 1)
        rank_col = jnp.sum(jnp.where(lane == g, flat(rk_ref), 0.0), axis=1, keepdims=True)

        def scatter(b, carry):
            slot_t = lax.broadcasted_iota(jnp.int32, (tile, MOE_ROWS), 1).astype(F32) + (b * MOE_ROWS).astype(F32)
            onehot_t = jnp.where(rank_col == slot_t, 1.0, 0.0).astype(BF16)
            y_ref[...] += jnp.dot(onehot_t, yg[b].astype(BF16), preferred_element_type=F32).reshape(y_ref.shape)
            return carry

        lax.fori_loop(0, nblk, scatter, 0)

    @pl.when(step == pl.num_programs(1) - 1)
    def _():
        y_ref[...] = _rms(x1_ref[...] + y_ref[...], gf_ref[...])


def _moe(xn2, comb, x1, gf, w1_b, w3_b, w2_b, tile, batch):
    n = xn2.shape[0]
    seq, chunk = n // batch, tile // batch
    counts, rank, rank_t, c3 = _route_tables(comb, tile, batch)
    max_blocks = -(-tile // MOE_ROWS)
    row = lambda t, s, cnt: (0, t, 0)
    rows = lambda width: pl.BlockSpec((batch, chunk, width), row)
    expert_specs = lambda shape: [pl.BlockSpec((1,) + shape, lambda t, s, cnt, j=j: (s * EXPERTS_PER_STEP + j, 0, 0))
                                  for j in range(EXPERTS_PER_STEP)]
    steps_per_group = EXPERTS_PER_GROUP // EXPERTS_PER_STEP
    return pl.pallas_call(
        functools.partial(_moe_kernel, tile=tile),
        grid_spec=pltpu.PrefetchScalarGridSpec(
            num_scalar_prefetch=1,
            grid=(n // tile, N_EXPERTS // EXPERTS_PER_STEP),
            in_specs=[rows(D_MODEL), rows(ROUTE_LANES), rows(ROUTE_LANES),
                      pl.BlockSpec((1, 1, tile), lambda t, s, cnt: (t * N_GROUPS + s // steps_per_group, 0, 0)),
                      rows(D_MODEL), pl.BlockSpec((1, D_MODEL), lambda t, s, cnt: (0, 0))]
            + expert_specs((D_MODEL, D_EXPERT)) + expert_specs((D_MODEL, D_EXPERT))
            + expert_specs((D_EXPERT, D_MODEL)),
            out_specs=rows(D_MODEL),
            scratch_shapes=[pltpu.VMEM((max_blocks, MOE_ROWS, D_MODEL), BF16),
                            pltpu.VMEM((max_blocks, MOE_ROWS, ROUTE_LANES), F32),
                            pltpu.VMEM((max_blocks, MOE_ROWS, D_MODEL), F32)]),
        out_shape=jax.ShapeDtypeStruct((batch, seq, D_MODEL), F32),
        compiler_params=_params("parallel", "arbitrary"),
    )(counts, xn2.reshape(batch, seq, D_MODEL), c3, rank, rank_t, x1.reshape(batch, seq, D_MODEL), gf, *([w1_b] * EXPERTS_PER_STEP + [w3_b] * EXPERTS_PER_STEP
                                                + [w2_b] * EXPERTS_PER_STEP))


def _stream(x, past, w, *, tm_in, tm_merge, tm_moe, tq, tt):
    batch, t, _ = x.shape
    n = batch * t
    x2d = x.reshape(n, D_MODEL)
    q, k, v, kb, vb, xr, gr, gates = _in_proj(x2d, w["g1"], w["w_in"], tm_in,
                                              seq=t if past is None else None)
    if past is None:
        attn = _attn_prompt(q, kb, vb, w["lams"], w["subln_g"], batch, t, tq)
        k = k.reshape(batch, N_HEADS, 2, QK_HEAD_DIM, t).transpose(0, 4, 1, 2, 3)
        conv_state = jnp.zeros((batch, CONV_WIDTH - 1, D_RNN), F32)
        h0 = jnp.zeros((batch, 1, D_RNN), F32)
    else:
        cache_k, cache_v, conv_state, h0 = past
        attn = _attn_sample(q, kb, vb, cache_k, cache_v, w["lams"], w["subln_g"], batch, t)
    y_rnn, new_conv, h_last = _rglru(xr, gr, conv_state, h0, w["conv_w"], w["conv_b"], w["w_a"], w["b_a"],
                                     w["w_i"], w["b_i"], w["rg_lambda"], batch, t, tt)
    x1, xn2, comb = _merge(attn, y_rnn, gates, x2d, w["w_attn"], w["w_rnn"], w["w_out"], w["b_gate"],
                           w["g2"], w["w_cat"], w["b_cat"], tm_merge)
    y = _moe(xn2, comb, x1, w["gf"], w["w1"], w["w3"], w["w2"], tm_moe, batch)
    return (y,
            k.reshape(1, batch, t, N_HEADS, 2, QK_HEAD_DIM),
            v.reshape(1, batch, t, N_HEADS, V_HEAD_DIM),
            new_conv.reshape(1, batch, CONV_WIDTH - 1, D_RNN),
            h_last.reshape(1, batch, D_RNN))


def kernel(x_prompt, x_sample, cache_k, cache_v, state_conv, state_rnn, norm1_g, w_in, lambda_q1, lambda_k1, lambda_q2, lambda_k2, subln_g, w_attn_proj, conv_w, conv_b, w_rg_a, b_rg_a, w_rg_i, b_rg_i, rg_lambda, w_rnn_proj, b_gate, w_out, norm2_g, w_group, b_group, w_router, b_router, w1, w3, w2, final_norm_g):
    assert norm1_g.shape[0] == 1, "single-layer model"
    dec_batch, past_len = cache_k.shape[1], cache_k.shape[2]
    pad = ROUTE_LANES - N_GROUPS - N_EXPERTS
    w = {
        "g1": norm1_g, "g2": norm2_g, "gf": final_norm_g.reshape(1, D_MODEL),
        "w_in": w_in[0].astype(BF16),
        "lams": (lambda_q1, lambda_k1, lambda_q2, lambda_k2),
        "subln_g": subln_g,
        "w_attn": w_attn_proj[0].astype(BF16), "w_rnn": w_rnn_proj[0].astype(BF16),
        "w_out": w_out[0].astype(BF16),
        "conv_w": conv_w[0], "conv_b": conv_b,
        "w_a": w_rg_a[0].astype(BF16), "b_a": b_rg_a, "w_i": w_rg_i[0].astype(BF16), "b_i": b_rg_i,
        "rg_lambda": rg_lambda, "b_gate": b_gate,
        "w_cat": jnp.pad(jnp.concatenate([w_group[0], w_router[0]], axis=1), ((0, 0), (0, pad))).astype(BF16),
        "b_cat": jnp.pad(jnp.concatenate([b_group[0], b_router[0]]), (0, pad)).reshape(1, ROUTE_LANES),
        "w1": w1[0].astype(BF16), "w3": w3[0].astype(BF16), "w2": w2[0].astype(BF16),
    }
    yp, kp, vp, cp, hp = _stream(x_prompt, None, w, tm_in=256, tm_merge=512, tm_moe=1024, tq=512, tt=256)
    past = (cache_k[0].transpose(0, 2, 3, 4, 1).reshape(dec_batch, D_MODEL, past_len),
            cache_v[0].reshape(dec_batch, past_len * N_HEADS, V_HEAD_DIM),
            state_conv[0], state_rnn[0].reshape(dec_batch, 1, D_RNN))
    dec_t = x_sample.shape[1]
    ys, ks, vs, cs, hs = _stream(x_sample, past, w, tm_in=256, tm_merge=256, tm_moe=256, tq=dec_t, tt=dec_t)
    return (yp, ys, kp, vp, cp, hp, ks, vs, cs, hs)
```

```python
import functools
import math

import jax
import jax.numpy as jnp
from jax import lax
from jax.experimental import pallas as pl
from jax.experimental.pallas import tpu as pltpu

F32 = jnp.float32
BF16 = jnp.bfloat16

D_MODEL = 1024
CHUNK = 64
N_HEADS = 8
QK_HEAD_DIM = 64
V_HEAD_DIM = 128
HEAD_COLS = 2 * QK_HEAD_DIM
D_RNN = 1024
N_RNN_BLOCKS = 8
RNN_BLOCK = D_RNN // N_RNN_BLOCKS
CONV_WIDTH = 4
RG_C = 8.0
N_GROUPS = 4
EXPERTS_PER_GROUP = 4
N_EXPERTS = N_GROUPS * EXPERTS_PER_GROUP
D_EXPERT = 512
EPS = 1e-6
LAMBDA_INIT = 0.8 - 0.6 * math.exp(-0.3 * 0)
Q_SCALE = QK_HEAD_DIM ** -0.5 * math.log2(math.e)
N_SEG = 7
IN_WIDTH = N_SEG * D_MODEL
ROUTE_LANES = 128
EXPERTS_PER_STEP = 4
MOE_ROWS = 288
EXPERT_LANE0 = N_GROUPS
SUBLANES = 8
LANES = 128
HEADS_PER_STEP = 8
MERGE_PART_ROWS = 256
VMEM_LIMIT = 56 * 1024 * 1024
NEG = float(jnp.finfo(jnp.float32).min)


def _params(*sem):
    return pltpu.CompilerParams(dimension_semantics=sem, vmem_limit_bytes=VMEM_LIMIT)


def _rms(x, g):
    return x * lax.rsqrt(jnp.mean(x * x, axis=-1, keepdims=True) + EPS) * g


def _sigmoid(x):
    return 0.5 * jnp.tanh(0.5 * x) + 0.5


def _in_proj_kernel(x_ref, g_ref, w_ref, q_ref, k_ref, v_ref, kb_ref, vb_ref, xr_ref, gr_ref, gt_ref,
                    *, k_transposed):
    xn = _rms(x_ref[...], g_ref[...]).astype(BF16)

    def seg(j):
        return jnp.dot(xn, w_ref[:, j * D_MODEL:(j + 1) * D_MODEL], preferred_element_type=F32)

    q_ref[...] = (seg(0) * Q_SCALE).astype(BF16)
    if k_transposed:
        kt = seg(1).T
        k_ref[0] = kt
        kb_ref[0, 0] = kt.astype(BF16)
    else:
        k = seg(1)
        k_ref[...] = k
        kb_ref[...] = k.astype(BF16)
    v = seg(2)
    v_ref[...] = v
    vb_ref[...] = v.astype(BF16)
    xr_ref[...] = seg(3)
    gr_ref[...] = seg(4)
    gt_ref[:, :D_MODEL] = seg(5)
    gt_ref[:, D_MODEL:] = seg(6)


def _in_proj(x2d, g1, w_in_b, tm, seq=None):
    n = x2d.shape[0]
    row = lambda i: (i, 0)
    const = lambda i: (0, 0)
    wide = lambda dt: jax.ShapeDtypeStruct((n, D_MODEL), dt)
    wide_spec = pl.BlockSpec((tm, D_MODEL), row)
    if seq is None:
        k_shape, k_spec, kb_shape, kb_spec = wide(F32), wide_spec, wide(BF16), wide_spec
    else:
        nt = seq // tm
        k_shape = jax.ShapeDtypeStruct((n // seq, D_MODEL, seq), F32)
        k_spec = pl.BlockSpec((1, D_MODEL, tm), lambda i: (i // nt, 0, i % nt))
        kb_shape = jax.ShapeDtypeStruct((n // seq, nt, D_MODEL, tm), BF16)
        kb_spec = pl.BlockSpec((1, 1, D_MODEL, tm), lambda i: (i // nt, i % nt, 0, 0))
    return pl.pallas_call(
        functools.partial(_in_proj_kernel, k_transposed=seq is not None),
        grid=(n // tm,),
        in_specs=[wide_spec,
                  pl.BlockSpec((1, D_MODEL), const),
                  pl.BlockSpec((D_MODEL, IN_WIDTH), const, pipeline_mode=pl.Buffered(1))],
        out_specs=[wide_spec, k_spec, wide_spec, kb_spec, wide_spec, wide_spec, wide_spec,
                   pl.BlockSpec((tm, 2 * D_MODEL), row)],
        out_shape=[wide(BF16), k_shape, wide(F32), kb_shape, wide(BF16), wide(F32), wide(F32),
                   jax.ShapeDtypeStruct((n, 2 * D_MODEL), F32)],
        compiler_params=_params("parallel"),
    )(x2d, g1, w_in_b)


def _lam(lq1, lk1, lq2, lk2):
    return (jnp.exp(jnp.sum(lq1[...] * lk1[...], axis=-1, keepdims=True))
            - jnp.exp(jnp.sum(lq2[...] * lk2[...], axis=-1, keepdims=True)) + LAMBDA_INIT)


def _split_q(q):
    lane = lax.broadcasted_iota(jnp.int32, q.shape, 1)
    zero = jnp.zeros_like(q)
    return jnp.where(lane < QK_HEAD_DIM, q, zero), jnp.where(lane >= QK_HEAD_DIM, q, zero)


def _scores(qm, k):
    return lax.dot_general(qm, k, (((1,), (1,)), ((), ())), preferred_element_type=F32)


def _subln(o, g):
    return (_rms(o, g) * (1.0 - LAMBDA_INIT)).astype(BF16)


def _attn_prompt_kernel(lq1, lk1, lq2, lk2, sg_ref, q_ref, k_ref, v_ref, o_ref, m_ref, acc_ref, *, tq, kw):
    i = pl.program_id(2)
    ones = jnp.ones((tq, LANES), BF16)
    qq = []
    for h in range(HEADS_PER_STEP):
        q1, q2 = _split_q(q_ref[:, h * HEAD_COLS:(h + 1) * HEAD_COLS])
        qq.append(jnp.concatenate([q1, q2], axis=0))

    def row_max(chunks):
        part = chunks[0]
        for x in chunks[1:]:
            part = jnp.maximum(part, x)
        return jnp.max(part, axis=1, keepdims=True)

    def split(s):
        return [s[:, c * LANES:(c + 1) * LANES] for c in range(s.shape[1] // LANES)]

    def head_cols(h, width):
        return slice(h * width, (h + 1) * width)

    def diagonal_tile(h):
        row = lax.broadcasted_iota(jnp.int32, (2 * kw, LANES), 0)
        lane = lax.broadcasted_iota(jnp.int32, (2 * kw, LANES), 1)
        qchunk = (row % kw) // CHUNK
        for r in range(tq // kw):
            rows1, rows2 = slice(r * kw, (r + 1) * kw), slice(tq + r * kw, tq + (r + 1) * kw)
            qr = jnp.concatenate([qq[h][rows1], qq[h][rows2]], axis=0)
            chunks = []
            for j in range(r + 1):
                s = split(jnp.dot(qr, k_ref[0, i * (tq // kw) + j, head_cols(h, HEAD_COLS), :],
                                  preferred_element_type=F32))
                if j == r:
                    s = [jnp.where((lane + c * LANES) // CHUNK <= qchunk, x, NEG) for c, x in enumerate(s)]
                chunks += s
            m_new = jnp.broadcast_to(row_max(chunks), (2 * kw, LANES))
            p = jnp.concatenate([jnp.exp2(x - m_new).astype(BF16) for x in chunks], axis=1)
            keys = (r + 1) * kw
            start = pl.multiple_of(i * tq, tq)
            v1 = jnp.concatenate([v_ref[pl.ds(start, keys), head_cols(h, V_HEAD_DIM)],
                                  jnp.ones((keys, LANES), BF16)], axis=1)
            pv = jnp.dot(p, v1, preferred_element_type=F32)
            for rows, part in ((rows1, slice(0, kw)), (rows2, slice(kw, 2 * kw))):
                acc_ref[h, rows, :] = pv[part]
                m_ref[h, rows, :] = m_new[part]

    def tile(kt, h):
        start = pl.multiple_of(kt * tq, tq)
        chunks = []
        for j in range(tq // kw):
            chunks += split(jnp.dot(qq[h], k_ref[0, kt * (tq // kw) + j, head_cols(h, HEAD_COLS), :],
                                    preferred_element_type=F32))
        m_next = jnp.maximum(m_ref[h], row_max(chunks))
        p = jnp.concatenate([jnp.exp2(x - m_next).astype(BF16) for x in chunks], axis=1)
        v1 = jnp.concatenate([v_ref[pl.ds(start, tq), head_cols(h, V_HEAD_DIM)], ones], axis=1)
        alpha = jnp.exp2(m_ref[h] - m_next)
        acc_ref[h] = (jnp.concatenate([alpha, alpha], axis=1) * acc_ref[h]
                      + jnp.dot(p, v1, preferred_element_type=F32))
        m_ref[h] = m_next

    def body(kt, c):
        for h in range(HEADS_PER_STEP):
            tile(kt, h)
        return c

    for h in range(HEADS_PER_STEP):
        diagonal_tile(h)
    lax.fori_loop(0, i, body, 0)
    lam = _lam(lq1, lk1, lq2, lk2)
    for h in range(HEADS_PER_STEP):
        o1 = acc_ref[h, :tq, :V_HEAD_DIM] / acc_ref[h, :tq, V_HEAD_DIM:]
        o2 = acc_ref[h, tq:, :V_HEAD_DIM] / acc_ref[h, tq:, V_HEAD_DIM:]
        o_ref[:, h * V_HEAD_DIM:(h + 1) * V_HEAD_DIM] = _subln(o1 - lam * o2, sg_ref[...])


def _attn_prompt(q, kbt, vb, lams, subln_g, batch, seq, tq):
    n = q.shape[0]
    nq = seq // tq
    nkb, kw = kbt.shape[1], kbt.shape[3]
    small = lambda b, h, i: (0, 0)
    qmap = lambda b, h, i: (b * nq + i, h)
    kvmap = lambda b, h, i: (b, h)
    return pl.pallas_call(
        functools.partial(_attn_prompt_kernel, tq=tq, kw=kw),
        grid=(batch, N_HEADS // HEADS_PER_STEP, nq),
        in_specs=[pl.BlockSpec((1, QK_HEAD_DIM), small)] * 4 + [
            pl.BlockSpec((1, V_HEAD_DIM), small),
            pl.BlockSpec((tq, HEADS_PER_STEP * HEAD_COLS), qmap),
            pl.BlockSpec((1, nkb, HEADS_PER_STEP * HEAD_COLS, kw), lambda b, h, i: (b, 0, h, 0)),
            pl.BlockSpec((seq, HEADS_PER_STEP * V_HEAD_DIM), kvmap)],
        out_specs=pl.BlockSpec((tq, HEADS_PER_STEP * V_HEAD_DIM), qmap),
        out_shape=jax.ShapeDtypeStruct((n, N_HEADS * V_HEAD_DIM), BF16),
        scratch_shapes=[pltpu.VMEM((HEADS_PER_STEP, 2 * tq, LANES), F32),
                        pltpu.VMEM((HEADS_PER_STEP, 2 * tq, 2 * LANES), F32)],
        compiler_params=_params("parallel", "parallel", "arbitrary"),
    )(*lams, subln_g, q, kbt, vb)


def _attn_sample_kernel(lq1, lk1, lq2, lk2, sg_ref, q_ref, kn_ref, vn_ref, ck_ref, cv_ref, o_ref, *, past):
    lam = _lam(lq1, lk1, lq2, lk2)
    for h in range(N_HEADS):
        cols = slice(h * HEAD_COLS, (h + 1) * HEAD_COLS)
        qq = jnp.concatenate(_split_q(q_ref[:, cols]), axis=0)
        kct = ck_ref[0, h * HEAD_COLS:(h + 1) * HEAD_COLS, :].astype(BF16)
        vc = cv_ref[0, pl.ds(h, past, stride=N_HEADS), :].astype(BF16)
        sc = jnp.dot(qq, kct, preferred_element_type=F32)
        sn = _scores(qq, kn_ref[:, cols])
        m = jnp.maximum(jnp.max(sc, axis=-1, keepdims=True), jnp.max(sn, axis=-1, keepdims=True))
        pc = jnp.exp2(sc - m)
        pn = jnp.exp2(sn - m)
        l = jnp.sum(pc, axis=-1, keepdims=True) + jnp.sum(pn, axis=-1, keepdims=True)
        acc = (jnp.dot(pc.astype(BF16), vc, preferred_element_type=F32)
               + jnp.dot(pn.astype(BF16), vn_ref[:, cols], preferred_element_type=F32))
        o = acc / l
        t = q_ref.shape[0]
        o_ref[:, cols] = _subln(o[:t] - lam * o[t:], sg_ref[...])


def _attn_sample(q, kb, vb, cache_kt, cache_v, lams, subln_g, batch, t):
    past = cache_kt.shape[2]
    small = lambda b: (0, 0)
    row = lambda b: (b, 0)
    cache = lambda b: (b, 0, 0)
    return pl.pallas_call(
        functools.partial(_attn_sample_kernel, past=past),
        grid=(batch,),
        in_specs=[pl.BlockSpec((1, QK_HEAD_DIM), small)] * 4 + [
            pl.BlockSpec((1, V_HEAD_DIM), small),
            pl.BlockSpec((t, D_MODEL), row),
            pl.BlockSpec((t, D_MODEL), row),
            pl.BlockSpec((t, D_MODEL), row),
            pl.BlockSpec((1, D_MODEL, past), cache),
            pl.BlockSpec((1, past * N_HEADS, V_HEAD_DIM), cache)],
        out_specs=pl.BlockSpec((t, D_MODEL), row),
        out_shape=jax.ShapeDtypeStruct((batch * t, D_MODEL), BF16),
        compiler_params=_params("parallel"),
    )(*lams, subln_g, q, kb, vb, cache_kt, cache_v)


def _rglru_kernel(xr_ref, gr_ref, cs_ref, h0_ref, cw_ref, cb_ref, wa_ref, ba_ref, wi_ref, bi_ref, lam_ref,
                  y_ref, nc_ref, hl_ref, xpad, hcar, a_s, b_s, h_s, *, tt):
    t = pl.program_id(1)
    pad = SUBLANES

    @pl.when(t == 0)
    def _():
        xpad[0:pad, :] = jnp.zeros((pad, D_RNN), F32)
        xpad[pad - (CONV_WIDTH - 1):pad, :] = cs_ref[0]
        hcar[...] = h0_ref[0]

    xpad[pad:pad + tt, :] = xr_ref[...]
    xc = cb_ref[...]
    for j in range(CONV_WIDTH):
        off = pad - (CONV_WIDTH - 1) + j
        xc = xc + xpad[off:off + tt, :] * cw_ref[j:j + 1, :]
    xcb = xc.astype(BF16)

    z = -lam_ref[...]
    softplus = jnp.maximum(z, 0.0) + jnp.log1p(jnp.exp(-jnp.abs(z)))
    for n in range(N_RNN_BLOCKS):
        cols = slice(n * RNN_BLOCK, (n + 1) * RNN_BLOCK)
        xs = xcb[:, cols]
        r = _sigmoid(jnp.dot(xs, wa_ref[n], preferred_element_type=F32) + ba_ref[:, cols])
        i = _sigmoid(jnp.dot(xs, wi_ref[n], preferred_element_type=F32) + bi_ref[:, cols])
        log_a = -RG_C * r * softplus[:, cols]
        a_s[:, cols] = jnp.exp(log_a)
        th = jnp.tanh(log_a)
        b_s[:, cols] = jnp.sqrt(-2.0 * th / (1.0 - th)) * (i * xc[:, cols])

    h = hcar[...]
    for s in range(tt):
        h = a_s[s:s + 1, :] * h + b_s[s:s + 1, :]
        h_s[s:s + 1, :] = h
    hcar[...] = h
    y_ref[...] = (h_s[...] * jax.nn.gelu(gr_ref[...])).astype(BF16)
    nc_ref[0] = xpad[pad + tt - (CONV_WIDTH - 1):pad + tt, :]
    hl_ref[0] = h
    xpad[0:pad, :] = xpad[tt:tt + pad, :]


def _rglru(xr, gr, conv_state, h0, conv_w, conv_b, w_a_b, b_a, w_i_b, b_i, rg_lambda, batch, t, tt):
    nt = t // tt
    row = lambda b, s: (b * nt + s, 0)
    const2 = lambda b, s: (0, 0)
    const3 = lambda b, s: (0, 0, 0)
    per_b = lambda b, s: (b, 0, 0)
    vec = pl.BlockSpec((1, D_RNN), const2)
    blk = pl.BlockSpec((N_RNN_BLOCKS, RNN_BLOCK, RNN_BLOCK), const3)
    big = pltpu.VMEM((tt, D_RNN), F32)
    return pl.pallas_call(
        functools.partial(_rglru_kernel, tt=tt),
        grid=(batch, nt),
        in_specs=[pl.BlockSpec((tt, D_RNN), row), pl.BlockSpec((tt, D_RNN), row),
                  pl.BlockSpec((1, CONV_WIDTH - 1, D_RNN), per_b), pl.BlockSpec((1, 1, D_RNN), per_b),
                  pl.BlockSpec((CONV_WIDTH, D_RNN), const2), vec, blk, vec, blk, vec, vec],
        out_specs=[pl.BlockSpec((tt, D_RNN), row),
                   pl.BlockSpec((1, CONV_WIDTH - 1, D_RNN), per_b),
                   pl.BlockSpec((1, 1, D_RNN), per_b)],
        out_shape=[jax.ShapeDtypeStruct((batch * t, D_RNN), BF16),
                   jax.ShapeDtypeStruct((batch, CONV_WIDTH - 1, D_RNN), F32),
                   jax.ShapeDtypeStruct((batch, 1, D_RNN), F32)],
        scratch_shapes=[pltpu.VMEM((tt + SUBLANES, D_RNN), F32), pltpu.VMEM((1, D_RNN), F32), big, big, big],
        compiler_params=_params("parallel", "arbitrary"),
    )(xr, gr, conv_state, h0, conv_w, conv_b, w_a_b, b_a, w_i_b, b_i, rg_lambda)


def _first_lane_of_max(vals, valid, lane):
    masked = jnp.where(valid, vals, -jnp.inf)
    mx = jnp.max(masked, axis=-1, keepdims=True)
    idx = jnp.min(jnp.where(valid & (masked == mx), lane, ROUTE_LANES), axis=-1, keepdims=True)
    return mx, idx


def _route(logits):
    lane = lax.broadcasted_iota(jnp.int32, logits.shape, 1)
    is_group = lane < N_GROUPS
    gmax, gsel = _first_lane_of_max(logits, is_group, lane)
    gsum = jnp.sum(jnp.where(is_group, jnp.exp(logits - gmax), 0.0), axis=-1, keepdims=True)
    g_w = 1.0 / gsum
    expert = lane - EXPERT_LANE0
    in_group = (expert >= 0) & (expert < N_EXPERTS) & ((expert // EXPERTS_PER_GROUP) == gsel)
    v1, i1 = _first_lane_of_max(logits, in_group, lane)
    v2, i2 = _first_lane_of_max(logits, in_group & (lane != i1), lane)
    e2 = jnp.exp(v2 - v1)
    den = 1.0 + e2
    return jnp.where(lane == i1, (1.0 / den) * g_w, 0.0) + jnp.where(lane == i2, (e2 / den) * g_w, 0.0)


def _merge_kernel(at_ref, y_ref, gt_ref, x_ref, wa_ref, wr_ref, wo_ref, bg_ref, g2_ref, wc_ref, bc_ref, *refs,
                  parts, n_cast):
    x1_ref, xn_ref, cm_ref = refs[n_cast:n_cast + 3]
    for src, dst in zip(refs[:n_cast], refs[n_cast + 3:]):
        dst[...] = src[...].astype(BF16)
    rows_per_part = x_ref.shape[0] // parts
    for part in range(parts):
        rows = slice(part * rows_per_part, (part + 1) * rows_per_part)
        pa = jnp.dot(at_ref[rows, :], wa_ref[...], preferred_element_type=F32)
        pr = jnp.dot(y_ref[rows, :], wr_ref[...], preferred_element_type=F32)
        g = _sigmoid(gt_ref[rows, :] + bg_ref[...])
        merged = g[:, :D_MODEL] * pa + g[:, D_MODEL:] * pr
        x1 = x_ref[rows, :] + jnp.dot(merged.astype(BF16), wo_ref[...], preferred_element_type=F32)
        x1_ref[rows, :] = x1
        xn = _rms(x1, g2_ref[...]).astype(BF16)
        xn_ref[rows, :] = xn
        cm_ref[rows, :] = _route(jnp.dot(xn, wc_ref[...], preferred_element_type=F32) + bc_ref[...])


def _merge(attn, y_rnn, gates, x2d, wa_b, wr_b, wo_b, b_gate, g2, w_cat_b, b_cat, tm, cast=()):
    n = x2d.shape[0]
    steps = n // tm
    row = lambda i: (i, 0)
    const = lambda i: (0, 0)
    sq = pl.BlockSpec((D_MODEL, D_MODEL), const)
    slabs = [a.reshape(steps, -1, a.shape[-1]) for a in cast]
    slab_specs = [pl.BlockSpec((1,) + s.shape[1:], lambda i: (i, 0, 0)) for s in slabs]
    outs = pl.pallas_call(
        functools.partial(_merge_kernel, parts=max(1, tm // MERGE_PART_ROWS), n_cast=len(cast)),
        grid=(steps,),
        in_specs=[pl.BlockSpec((tm, D_MODEL), row), pl.BlockSpec((tm, D_MODEL), row),
                  pl.BlockSpec((tm, 2 * D_MODEL), row), pl.BlockSpec((tm, D_MODEL), row),
                  sq, sq, sq, pl.BlockSpec((1, 2 * D_MODEL), const), pl.BlockSpec((1, D_MODEL), const),
                  pl.BlockSpec((D_MODEL, ROUTE_LANES), const), pl.BlockSpec((1, ROUTE_LANES), const)] + slab_specs,
        out_specs=[pl.BlockSpec((tm, D_MODEL), row), pl.BlockSpec((tm, D_MODEL), row),
                   pl.BlockSpec((tm, ROUTE_LANES), row)] + slab_specs,
        out_shape=[jax.ShapeDtypeStruct((n, D_MODEL), F32), jax.ShapeDtypeStruct((n, D_MODEL), BF16),
                   jax.ShapeDtypeStruct((n, ROUTE_LANES), F32)]
        + [jax.ShapeDtypeStruct(s.shape, BF16) for s in slabs],
        compiler_params=_params("parallel"),
    )(attn, y_rnn, gates, x2d, wa_b, wr_b, wo_b, b_gate, g2, w_cat_b, b_cat, *slabs)
    return outs[0], outs[1], outs[2], tuple(o.reshape(a.shape) for o, a in zip(outs[3:], cast))


def _route_tables(comb, tile, batch):
    n = comb.shape[0]
    seq, chunk = n // batch, tile // batch
    nt = seq // chunk
    c = comb[:, EXPERT_LANE0:EXPERT_LANE0 + N_EXPERTS]
    sel = jnp.any(c.reshape(batch, nt, chunk, N_GROUPS, EXPERTS_PER_GROUP) != 0.0, axis=-1)
    sel = sel.transpose(1, 0, 2, 3).reshape(nt, tile, N_GROUPS)
    csum = jnp.cumsum(sel.astype(jnp.int32), axis=1)
    rank = jnp.where(sel, csum - 1, -1).astype(F32)
    counts = csum[:, -1, :].reshape(nt * N_GROUPS)
    rank_t = rank.transpose(0, 2, 1).reshape(nt * N_GROUPS, 1, tile)
    rank = rank.reshape(nt, batch, chunk, N_GROUPS).transpose(1, 0, 2, 3).reshape(batch, seq, N_GROUPS)
    rank = jnp.pad(rank, ((0, 0), (0, 0), (0, ROUTE_LANES - N_GROUPS)), constant_values=-1.0)
    hi = c.astype(BF16)
    rest = c - hi.astype(F32)
    mid = rest.astype(BF16)
    lo = (rest - mid.astype(F32)).astype(BF16)
    c3 = jnp.pad(jnp.concatenate([hi, mid, lo], axis=1), ((0, 0), (0, ROUTE_LANES - 3 * N_EXPERTS)))
    return counts, rank, rank_t, c3.reshape(batch, seq, ROUTE_LANES)


def _moe_kernel(cnt_ref, x_ref, c3_ref, rk_ref, rkt_ref, x1_ref, gf_ref, *refs, tile):
    flat = lambda ref: ref[...].reshape(tile, ref.shape[-1])
    w1_refs, w3_refs, w2_refs = (refs[k * EXPERTS_PER_STEP:(k + 1) * EXPERTS_PER_STEP] for k in range(3))
    y_ref, xg, cg, yg = refs[3 * EXPERTS_PER_STEP:]
    t = pl.program_id(0)
    step = pl.program_id(1)
    steps_per_group = EXPERTS_PER_GROUP // EXPERTS_PER_STEP
    g = step // steps_per_group
    member = step % steps_per_group
    count = cnt_ref[t * N_GROUPS + g]
    nblk = (count + MOE_ROWS - 1) // MOE_ROWS

    @pl.when(step == 0)
    def _():
        y_ref[...] = jnp.zeros(y_ref.shape, F32)

    @pl.when(member == 0)
    def _():
        rank_row = rkt_ref[0]

        def gather(b, carry):
            slot = lax.broadcasted_iota(jnp.int32, (MOE_ROWS, tile), 0).astype(F32) + (b * MOE_ROWS).astype(F32)
            onehot = jnp.where(rank_row == slot, 1.0, 0.0).astype(BF16)
            xg[b] = jnp.dot(onehot, flat(x_ref), preferred_element_type=F32).astype(BF16)
            cg[b] = jnp.dot(onehot, flat(c3_ref), preferred_element_type=F32)
            yg[b] = jnp.zeros((MOE_ROWS, D_MODEL), F32)
            return carry

        lax.fori_loop(0, nblk, gather, 0)

    lane_r = lax.broadcasted_iota(jnp.int32, (MOE_ROWS, ROUTE_LANES), 1)

    def experts(b, carry):
        xc = xg[b]
        out = yg[b]
        for j in range(EXPERTS_PER_STEP):
            mine = (lane_r % N_EXPERTS == step * EXPERTS_PER_STEP + j) & (lane_r < 3 * N_EXPERTS)
            cw = jnp.sum(jnp.where(mine, cg[b], 0.0), axis=1, keepdims=True)
            a = jnp.dot(xc, w1_refs[j][0], preferred_element_type=F32)
            h = (a * _sigmoid(a)) * jnp.dot(xc, w3_refs[j][0], preferred_element_type=F32)
            out = out + cw * jnp.dot(h.astype(BF16), w2_refs[j][0], preferred_element_type=F32)
        yg[b] = out
        return carry

    lax.fori_loop(0, nblk, experts, 0)

    @pl.when(member == steps_per_group - 1)
    def _():
        lane = lax.broadcasted_iota(jnp.int32, (tile, ROUTE_LANES), 1)
        rank_col = jnp.sum(jnp.where(lane == g, flat(rk_ref), 0.0), axis=1, keepdims=True)

        def scatter(b, carry):
            slot_t = lax.broadcasted_iota(jnp.int32, (tile, MOE_ROWS), 1).astype(F32) + (b * MOE_ROWS).astype(F32)
            onehot_t = jnp.where(rank_col == slot_t, 1.0, 0.0).astype(BF16)
            y_ref[...] += jnp.dot(onehot_t, yg[b].astype(BF16), preferred_element_type=F32).reshape(y_ref.shape)
            return carry

        lax.fori_loop(0, nblk, scatter, 0)

    @pl.when(step == pl.num_programs(1) - 1)
    def _():
        y_ref[...] = _rms(x1_ref[...] + y_ref[...], gf_ref[...])


def _moe(xn2, comb, x1, gf, w1_b, w3_b, w2_b, tile, batch):
    n = xn2.shape[0]
    seq, chunk = n // batch, tile // batch
    counts, rank, rank_t, c3 = _route_tables(comb, tile, batch)
    max_blocks = -(-tile // MOE_ROWS)
    row = lambda t, s, cnt: (0, t, 0)
    rows = lambda width: pl.BlockSpec((batch, chunk, width), row)
    expert_specs = lambda shape: [pl.BlockSpec((1,) + shape, lambda t, s, cnt, j=j: (s * EXPERTS_PER_STEP + j, 0, 0))
                                  for j in range(EXPERTS_PER_STEP)]
    steps_per_group = EXPERTS_PER_GROUP // EXPERTS_PER_STEP
    return pl.pallas_call(
        functools.partial(_moe_kernel, tile=tile),
        grid_spec=pltpu.PrefetchScalarGridSpec(
            num_scalar_prefetch=1,
            grid=(n // tile, N_EXPERTS // EXPERTS_PER_STEP),
            in_specs=[rows(D_MODEL), rows(ROUTE_LANES), rows(ROUTE_LANES),
                      pl.BlockSpec((1, 1, tile), lambda t, s, cnt: (t * N_GROUPS + s // steps_per_group, 0, 0)),
                      rows(D_MODEL), pl.BlockSpec((1, D_MODEL), lambda t, s, cnt: (0, 0))]
            + expert_specs((D_MODEL, D_EXPERT)) + expert_specs((D_MODEL, D_EXPERT))
            + expert_specs((D_EXPERT, D_MODEL)),
            out_specs=rows(D_MODEL),
            scratch_shapes=[pltpu.VMEM((max_blocks, MOE_ROWS, D_MODEL), BF16),
                            pltpu.VMEM((max_blocks, MOE_ROWS, ROUTE_LANES), F32),
                            pltpu.VMEM((max_blocks, MOE_ROWS, D_MODEL), F32)]),
        out_shape=jax.ShapeDtypeStruct((batch, seq, D_MODEL), F32),
        compiler_params=_params("parallel", "arbitrary"),
    )(counts, xn2.reshape(batch, seq, D_MODEL), c3, rank, rank_t, x1.reshape(batch, seq, D_MODEL), gf, *([w1_b] * EXPERTS_PER_STEP + [w3_b] * EXPERTS_PER_STEP
                                                + [w2_b] * EXPERTS_PER_STEP))


def _tiles(batch, t):
    n = batch * t
    return dict(tm_in=min(256, n),
                tm_merge=min(2 * MERGE_PART_ROWS, n),
                tm_moe=min(1024, n),
                tq=min(512, t),
                tt=min(256, t))


def _stream(x, past, w, experts):
    batch, t, _ = x.shape
    n = batch * t
    tm_in, tm_merge, tm_moe, tq, tt = (_tiles(batch, t)[k] for k in ("tm_in", "tm_merge", "tm_moe", "tq", "tt"))
    x2d = x.reshape(n, D_MODEL)
    q, k, v, kb, vb, xr, gr, gates = _in_proj(x2d, w["g1"], w["w_in"], tm_in,
                                              seq=t if past is None else None)
    if past is None:
        attn = _attn_prompt(q, kb, vb, w["lams"], w["subln_g"], batch, t, tq)
        k = k.reshape(batch, N_HEADS, 2, QK_HEAD_DIM, t).transpose(0, 4, 1, 2, 3)
        conv_state = jnp.zeros((batch, CONV_WIDTH - 1, D_RNN), F32)
        h0 = jnp.zeros((batch, 1, D_RNN), F32)
    else:
        cache_k, cache_v, conv_state, h0 = past
        attn = _attn_sample(q, kb, vb, cache_k, cache_v, w["lams"], w["subln_g"], batch, t)
    y_rnn, new_conv, h_last = _rglru(xr, gr, conv_state, h0, w["conv_w"], w["conv_b"], w["w_a"], w["b_a"],
                                     w["w_i"], w["b_i"], w["rg_lambda"], batch, t, tt)
    to_cast = experts if experts[0].dtype != BF16 else ()
    x1, xn2, comb, cast = _merge(attn, y_rnn, gates, x2d, w["w_attn"], w["w_rnn"], w["w_out"], w["b_gate"],
                                 w["g2"], w["w_cat"], w["b_cat"], tm_merge, cast=to_cast)
    experts = cast or experts
    y = _moe(xn2, comb, x1, w["gf"], *experts, tm_moe, batch)
    return (y,
            k.reshape(1, batch, t, N_HEADS, 2, QK_HEAD_DIM),
            v.reshape(1, batch, t, N_HEADS, V_HEAD_DIM),
            new_conv.reshape(1, batch, CONV_WIDTH - 1, D_RNN),
            h_last.reshape(1, batch, D_RNN)), experts


def kernel(x_prompt, x_sample, cache_k, cache_v, state_conv, state_rnn, norm1_g, w_in, lambda_q1, lambda_k1, lambda_q2, lambda_k2, subln_g, w_attn_proj, conv_w, conv_b, w_rg_a, b_rg_a, w_rg_i, b_rg_i, rg_lambda, w_rnn_proj, b_gate, w_out, norm2_g, w_group, b_group, w_router, b_router, w1, w3, w2, final_norm_g):
    assert norm1_g.shape[0] == 1, "single-layer model"
    dec_batch, past_len = cache_k.shape[1], cache_k.shape[2]
    pad = ROUTE_LANES - N_GROUPS - N_EXPERTS
    w = {
        "g1": norm1_g, "g2": norm2_g, "gf": final_norm_g.reshape(1, D_MODEL),
        "w_in": w_in[0].astype(BF16),
        "lams": (lambda_q1, lambda_k1, lambda_q2, lambda_k2),
        "subln_g": subln_g,
        "w_attn": w_attn_proj[0].astype(BF16), "w_rnn": w_rnn_proj[0].astype(BF16),
        "w_out": w_out[0].astype(BF16),
        "conv_w": conv_w[0], "conv_b": conv_b,
        "w_a": w_rg_a[0].astype(BF16), "b_a": b_rg_a, "w_i": w_rg_i[0].astype(BF16), "b_i": b_rg_i,
        "rg_lambda": rg_lambda, "b_gate": b_gate,
        "w_cat": jnp.pad(jnp.concatenate([w_group[0], w_router[0]], axis=1), ((0, 0), (0, pad))).astype(BF16),
        "b_cat": jnp.pad(jnp.concatenate([b_group[0], b_router[0]]), (0, pad)).reshape(1, ROUTE_LANES),
    }
    (yp, kp, vp, cp, hp), experts_b = _stream(x_prompt, None, w, (w1[0], w3[0], w2[0]))
    past = (cache_k[0].transpose(0, 2, 3, 4, 1).reshape(dec_batch, D_MODEL, past_len),
            cache_v[0].reshape(dec_batch, past_len * N_HEADS, V_HEAD_DIM),
            state_conv[0], state_rnn[0].reshape(dec_batch, 1, D_RNN))
    (ys, ks, vs, cs, hs), _ = _stream(x_sample, past, w, experts_b)
    return (yp, ys, kp, vp, cp, hp, ks, vs, cs, hs)
```

```python
import functools
import math

import jax
import jax.numpy as jnp
from jax import lax
from jax.experimental import pallas as pl
from jax.experimental.pallas import tpu as pltpu

F32 = jnp.float32
BF16 = jnp.bfloat16

D_MODEL = 1024
CHUNK = 64
N_HEADS = 8
QK_HEAD_DIM = 64
V_HEAD_DIM = 128
HEAD_COLS = 2 * QK_HEAD_DIM
D_RNN = 1024
N_RNN_BLOCKS = 8
RNN_BLOCK = D_RNN // N_RNN_BLOCKS
CONV_WIDTH = 4
RG_C = 8.0
N_GROUPS = 4
EXPERTS_PER_GROUP = 4
N_EXPERTS = N_GROUPS * EXPERTS_PER_GROUP
D_EXPERT = 512
EPS = 1e-6
LAMBDA_INIT = 0.8 - 0.6 * math.exp(-0.3 * 0)
Q_SCALE = QK_HEAD_DIM ** -0.5 * math.log2(math.e)
N_SEG = 7
IN_WIDTH = N_SEG * D_MODEL
ROUTE_LANES = 128
EXPERTS_PER_STEP = 4
MOE_ROWS = 288
EXPERT_LANE0 = N_GROUPS
SUBLANES = 8
LANES = 128
HEADS_PER_STEP = 8
MERGE_PART_ROWS = 256
VMEM_LIMIT = 56 * 1024 * 1024
NEG = float(jnp.finfo(jnp.float32).min)


def _params(*sem):
    return pltpu.CompilerParams(dimension_semantics=sem, vmem_limit_bytes=VMEM_LIMIT)


def _rms(x, g):
    return x * lax.rsqrt(jnp.mean(x * x, axis=-1, keepdims=True) + EPS) * g


def _sigmoid(x):
    return 0.5 * jnp.tanh(0.5 * x) + 0.5


def _in_proj_kernel(x_ref, g_ref, w_ref, q_ref, k_ref, v_ref, kb_ref, vb_ref, xr_ref, gr_ref, gt_ref,
                    *, k_transposed):
    xn = _rms(x_ref[...], g_ref[...]).astype(BF16)

    def seg(j):
        return jnp.dot(xn, w_ref[:, j * D_MODEL:(j + 1) * D_MODEL], preferred_element_type=F32)

    q_ref[...] = (seg(0) * Q_SCALE).astype(BF16)
    if k_transposed:
        kt = seg(1).T
        k_ref[0] = kt
        kb_ref[0, 0] = kt.astype(BF16)
    else:
        k = seg(1)
        k_ref[...] = k
        kb_ref[...] = k.astype(BF16)
    v = seg(2)
    v_ref[...] = v
    vb_ref[...] = v.astype(BF16)
    xr_ref[...] = seg(3)
    gr_ref[...] = seg(4)
    gt_ref[:, :D_MODEL] = seg(5)
    gt_ref[:, D_MODEL:] = seg(6)


def _in_proj(x2d, g1, w_in_b, tm, seq=None):
    n = x2d.shape[0]
    row = lambda i: (i, 0)
    const = lambda i: (0, 0)
    wide = lambda dt: jax.ShapeDtypeStruct((n, D_MODEL), dt)
    wide_spec = pl.BlockSpec((tm, D_MODEL), row)
    if seq is None:
        k_shape, k_spec, kb_shape, kb_spec = wide(F32), wide_spec, wide(BF16), wide_spec
    else:
        nt = seq // tm
        k_shape = jax.ShapeDtypeStruct((n // seq, D_MODEL, seq), F32)
        k_spec = pl.BlockSpec((1, D_MODEL, tm), lambda i: (i // nt, 0, i % nt))
        kb_shape = jax.ShapeDtypeStruct((n // seq, nt, D_MODEL, tm), BF16)
        kb_spec = pl.BlockSpec((1, 1, D_MODEL, tm), lambda i: (i // nt, i % nt, 0, 0))
    return pl.pallas_call(
        functools.partial(_in_proj_kernel, k_transposed=seq is not None),
        grid=(n // tm,),
        in_specs=[wide_spec,
                  pl.BlockSpec((1, D_MODEL), const),
                  pl.BlockSpec((D_MODEL, IN_WIDTH), const, pipeline_mode=pl.Buffered(1))],
        out_specs=[wide_spec, k_spec, wide_spec, kb_spec, wide_spec, wide_spec, wide_spec,
                   pl.BlockSpec((tm, 2 * D_MODEL), row)],
        out_shape=[wide(BF16), k_shape, wide(F32), kb_shape, wide(BF16), wide(F32), wide(F32),
                   jax.ShapeDtypeStruct((n, 2 * D_MODEL), F32)],
        compiler_params=_params("parallel"),
    )(x2d, g1, w_in_b)


def _lam(lq1, lk1, lq2, lk2):
    return (jnp.exp(jnp.sum(lq1[...] * lk1[...], axis=-1, keepdims=True))
            - jnp.exp(jnp.sum(lq2[...] * lk2[...], axis=-1, keepdims=True)) + LAMBDA_INIT)


def _split_q(q):
    lane = lax.broadcasted_iota(jnp.int32, q.shape, 1)
    zero = jnp.zeros_like(q)
    return jnp.where(lane < QK_HEAD_DIM, q, zero), jnp.where(lane >= QK_HEAD_DIM, q, zero)


def _scores(qm, k):
    return lax.dot_general(qm, k, (((1,), (1,)), ((), ())), preferred_element_type=F32)


def _subln(o, g):
    return (_rms(o, g) * (1.0 - LAMBDA_INIT)).astype(BF16)


def _attn_prompt_kernel(lq1, lk1, lq2, lk2, sg_ref, q_ref, k_ref, v_ref, o_ref, m_ref, acc_ref, *, tq, kw):
    i = pl.program_id(2)
    ones = jnp.ones((tq, LANES), BF16)
    qq = []
    for h in range(HEADS_PER_STEP):
        q1, q2 = _split_q(q_ref[:, h * HEAD_COLS:(h + 1) * HEAD_COLS])
        qq.append(jnp.concatenate([q1, q2], axis=0))

    def row_max(chunks):
        part = chunks[0]
        for x in chunks[1:]:
            part = jnp.maximum(part, x)
        return jnp.max(part, axis=1, keepdims=True)

    def split(s):
        return [s[:, c * LANES:(c + 1) * LANES] for c in range(s.shape[1] // LANES)]

    def head_cols(h, width):
        return slice(h * width, (h + 1) * width)

    def diagonal_tile(h):
        row = lax.broadcasted_iota(jnp.int32, (2 * kw, LANES), 0)
        lane = lax.broadcasted_iota(jnp.int32, (2 * kw, LANES), 1)
        qchunk = (row % kw) // CHUNK
        for r in range(tq // kw):
            rows1, rows2 = slice(r * kw, (r + 1) * kw), slice(tq + r * kw, tq + (r + 1) * kw)
            qr = jnp.concatenate([qq[h][rows1], qq[h][rows2]], axis=0)
            chunks = []
            for j in range(r + 1):
                s = split(jnp.dot(qr, k_ref[0, i * (tq // kw) + j, head_cols(h, HEAD_COLS), :],
                                  preferred_element_type=F32))
                if j == r:
                    s = [jnp.where((lane + c * LANES) // CHUNK <= qchunk, x, NEG) for c, x in enumerate(s)]
                chunks += s
            m_new = jnp.broadcast_to(row_max(chunks), (2 * kw, LANES))
            p = jnp.concatenate([jnp.exp2(x - m_new).astype(BF16) for x in chunks], axis=1)
            keys = (r + 1) * kw
            start = pl.multiple_of(i * tq, tq)
            v1 = jnp.concatenate([v_ref[pl.ds(start, keys), head_cols(h, V_HEAD_DIM)],
                                  jnp.ones((keys, LANES), BF16)], axis=1)
            pv = jnp.dot(p, v1, preferred_element_type=F32)
            for rows, part in ((rows1, slice(0, kw)), (rows2, slice(kw, 2 * kw))):
                acc_ref[h, rows, :] = pv[part]
                m_ref[h, rows, :] = m_new[part]

    def tile(kt, h):
        start = pl.multiple_of(kt * tq, tq)
        chunks = []
        for j in range(tq // kw):
            chunks += split(jnp.dot(qq[h], k_ref[0, kt * (tq // kw) + j, head_cols(h, HEAD_COLS), :],
                                    preferred_element_type=F32))
        m_next = jnp.maximum(m_ref[h], row_max(chunks))
        p = jnp.concatenate([jnp.exp2(x - m_next).astype(BF16) for x in chunks], axis=1)
        v1 = jnp.concatenate([v_ref[pl.ds(start, tq), head_cols(h, V_HEAD_DIM)], ones], axis=1)
        alpha = jnp.exp2(m_ref[h] - m_next)
        acc_ref[h] = (jnp.concatenate([alpha, alpha], axis=1) * acc_ref[h]
                      + jnp.dot(p, v1, preferred_element_type=F32))
        m_ref[h] = m_next

    def body(kt, c):
        for h in range(HEADS_PER_STEP):
            tile(kt, h)
        return c

    for h in range(HEADS_PER_STEP):
        diagonal_tile(h)
    lax.fori_loop(0, i, body, 0)
    lam = _lam(lq1, lk1, lq2, lk2)
    for h in range(HEADS_PER_STEP):
        o1 = acc_ref[h, :tq, :V_HEAD_DIM] / acc_ref[h, :tq, V_HEAD_DIM:]
        o2 = acc_ref[h, tq:, :V_HEAD_DIM] / acc_ref[h, tq:, V_HEAD_DIM:]
        o_ref[:, h * V_HEAD_DIM:(h + 1) * V_HEAD_DIM] = _subln(o1 - lam * o2, sg_ref[...])


def _attn_prompt(q, kbt, vb, lams, subln_g, batch, seq, tq):
    n = q.shape[0]
    nq = seq // tq
    nkb, kw = kbt.shape[1], kbt.shape[3]
    small = lambda b, h, i: (0, 0)
    qmap = lambda b, h, i: (b * nq + i, h)
    kvmap = lambda b, h, i: (b, h)
    return pl.pallas_call(
        functools.partial(_attn_prompt_kernel, tq=tq, kw=kw),
        grid=(batch, N_HEADS // HEADS_PER_STEP, nq),
        in_specs=[pl.BlockSpec((1, QK_HEAD_DIM), small)] * 4 + [
            pl.BlockSpec((1, V_HEAD_DIM), small),
            pl.BlockSpec((tq, HEADS_PER_STEP * HEAD_COLS), qmap),
            pl.BlockSpec((1, nkb, HEADS_PER_STEP * HEAD_COLS, kw), lambda b, h, i: (b, 0, h, 0)),
            pl.BlockSpec((seq, HEADS_PER_STEP * V_HEAD_DIM), kvmap)],
        out_specs=pl.BlockSpec((tq, HEADS_PER_STEP * V_HEAD_DIM), qmap),
        out_shape=jax.ShapeDtypeStruct((n, N_HEADS * V_HEAD_DIM), BF16),
        scratch_shapes=[pltpu.VMEM((HEADS_PER_STEP, 2 * tq, LANES), F32),
                        pltpu.VMEM((HEADS_PER_STEP, 2 * tq, 2 * LANES), F32)],
        compiler_params=_params("parallel", "parallel", "arbitrary"),
    )(*lams, subln_g, q, kbt, vb)


def _attn_sample_kernel(lq1, lk1, lq2, lk2, sg_ref, q_ref, kn_ref, vn_ref, ck_ref, cv_ref, o_ref, *, past):
    lam = _lam(lq1, lk1, lq2, lk2)
    for h in range(N_HEADS):
        cols = slice(h * HEAD_COLS, (h + 1) * HEAD_COLS)
        qq = jnp.concatenate(_split_q(q_ref[:, cols]), axis=0)
        kct = ck_ref[0, h * HEAD_COLS:(h + 1) * HEAD_COLS, :].astype(BF16)
        vc = cv_ref[0, pl.ds(h, past, stride=N_HEADS), :].astype(BF16)
        sc = jnp.dot(qq, kct, preferred_element_type=F32)
        sn = _scores(qq, kn_ref[:, cols])
        m = jnp.maximum(jnp.max(sc, axis=-1, keepdims=True), jnp.max(sn, axis=-1, keepdims=True))
        pc = jnp.exp2(sc - m)
        pn = jnp.exp2(sn - m)
        l = jnp.sum(pc, axis=-1, keepdims=True) + jnp.sum(pn, axis=-1, keepdims=True)
        acc = (jnp.dot(pc.astype(BF16), vc, preferred_element_type=F32)
               + jnp.dot(pn.astype(BF16), vn_ref[:, cols], preferred_element_type=F32))
        o = acc / l
        t = q_ref.shape[0]
        o_ref[:, cols] = _subln(o[:t] - lam * o[t:], sg_ref[...])


def _attn_sample(q, kb, vb, cache_kt, cache_v, lams, subln_g, batch, t):
    past = cache_kt.shape[2]
    small = lambda b: (0, 0)
    row = lambda b: (b, 0)
    cache = lambda b: (b, 0, 0)
    return pl.pallas_call(
        functools.partial(_attn_sample_kernel, past=past),
        grid=(batch,),
        in_specs=[pl.BlockSpec((1, QK_HEAD_DIM), small)] * 4 + [
            pl.BlockSpec((1, V_HEAD_DIM), small),
            pl.BlockSpec((t, D_MODEL), row),
            pl.BlockSpec((t, D_MODEL), row),
            pl.BlockSpec((t, D_MODEL), row),
            pl.BlockSpec((1, D_MODEL, past), cache),
            pl.BlockSpec((1, past * N_HEADS, V_HEAD_DIM), cache)],
        out_specs=pl.BlockSpec((t, D_MODEL), row),
        out_shape=jax.ShapeDtypeStruct((batch * t, D_MODEL), BF16),
        compiler_params=_params("parallel"),
    )(*lams, subln_g, q, kb, vb, cache_kt, cache_v)


def _rglru_kernel(xr_ref, gr_ref, cs_ref, h0_ref, cw_ref, cb_ref, wa_ref, ba_ref, wi_ref, bi_ref, lam_ref,
                  y_ref, nc_ref, hl_ref, xpad, hcar, a_s, b_s, h_s, *, tt):
    t = pl.program_id(1)
    pad = SUBLANES

    @pl.when(t == 0)
    def _():
        xpad[0:pad, :] = jnp.zeros((pad, D_RNN), F32)
        xpad[pad - (CONV_WIDTH - 1):pad, :] = cs_ref[0]
        hcar[...] = h0_ref[0]

    xpad[pad:pad + tt, :] = xr_ref[...]
    xc = cb_ref[...]
    for j in range(CONV_WIDTH):
        off = pad - (CONV_WIDTH - 1) + j
        xc = xc + xpad[off:off + tt, :] * cw_ref[j:j + 1, :]
    xcb = xc.astype(BF16)

    z = -lam_ref[...]
    softplus = jnp.maximum(z, 0.0) + jnp.log1p(jnp.exp(-jnp.abs(z)))
    for n in range(N_RNN_BLOCKS):
        cols = slice(n * RNN_BLOCK, (n + 1) * RNN_BLOCK)
        xs = xcb[:, cols]
        r = _sigmoid(jnp.dot(xs, wa_ref[n], preferred_element_type=F32) + ba_ref[:, cols])
        i = _sigmoid(jnp.dot(xs, wi_ref[n], preferred_element_type=F32) + bi_ref[:, cols])
        log_a = -RG_C * r * softplus[:, cols]
        a_s[:, cols] = jnp.exp(log_a)
        th = jnp.tanh(log_a)
        b_s[:, cols] = jnp.sqrt(-2.0 * th / (1.0 - th)) * (i * xc[:, cols])

    h = hcar[...]
    for s in range(tt):
        h = a_s[s:s + 1, :] * h + b_s[s:s + 1, :]
        h_s[s:s + 1, :] = h
    hcar[...] = h
    y_ref[...] = (h_s[...] * jax.nn.gelu(gr_ref[...])).astype(BF16)
    nc_ref[0] = xpad[pad + tt - (CONV_WIDTH - 1):pad + tt, :]
    hl_ref[0] = h
    xpad[0:pad, :] = xpad[tt:tt + pad, :]


def _rglru(xr, gr, conv_state, h0, conv_w, conv_b, w_a_b, b_a, w_i_b, b_i, rg_lambda, batch, t, tt):
    nt = t // tt
    row = lambda b, s: (b * nt + s, 0)
    const2 = lambda b, s: (0, 0)
    const3 = lambda b, s: (0, 0, 0)
    per_b = lambda b, s: (b, 0, 0)
    vec = pl.BlockSpec((1, D_RNN), const2)
    blk = pl.BlockSpec((N_RNN_BLOCKS, RNN_BLOCK, RNN_BLOCK), const3)
    big = pltpu.VMEM((tt, D_RNN), F32)
    return pl.pallas_call(
        functools.partial(_rglru_kernel, tt=tt),
        grid=(batch, nt),
        in_specs=[pl.BlockSpec((tt, D_RNN), row), pl.BlockSpec((tt, D_RNN), row),
                  pl.BlockSpec((1, CONV_WIDTH - 1, D_RNN), per_b), pl.BlockSpec((1, 1, D_RNN), per_b),
                  pl.BlockSpec((CONV_WIDTH, D_RNN), const2), vec, blk, vec, blk, vec, vec],
        out_specs=[pl.BlockSpec((tt, D_RNN), row),
                   pl.BlockSpec((1, CONV_WIDTH - 1, D_RNN), per_b),
                   pl.BlockSpec((1, 1, D_RNN), per_b)],
        out_shape=[jax.ShapeDtypeStruct((batch * t, D_RNN), BF16),
                   jax.ShapeDtypeStruct((batch, CONV_WIDTH - 1, D_RNN), F32),
                   jax.ShapeDtypeStruct((batch, 1, D_RNN), F32)],
        scratch_shapes=[pltpu.VMEM((tt + SUBLANES, D_RNN), F32), pltpu.VMEM((1, D_RNN), F32), big, big, big],
        compiler_params=_params("parallel", "arbitrary"),
    )(xr, gr, conv_state, h0, conv_w, conv_b, w_a_b, b_a, w_i_b, b_i, rg_lambda)


def _first_lane_of_max(vals, valid, lane):
    masked = jnp.where(valid, vals, -jnp.inf)
    mx = jnp.max(masked, axis=-1, keepdims=True)
    idx = jnp.min(jnp.where(valid & (masked == mx), lane, ROUTE_LANES), axis=-1, keepdims=True)
    return mx, idx


def _route(logits):
    lane = lax.broadcasted_iota(jnp.int32, logits.shape, 1)
    is_group = lane < N_GROUPS
    gmax, gsel = _first_lane_of_max(logits, is_group, lane)
    gsum = jnp.sum(jnp.where(is_group, jnp.exp(logits - gmax), 0.0), axis=-1, keepdims=True)
    g_w = 1.0 / gsum
    expert = lane - EXPERT_LANE0
    in_group = (expert >= 0) & (expert < N_EXPERTS) & ((expert // EXPERTS_PER_GROUP) == gsel)
    v1, i1 = _first_lane_of_max(logits, in_group, lane)
    v2, i2 = _first_lane_of_max(logits, in_group & (lane != i1), lane)
    e2 = jnp.exp(v2 - v1)
    den = 1.0 + e2
    return jnp.where(lane == i1, (1.0 / den) * g_w, 0.0) + jnp.where(lane == i2, (e2 / den) * g_w, 0.0)


def _merge_kernel(at_ref, y_ref, gt_ref, x_ref, wa_ref, wr_ref, wo_ref, bg_ref, g2_ref, wc_ref, bc_ref, *refs,
                  parts, n_cast):
    x1_ref, xn_ref, cm_ref = refs[n_cast:n_cast + 3]
    for src, dst in zip(refs[:n_cast], refs[n_cast + 3:]):
        dst[...] = src[...].astype(BF16)
    rows_per_part = x_ref.shape[0] // parts
    for part in range(parts):
        rows = slice(part * rows_per_part, (part + 1) * rows_per_part)
        pa = jnp.dot(at_ref[rows, :], wa_ref[...], preferred_element_type=F32)
        pr = jnp.dot(y_ref[rows, :], wr_ref[...], preferred_element_type=F32)
        g = _sigmoid(gt_ref[rows, :] + bg_ref[...])
        merged = g[:, :D_MODEL] * pa + g[:, D_MODEL:] * pr
        x1 = x_ref[rows, :] + jnp.dot(merged.astype(BF16), wo_ref[...], preferred_element_type=F32)
        x1_ref[rows, :] = x1
        xn = _rms(x1, g2_ref[...]).astype(BF16)
        xn_ref[rows, :] = xn
        cm_ref[rows, :] = _route(jnp.dot(xn, wc_ref[...], preferred_element_type=F32) + bc_ref[...])


def _merge(attn, y_rnn, gates, x2d, wa_b, wr_b, wo_b, b_gate, g2, w_cat_b, b_cat, tm, cast=()):
    n = x2d.shape[0]
    steps = n // tm
    row = lambda i: (i, 0)
    const = lambda i: (0, 0)
    sq = pl.BlockSpec((D_MODEL, D_MODEL), const)
    slabs = [a.reshape(steps, -1, a.shape[-1]) for a in cast]
    slab_specs = [pl.BlockSpec((1,) + s.shape[1:], lambda i: (i, 0, 0)) for s in slabs]
    outs = pl.pallas_call(
        functools.partial(_merge_kernel, parts=max(1, tm // MERGE_PART_ROWS), n_cast=len(cast)),
        grid=(steps,),
        in_specs=[pl.BlockSpec((tm, D_MODEL), row), pl.BlockSpec((tm, D_MODEL), row),
                  pl.BlockSpec((tm, 2 * D_MODEL), row), pl.BlockSpec((tm, D_MODEL), row),
                  sq, sq, sq, pl.BlockSpec((1, 2 * D_MODEL), const), pl.BlockSpec((1, D_MODEL), const),
                  pl.BlockSpec((D_MODEL, ROUTE_LANES), const), pl.BlockSpec((1, ROUTE_LANES), const)] + slab_specs,
        out_specs=[pl.BlockSpec((tm, D_MODEL), row), pl.BlockSpec((tm, D_MODEL), row),
                   pl.BlockSpec((tm, ROUTE_LANES), row)] + slab_specs,
        out_shape=[jax.ShapeDtypeStruct((n, D_MODEL), F32), jax.ShapeDtypeStruct((n, D_MODEL), BF16),
                   jax.ShapeDtypeStruct((n, ROUTE_LANES), F32)]
        + [jax.ShapeDtypeStruct(s.shape, BF16) for s in slabs],
        compiler_params=_params("parallel"),
    )(attn, y_rnn, gates, x2d, wa_b, wr_b, wo_b, b_gate, g2, w_cat_b, b_cat, *slabs)
    return outs[0], outs[1], outs[2], tuple(o.reshape(a.shape) for o, a in zip(outs[3:], cast))


def _route_tables(comb, tile, batch):
    n = comb.shape[0]
    seq, chunk = n // batch, tile // batch
    nt = seq // chunk
    c = comb[:, EXPERT_LANE0:EXPERT_LANE0 + N_EXPERTS]
    sel = jnp.any(c.reshape(batch, nt, chunk, N_GROUPS, EXPERTS_PER_GROUP) != 0.0, axis=-1)
    sel = sel.transpose(1, 0, 2, 3).reshape(nt, tile, N_GROUPS)
    csum = jnp.cumsum(sel.astype(jnp.int32), axis=1)
    rank = jnp.where(sel, csum - 1, -1).astype(F32)
    counts = csum[:, -1, :].reshape(nt * N_GROUPS)
    rank_t = rank.transpose(0, 2, 1).reshape(nt * N_GROUPS, 1, tile)
    rank = rank.reshape(nt, batch, chunk, N_GROUPS).transpose(1, 0, 2, 3).reshape(batch, seq, N_GROUPS)
    rank = jnp.pad(rank, ((0, 0), (0, 0), (0, ROUTE_LANES - N_GROUPS)), constant_values=-1.0)
    hi = c.astype(BF16)
    rest = c - hi.astype(F32)
    mid = rest.astype(BF16)
    lo = (rest - mid.astype(F32)).astype(BF16)
    c3 = jnp.pad(jnp.concatenate([hi, mid, lo], axis=1), ((0, 0), (0, ROUTE_LANES - 3 * N_EXPERTS)))
    return counts, rank, rank_t, c3.reshape(batch, seq, ROUTE_LANES)


def _moe_kernel(cnt_ref, x_ref, c3_ref, rk_ref, rkt_ref, x1_ref, gf_ref, *refs, tile):
    flat = lambda ref: ref[...].reshape(tile, ref.shape[-1])
    w1_refs, w3_refs, w2_refs = (refs[k * EXPERTS_PER_STEP:(k + 1) * EXPERTS_PER_STEP] for k in range(3))
    y_ref, xg, cg, yg = refs[3 * EXPERTS_PER_STEP:]
    t = pl.program_id(0)
    step = pl.program_id(1)
    steps_per_group = EXPERTS_PER_GROUP // EXPERTS_PER_STEP
    g = step // steps_per_group
    member = step % steps_per_group
    count = cnt_ref[t * N_GROUPS + g]
    nblk = (count + MOE_ROWS - 1) // MOE_ROWS

    @pl.when(step == 0)
    def _():
        y_ref[...] = jnp.zeros(y_ref.shape, F32)

    @pl.when(member == 0)
    def _():
        rank_row = rkt_ref[0]

        def gather(b, carry):
            slot = lax.broadcasted_iota(jnp.int32, (MOE_ROWS, tile), 0).astype(F32) + (b * MOE_ROWS).astype(F32)
            onehot = jnp.where(rank_row == slot, 1.0, 0.0).astype(BF16)
            xg[b] = jnp.dot(onehot, flat(x_ref), preferred_element_type=F32).astype(BF16)
            cg[b] = jnp.dot(onehot, flat(c3_ref), preferred_element_type=F32)
            yg[b] = jnp.zeros((MOE_ROWS, D_MODEL), F32)
            return carry

        lax.fori_loop(0, nblk, gather, 0)

    lane_r = lax.broadcasted_iota(jnp.int32, (MOE_ROWS, ROUTE_LANES), 1)

    def experts(b, carry):
        xc = xg[b]
        out = yg[b]
        for j in range(EXPERTS_PER_STEP):
            mine = (lane_r % N_EXPERTS == step * EXPERTS_PER_STEP + j) & (lane_r < 3 * N_EXPERTS)
            cw = jnp.sum(jnp.where(mine, cg[b], 0.0), axis=1, keepdims=True)
            a = jnp.dot(xc, w1_refs[j][0], preferred_element_type=F32)
            h = (a * _sigmoid(a)) * jnp.dot(xc, w3_refs[j][0], preferred_element_type=F32)
            out = out + cw * jnp.dot(h.astype(BF16), w2_refs[j][0], preferred_element_type=F32)
        yg[b] = out
        return carry

    lax.fori_loop(0, nblk, experts, 0)

    @pl.when(member == steps_per_group - 1)
    def _():
        lane = lax.broadcasted_iota(jnp.int32, (tile, ROUTE_LANES), 1)
        rank_col = jnp.sum(jnp.where(lane == g, flat(rk_ref), 0.0), axis=1, keepdims=True)

        def scatter(b, carry):
            slot_t = lax.broadcasted_iota(jnp.int32, (tile, MOE_ROWS), 1).astype(F32) + (b * MOE_ROWS).astype(F32)
            onehot_t = jnp.where(rank_col == slot_t, 1.0, 0.0).astype(BF16)
            y_ref[...] += jnp.dot(onehot_t, yg[b].astype(BF16), preferred_element_type=F32).reshape(y_ref.shape)
            return carry

        lax.fori_loop(0, nblk, scatter, 0)

    @pl.when(step == pl.num_programs(1) - 1)
    def _():
        y_ref[...] = _rms(x1_ref[...] + y_ref[...], gf_ref[...])


def _moe(xn2, comb, x1, gf, w1_b, w3_b, w2_b, tile, batch):
    n = xn2.shape[0]
    seq, chunk = n // batch, tile // batch
    counts, rank, rank_t, c3 = _route_tables(comb, tile, batch)
    max_blocks = -(-tile // MOE_ROWS)
    row = lambda t, s, cnt: (0, t, 0)
    rows = lambda width: pl.BlockSpec((batch, chunk, width), row)
    expert_specs = lambda shape: [pl.BlockSpec((1,) + shape, lambda t, s, cnt, j=j: (s * EXPERTS_PER_STEP + j, 0, 0))
                                  for j in range(EXPERTS_PER_STEP)]
    steps_per_group = EXPERTS_PER_GROUP // EXPERTS_PER_STEP
    return pl.pallas_call(
        functools.partial(_moe_kernel, tile=tile),
        grid_spec=pltpu.PrefetchScalarGridSpec(
            num_scalar_prefetch=1,
            grid=(n // tile, N_EXPERTS // EXPERTS_PER_STEP),
            in_specs=[rows(D_MODEL), rows(ROUTE_LANES), rows(ROUTE_LANES),
                      pl.BlockSpec((1, 1, tile), lambda t, s, cnt: (t * N_GROUPS + s // steps_per_group, 0, 0)),
                      rows(D_MODEL), pl.BlockSpec((1, D_MODEL), lambda t, s, cnt: (0, 0))]
            + expert_specs((D_MODEL, D_EXPERT)) + expert_specs((D_MODEL, D_EXPERT))
            + expert_specs((D_EXPERT, D_MODEL)),
            out_specs=rows(D_MODEL),
            scratch_shapes=[pltpu.VMEM((max_blocks, MOE_ROWS, D_MODEL), BF16),
                            pltpu.VMEM((max_blocks, MOE_ROWS, ROUTE_LANES), F32),
                            pltpu.VMEM((max_blocks, MOE_ROWS, D_MODEL), F32)]),
        out_shape=jax.ShapeDtypeStruct((batch, seq, D_MODEL), F32),
        compiler_params=_params("parallel", "arbitrary"),
    )(counts, xn2.reshape(batch, seq, D_MODEL), c3, rank, rank_t, x1.reshape(batch, seq, D_MODEL), gf, *([w1_b] * EXPERTS_PER_STEP + [w3_b] * EXPERTS_PER_STEP
                                                + [w2_b] * EXPERTS_PER_STEP))


def _tiles(batch, t):
    n = batch * t
    return dict(tm_in=min(256, n),
                tm_merge=min(2 * MERGE_PART_ROWS, n),
                tm_moe=min(1024, n),
                tq=min(512, t),
                tt=min(256, t))


def _stream(x, past, w, experts):
    batch, t, _ = x.shape
    n = batch * t
    tm_in, tm_merge, tm_moe, tq, tt = (_tiles(batch, t)[k] for k in ("tm_in", "tm_merge", "tm_moe", "tq", "tt"))
    x2d = x.reshape(n, D_MODEL)
    q, k, v, kb, vb, xr, gr, gates = _in_proj(x2d, w["g1"], w["w_in"], tm_in,
                                              seq=t if past is None else None)
    if past is None:
        attn = _attn_prompt(q, kb, vb, w["lams"], w["subln_g"], batch, t, tq)
        k = k.reshape(batch, N_HEADS, 2, QK_HEAD_DIM, t).transpose(0, 4, 1, 2, 3)
        conv_state = jnp.zeros((batch, CONV_WIDTH - 1, D_RNN), F32)
        h0 = jnp.zeros((batch, 1, D_RNN), F32)
    else:
        cache_k, cache_v, conv_state, h0 = past
        attn = _attn_sample(q, kb, vb, cache_k, cache_v, w["lams"], w["subln_g"], batch, t)
    y_rnn, new_conv, h_last = _rglru(xr, gr, conv_state, h0, w["conv_w"], w["conv_b"], w["w_a"], w["b_a"],
                                     w["w_i"], w["b_i"], w["rg_lambda"], batch, t, tt)
    to_cast = experts if experts[0].dtype != BF16 else ()
    x1, xn2, comb, cast = _merge(attn, y_rnn, gates, x2d, w["w_attn"], w["w_rnn"], w["w_out"], w["b_gate"],
                                 w["g2"], w["w_cat"], w["b_cat"], tm_merge, cast=to_cast)
    experts = cast or experts
    y = _moe(xn2, comb, x1, w["gf"], *experts, tm_moe, batch)
    return (y,
            k.reshape(1, batch, t, N_HEADS, 2, QK_HEAD_DIM),
            v.reshape(1, batch, t, N_HEADS, V_HEAD_DIM),
            new_conv.reshape(1, batch, CONV_WIDTH - 1, D_RNN),
            h_last.reshape(1, batch, D_RNN)), experts


def kernel(x_prompt, x_sample, cache_k, cache_v, state_conv, state_rnn, norm1_g, w_in, lambda_q1, lambda_k1, lambda_q2, lambda_k2, subln_g, w_attn_proj, conv_w, conv_b, w_rg_a, b_rg_a, w_rg_i, b_rg_i, rg_lambda, w_rnn_proj, b_gate, w_out, norm2_g, w_group, b_group, w_router, b_router, w1, w3, w2, final_norm_g):
    assert norm1_g.shape[0] == 1, "single-layer model"
    dec_batch, past_len = cache_k.shape[1], cache_k.shape[2]
    pad = ROUTE_LANES - N_GROUPS - N_EXPERTS
    w = {
        "g1": norm1_g, "g2": norm2_g, "gf": final_norm_g.reshape(1, D_MODEL),
        "w_in": w_in[0].astype(BF16),
        "lams": (lambda_q1, lambda_k1, lambda_q2, lambda_k2),
        "subln_g": subln_g,
        "w_attn": w_attn_proj[0].astype(BF16), "w_rnn": w_rnn_proj[0].astype(BF16),
        "w_out": w_out[0].astype(BF16),
        "conv_w": conv_w[0], "conv_b": conv_b,
        "w_a": w_rg_a[0].astype(BF16), "b_a": b_rg_a, "w_i": w_rg_i[0].astype(BF16), "b_i": b_rg_i,
        "rg_lambda": rg_lambda, "b_gate": b_gate,
        "w_cat": jnp.pad(jnp.concatenate([w_group[0], w_router[0]], axis=1), ((0, 0), (0, pad))).astype(BF16),
        "b_cat": jnp.pad(jnp.concatenate([b_group[0], b_router[0]]), (0, pad)).reshape(1, ROUTE_LANES),
    }
    (yp, kp, vp, cp, hp), experts_b = _stream(x_prompt, None, w, (w1[0], w3[0], w2[0]))
    past = (cache_k[0].transpose(0, 2, 3, 4, 1).reshape(dec_batch, D_MODEL, past_len),
            cache_v[0].reshape(dec_batch, past_len * N_HEADS, V_HEAD_DIM),
            state_conv[0], state_rnn[0].reshape(dec_batch, 1, D_RNN))
    x_sample, yp = lax.optimization_barrier((x_sample, yp))
    (ys, ks, vs, cs, hs), _ = _stream(x_sample, past, w, experts_b)
    return (yp, ys, kp, vp, cp, hp, ks, vs, cs, hs)
```

```python
import functools
import math

import jax
import jax.numpy as jnp
from jax import lax
from jax.experimental import pallas as pl
from jax.experimental.pallas import tpu as pltpu

F32 = jnp.float32
BF16 = jnp.bfloat16

D_MODEL = 1024
CHUNK = 64
N_HEADS = 8
QK_HEAD_DIM = 64
V_HEAD_DIM = 128
HEAD_COLS = 2 * QK_HEAD_DIM
D_RNN = 1024
N_RNN_BLOCKS = 8
RNN_BLOCK = D_RNN // N_RNN_BLOCKS
CONV_WIDTH = 4
RG_C = 8.0
N_GROUPS = 4
EXPERTS_PER_GROUP = 4
N_EXPERTS = N_GROUPS * EXPERTS_PER_GROUP
D_EXPERT = 512
EPS = 1e-6
LAMBDA_INIT = 0.8 - 0.6 * math.exp(-0.3 * 0)
Q_SCALE = QK_HEAD_DIM ** -0.5 * math.log2(math.e)
N_SEG = 7
IN_WIDTH = N_SEG * D_MODEL
ROUTE_LANES = 128
EXPERTS_PER_STEP = 4
MOE_ROWS = 288
EXPERT_LANE0 = N_GROUPS
SUBLANES = 8
LANES = 128
HEADS_PER_STEP = 8
MERGE_PART_ROWS = 256
VMEM_LIMIT = 56 * 1024 * 1024
NEG = float(jnp.finfo(jnp.float32).min)


def _params(*sem):
    return pltpu.CompilerParams(dimension_semantics=sem, vmem_limit_bytes=VMEM_LIMIT)


def _rms(x, g):
    return x * lax.rsqrt(jnp.mean(x * x, axis=-1, keepdims=True) + EPS) * g


def _sigmoid(x):
    return 0.5 * jnp.tanh(0.5 * x) + 0.5


def _in_proj_kernel(x_ref, g_ref, w_ref, q_ref, k_ref, v_ref, kb_ref, vb_ref, xr_ref, gr_ref, gt_ref,
                    *, k_transposed):
    xn = _rms(x_ref[...], g_ref[...]).astype(BF16)

    def seg(j):
        return jnp.dot(xn, w_ref[:, j * D_MODEL:(j + 1) * D_MODEL], preferred_element_type=F32)

    q_ref[...] = (seg(0) * Q_SCALE).astype(BF16)
    if k_transposed:
        kt = seg(1).T
        k_ref[0] = kt
        kb_ref[0, 0] = kt.astype(BF16)
    else:
        k = seg(1)
        k_ref[...] = k
        kb_ref[...] = k.astype(BF16)
    v = seg(2)
    v_ref[...] = v
    vb_ref[...] = v.astype(BF16)
    xr_ref[...] = seg(3)
    gr_ref[...] = seg(4)
    gt_ref[:, :D_MODEL] = seg(5)
    gt_ref[:, D_MODEL:] = seg(6)


def _in_proj(x2d, g1, w_in_b, tm, seq=None):
    n = x2d.shape[0]
    row = lambda i: (i, 0)
    const = lambda i: (0, 0)
    wide = lambda dt: jax.ShapeDtypeStruct((n, D_MODEL), dt)
    wide_spec = pl.BlockSpec((tm, D_MODEL), row)
    if seq is None:
        k_shape, k_spec, kb_shape, kb_spec = wide(F32), wide_spec, wide(BF16), wide_spec
    else:
        nt = seq // tm
        k_shape = jax.ShapeDtypeStruct((n // seq, D_MODEL, seq), F32)
        k_spec = pl.BlockSpec((1, D_MODEL, tm), lambda i: (i // nt, 0, i % nt))
        kb_shape = jax.ShapeDtypeStruct((n // seq, nt, D_MODEL, tm), BF16)
        kb_spec = pl.BlockSpec((1, 1, D_MODEL, tm), lambda i: (i // nt, i % nt, 0, 0))
    return pl.pallas_call(
        functools.partial(_in_proj_kernel, k_transposed=seq is not None),
        grid=(n // tm,),
        in_specs=[wide_spec,
                  pl.BlockSpec((1, D_MODEL), const),
                  pl.BlockSpec((D_MODEL, IN_WIDTH), const, pipeline_mode=pl.Buffered(1))],
        out_specs=[wide_spec, k_spec, wide_spec, kb_spec, wide_spec, wide_spec, wide_spec,
                   pl.BlockSpec((tm, 2 * D_MODEL), row)],
        out_shape=[wide(BF16), k_shape, wide(F32), kb_shape, wide(BF16), wide(F32), wide(F32),
                   jax.ShapeDtypeStruct((n, 2 * D_MODEL), F32)],
        compiler_params=_params("parallel"),
    )(x2d, g1, w_in_b)


def _lam(lq1, lk1, lq2, lk2):
    return (jnp.exp(jnp.sum(lq1[...] * lk1[...], axis=-1, keepdims=True))
            - jnp.exp(jnp.sum(lq2[...] * lk2[...], axis=-1, keepdims=True)) + LAMBDA_INIT)


def _split_q(q):
    lane = lax.broadcasted_iota(jnp.int32, q.shape, 1)
    zero = jnp.zeros_like(q)
    return jnp.where(lane < QK_HEAD_DIM, q, zero), jnp.where(lane >= QK_HEAD_DIM, q, zero)


def _scores(qm, k):
    return lax.dot_general(qm, k, (((1,), (1,)), ((), ())), preferred_element_type=F32)


def _subln(o, g):
    return (_rms(o, g) * (1.0 - LAMBDA_INIT)).astype(BF16)


def _attn_prompt_kernel(lq1, lk1, lq2, lk2, sg_ref, q_ref, k_ref, v_ref, o_ref, m_ref, acc_ref, *, tq, kw):
    i = pl.program_id(2)
    ones = jnp.ones((tq, LANES), BF16)
    qq = []
    for h in range(HEADS_PER_STEP):
        q1, q2 = _split_q(q_ref[:, h * HEAD_COLS:(h + 1) * HEAD_COLS])
        qq.append(jnp.concatenate([q1, q2], axis=0))

    def row_max(chunks):
        part = chunks[0]
        for x in chunks[1:]:
            part = jnp.maximum(part, x)
        return jnp.max(part, axis=1, keepdims=True)

    def split(s):
        return [s[:, c * LANES:(c + 1) * LANES] for c in range(s.shape[1] // LANES)]

    def head_cols(h, width):
        return slice(h * width, (h + 1) * width)

    def diagonal_tile(h):
        row = lax.broadcasted_iota(jnp.int32, (2 * kw, LANES), 0)
        lane = lax.broadcasted_iota(jnp.int32, (2 * kw, LANES), 1)
        qchunk = (row % kw) // CHUNK
        for r in range(tq // kw):
            rows1, rows2 = slice(r * kw, (r + 1) * kw), slice(tq + r * kw, tq + (r + 1) * kw)
            qr = jnp.concatenate([qq[h][rows1], qq[h][rows2]], axis=0)
            chunks = []
            for j in range(r + 1):
                s = split(jnp.dot(qr, k_ref[0, i * (tq // kw) + j, head_cols(h, HEAD_COLS), :],
                                  preferred_element_type=F32))
                if j == r:
                    s = [jnp.where((lane + c * LANES) // CHUNK <= qchunk, x, NEG) for c, x in enumerate(s)]
                chunks += s
            m_new = jnp.broadcast_to(row_max(chunks), (2 * kw, LANES))
            p = jnp.concatenate([jnp.exp2(x - m_new).astype(BF16) for x in chunks], axis=1)
            keys = (r + 1) * kw
            start = pl.multiple_of(i * tq, tq)
            v1 = jnp.concatenate([v_ref[pl.ds(start, keys), head_cols(h, V_HEAD_DIM)],
                                  jnp.ones((keys, LANES), BF16)], axis=1)
            pv = jnp.dot(p, v1, preferred_element_type=F32)
            for rows, part in ((rows1, slice(0, kw)), (rows2, slice(kw, 2 * kw))):
                acc_ref[h, rows, :] = pv[part]
                m_ref[h, rows, :] = m_new[part]

    def tile(kt, h):
        start = pl.multiple_of(kt * tq, tq)
        chunks = []
        for j in range(tq // kw):
            chunks += split(jnp.dot(qq[h], k_ref[0, kt * (tq // kw) + j, head_cols(h, HEAD_COLS), :],
                                    preferred_element_type=F32))
        m_next = jnp.maximum(m_ref[h], row_max(chunks))
        p = jnp.concatenate([jnp.exp2(x - m_next).astype(BF16) for x in chunks], axis=1)
        v1 = jnp.concatenate([v_ref[pl.ds(start, tq), head_cols(h, V_HEAD_DIM)], ones], axis=1)
        alpha = jnp.exp2(m_ref[h] - m_next)
        acc_ref[h] = (jnp.concatenate([alpha, alpha], axis=1) * acc_ref[h]
                      + jnp.dot(p, v1, preferred_element_type=F32))
        m_ref[h] = m_next

    def body(kt, c):
        for h in range(HEADS_PER_STEP):
            tile(kt, h)
        return c

    for h in range(HEADS_PER_STEP):
        diagonal_tile(h)
    lax.fori_loop(0, i, body, 0)
    lam = _lam(lq1, lk1, lq2, lk2)
    for h in range(HEADS_PER_STEP):
        o1 = acc_ref[h, :tq, :V_HEAD_DIM] / acc_ref[h, :tq, V_HEAD_DIM:]
        o2 = acc_ref[h, tq:, :V_HEAD_DIM] / acc_ref[h, tq:, V_HEAD_DIM:]
        o_ref[:, h * V_HEAD_DIM:(h + 1) * V_HEAD_DIM] = _subln(o1 - lam * o2, sg_ref[...])


def _attn_prompt(q, kbt, vb, lams, subln_g, batch, seq, tq):
    n = q.shape[0]
    nq = seq // tq
    nkb, kw = kbt.shape[1], kbt.shape[3]
    small = lambda b, h, i: (0, 0)
    qmap = lambda b, h, i: (b * nq + i, h)
    kvmap = lambda b, h, i: (b, h)
    return pl.pallas_call(
        functools.partial(_attn_prompt_kernel, tq=tq, kw=kw),
        grid=(batch, N_HEADS // HEADS_PER_STEP, nq),
        in_specs=[pl.BlockSpec((1, QK_HEAD_DIM), small)] * 4 + [
            pl.BlockSpec((1, V_HEAD_DIM), small),
            pl.BlockSpec((tq, HEADS_PER_STEP * HEAD_COLS), qmap),
            pl.BlockSpec((1, nkb, HEADS_PER_STEP * HEAD_COLS, kw), lambda b, h, i: (b, 0, h, 0)),
            pl.BlockSpec((seq, HEADS_PER_STEP * V_HEAD_DIM), kvmap)],
        out_specs=pl.BlockSpec((tq, HEADS_PER_STEP * V_HEAD_DIM), qmap),
        out_shape=jax.ShapeDtypeStruct((n, N_HEADS * V_HEAD_DIM), BF16),
        scratch_shapes=[pltpu.VMEM((HEADS_PER_STEP, 2 * tq, LANES), F32),
                        pltpu.VMEM((HEADS_PER_STEP, 2 * tq, 2 * LANES), F32)],
        compiler_params=_params("parallel", "parallel", "arbitrary"),
    )(*lams, subln_g, q, kbt, vb)


def _attn_sample_kernel(lq1, lk1, lq2, lk2, sg_ref, q_ref, kn_ref, vn_ref, ck_ref, cv_ref, o_ref, *, past):
    lam = _lam(lq1, lk1, lq2, lk2)
    for h in range(N_HEADS):
        cols = slice(h * HEAD_COLS, (h + 1) * HEAD_COLS)
        qq = jnp.concatenate(_split_q(q_ref[:, cols]), axis=0)
        kct = ck_ref[0, h * HEAD_COLS:(h + 1) * HEAD_COLS, :].astype(BF16)
        vc = cv_ref[0, pl.ds(h, past, stride=N_HEADS), :].astype(BF16)
        sc = jnp.dot(qq, kct, preferred_element_type=F32)
        sn = _scores(qq, kn_ref[:, cols])
        m = jnp.maximum(jnp.max(sc, axis=-1, keepdims=True), jnp.max(sn, axis=-1, keepdims=True))
        pc = jnp.exp2(sc - m)
        pn = jnp.exp2(sn - m)
        l = jnp.sum(pc, axis=-1, keepdims=True) + jnp.sum(pn, axis=-1, keepdims=True)
        acc = (jnp.dot(pc.astype(BF16), vc, preferred_element_type=F32)
               + jnp.dot(pn.astype(BF16), vn_ref[:, cols], preferred_element_type=F32))
        o = acc / l
        t = q_ref.shape[0]
        o_ref[:, cols] = _subln(o[:t] - lam * o[t:], sg_ref[...])


def _attn_sample(q, kb, vb, cache_kt, cache_v, lams, subln_g, batch, t):
    past = cache_kt.shape[2]
    small = lambda b: (0, 0)
    row = lambda b: (b, 0)
    cache = lambda b: (b, 0, 0)
    return pl.pallas_call(
        functools.partial(_attn_sample_kernel, past=past),
        grid=(batch,),
        in_specs=[pl.BlockSpec((1, QK_HEAD_DIM), small)] * 4 + [
            pl.BlockSpec((1, V_HEAD_DIM), small),
            pl.BlockSpec((t, D_MODEL), row),
            pl.BlockSpec((t, D_MODEL), row),
            pl.BlockSpec((t, D_MODEL), row),
            pl.BlockSpec((1, D_MODEL, past), cache),
            pl.BlockSpec((1, past * N_HEADS, V_HEAD_DIM), cache)],
        out_specs=pl.BlockSpec((t, D_MODEL), row),
        out_shape=jax.ShapeDtypeStruct((batch * t, D_MODEL), BF16),
        compiler_params=_params("parallel"),
    )(*lams, subln_g, q, kb, vb, cache_kt, cache_v)


def _rglru_kernel(xr_ref, gr_ref, cs_ref, h0_ref, cw_ref, cb_ref, wa_ref, ba_ref, wi_ref, bi_ref, lam_ref,
                  y_ref, nc_ref, hl_ref, xpad, hcar, a_s, b_s, h_s, *, tt):
    t = pl.program_id(1)
    pad = SUBLANES

    @pl.when(t == 0)
    def _():
        xpad[0:pad, :] = jnp.zeros((pad, D_RNN), F32)
        xpad[pad - (CONV_WIDTH - 1):pad, :] = cs_ref[0]
        hcar[...] = h0_ref[0]

    xpad[pad:pad + tt, :] = xr_ref[...]
    xc = cb_ref[...]
    for j in range(CONV_WIDTH):
        off = pad - (CONV_WIDTH - 1) + j
        xc = xc + xpad[off:off + tt, :] * cw_ref[j:j + 1, :]
    xcb = xc.astype(BF16)

    z = -lam_ref[...]
    softplus = jnp.maximum(z, 0.0) + jnp.log1p(jnp.exp(-jnp.abs(z)))
    for n in range(N_RNN_BLOCKS):
        cols = slice(n * RNN_BLOCK, (n + 1) * RNN_BLOCK)
        xs = xcb[:, cols]
        r = _sigmoid(jnp.dot(xs, wa_ref[n], preferred_element_type=F32) + ba_ref[:, cols])
        i = _sigmoid(jnp.dot(xs, wi_ref[n], preferred_element_type=F32) + bi_ref[:, cols])
        log_a = -RG_C * r * softplus[:, cols]
        a_s[:, cols] = jnp.exp(log_a)
        th = jnp.tanh(log_a)
        b_s[:, cols] = jnp.sqrt(-2.0 * th / (1.0 - th)) * (i * xc[:, cols])

    h = hcar[...]
    for s in range(tt):
        h = a_s[s:s + 1, :] * h + b_s[s:s + 1, :]
        h_s[s:s + 1, :] = h
    hcar[...] = h
    y_ref[...] = (h_s[...] * jax.nn.gelu(gr_ref[...])).astype(BF16)
    nc_ref[0] = xpad[pad + tt - (CONV_WIDTH - 1):pad + tt, :]
    hl_ref[0] = h
    xpad[0:pad, :] = xpad[tt:tt + pad, :]


def _rglru(xr, gr, conv_state, h0, conv_w, conv_b, w_a_b, b_a, w_i_b, b_i, rg_lambda, batch, t, tt):
    nt = t // tt
    row = lambda b, s: (b * nt + s, 0)
    const2 = lambda b, s: (0, 0)
    const3 = lambda b, s: (0, 0, 0)
    per_b = lambda b, s: (b, 0, 0)
    vec = pl.BlockSpec((1, D_RNN), const2)
    blk = pl.BlockSpec((N_RNN_BLOCKS, RNN_BLOCK, RNN_BLOCK), const3)
    big = pltpu.VMEM((tt, D_RNN), F32)
    return pl.pallas_call(
        functools.partial(_rglru_kernel, tt=tt),
        grid=(batch, nt),
        in_specs=[pl.BlockSpec((tt, D_RNN), row), pl.BlockSpec((tt, D_RNN), row),
                  pl.BlockSpec((1, CONV_WIDTH - 1, D_RNN), per_b), pl.BlockSpec((1, 1, D_RNN), per_b),
                  pl.BlockSpec((CONV_WIDTH, D_RNN), const2), vec, blk, vec, blk, vec, vec],
        out_specs=[pl.BlockSpec((tt, D_RNN), row),
                   pl.BlockSpec((1, CONV_WIDTH - 1, D_RNN), per_b),
                   pl.BlockSpec((1, 1, D_RNN), per_b)],
        out_shape=[jax.ShapeDtypeStruct((batch * t, D_RNN), BF16),
                   jax.ShapeDtypeStruct((batch, CONV_WIDTH - 1, D_RNN), F32),
                   jax.ShapeDtypeStruct((batch, 1, D_RNN), F32)],
        scratch_shapes=[pltpu.VMEM((tt + SUBLANES, D_RNN), F32), pltpu.VMEM((1, D_RNN), F32), big, big, big],
        compiler_params=_params("parallel", "arbitrary"),
    )(xr, gr, conv_state, h0, conv_w, conv_b, w_a_b, b_a, w_i_b, b_i, rg_lambda)


def _first_lane_of_max(vals, valid, lane):
    masked = jnp.where(valid, vals, -jnp.inf)
    mx = jnp.max(masked, axis=-1, keepdims=True)
    idx = jnp.min(jnp.where(valid & (masked == mx), lane, ROUTE_LANES), axis=-1, keepdims=True)
    return mx, idx


def _route(logits):
    lane = lax.broadcasted_iota(jnp.int32, logits.shape, 1)
    is_group = lane < N_GROUPS
    gmax, gsel = _first_lane_of_max(logits, is_group, lane)
    gsum = jnp.sum(jnp.where(is_group, jnp.exp(logits - gmax), 0.0), axis=-1, keepdims=True)
    g_w = 1.0 / gsum
    expert = lane - EXPERT_LANE0
    in_group = (expert >= 0) & (expert < N_EXPERTS) & ((expert // EXPERTS_PER_GROUP) == gsel)
    v1, i1 = _first_lane_of_max(logits, in_group, lane)
    v2, i2 = _first_lane_of_max(logits, in_group & (lane != i1), lane)
    e2 = jnp.exp(v2 - v1)
    den = 1.0 + e2
    return jnp.where(lane == i1, (1.0 / den) * g_w, 0.0) + jnp.where(lane == i2, (e2 / den) * g_w, 0.0)


def _split_weights(comb):
    hi = comb.astype(BF16)
    rest = comb - hi.astype(F32)
    mid = rest.astype(BF16)
    lo = (rest - mid.astype(F32)).astype(BF16)
    src = lax.broadcasted_iota(jnp.int32, (ROUTE_LANES, ROUTE_LANES), 0)
    dst = lax.broadcasted_iota(jnp.int32, (ROUTE_LANES, ROUTE_LANES), 1)
    out = jnp.zeros(comb.shape, F32)
    for piece, base in ((hi, 0), (mid, N_EXPERTS), (lo, 2 * N_EXPERTS)):
        inside = jnp.where(dst < base + N_EXPERTS, dst - base, -1)
        move = jnp.where(src == inside + EXPERT_LANE0, 1.0, 0.0)
        move = jnp.where(inside >= 0, move, 0.0).astype(BF16)
        out = out + jnp.dot(piece, move, preferred_element_type=F32)
    return out.astype(BF16)


def _merge_kernel(at_ref, y_ref, gt_ref, x_ref, wa_ref, wr_ref, wo_ref, bg_ref, g2_ref, wc_ref, bc_ref, *refs,
                  parts, n_cast):
    x1_ref, xn_ref, cm_ref, c3_ref = refs[n_cast:n_cast + 4]
    for src, dst in zip(refs[:n_cast], refs[n_cast + 4:]):
        dst[...] = src[...].astype(BF16)
    rows_per_part = x_ref.shape[0] // parts
    for part in range(parts):
        rows = slice(part * rows_per_part, (part + 1) * rows_per_part)
        pa = jnp.dot(at_ref[rows, :], wa_ref[...], preferred_element_type=F32)
        pr = jnp.dot(y_ref[rows, :], wr_ref[...], preferred_element_type=F32)
        g = _sigmoid(gt_ref[rows, :] + bg_ref[...])
        merged = g[:, :D_MODEL] * pa + g[:, D_MODEL:] * pr
        x1 = x_ref[rows, :] + jnp.dot(merged.astype(BF16), wo_ref[...], preferred_element_type=F32)
        x1_ref[rows, :] = x1
        xn = _rms(x1, g2_ref[...]).astype(BF16)
        xn_ref[rows, :] = xn
        comb = _route(jnp.dot(xn, wc_ref[...], preferred_element_type=F32) + bc_ref[...])
        cm_ref[rows, :] = comb
        c3_ref[rows, :] = _split_weights(comb)


def _merge(attn, y_rnn, gates, x2d, wa_b, wr_b, wo_b, b_gate, g2, w_cat_b, b_cat, tm, cast=()):
    n = x2d.shape[0]
    steps = n // tm
    row = lambda i: (i, 0)
    const = lambda i: (0, 0)
    sq = pl.BlockSpec((D_MODEL, D_MODEL), const)
    slabs = [a.reshape(steps, -1, a.shape[-1]) for a in cast]
    slab_specs = [pl.BlockSpec((1,) + s.shape[1:], lambda i: (i, 0, 0)) for s in slabs]
    outs = pl.pallas_call(
        functools.partial(_merge_kernel, parts=max(1, tm // MERGE_PART_ROWS), n_cast=len(cast)),
        grid=(steps,),
        in_specs=[pl.BlockSpec((tm, D_MODEL), row), pl.BlockSpec((tm, D_MODEL), row),
                  pl.BlockSpec((tm, 2 * D_MODEL), row), pl.BlockSpec((tm, D_MODEL), row),
                  sq, sq, sq, pl.BlockSpec((1, 2 * D_MODEL), const), pl.BlockSpec((1, D_MODEL), const),
                  pl.BlockSpec((D_MODEL, ROUTE_LANES), const), pl.BlockSpec((1, ROUTE_LANES), const)] + slab_specs,
        out_specs=[pl.BlockSpec((tm, D_MODEL), row), pl.BlockSpec((tm, D_MODEL), row),
                   pl.BlockSpec((tm, ROUTE_LANES), row), pl.BlockSpec((tm, ROUTE_LANES), row)] + slab_specs,
        out_shape=[jax.ShapeDtypeStruct((n, D_MODEL), F32), jax.ShapeDtypeStruct((n, D_MODEL), BF16),
                   jax.ShapeDtypeStruct((n, ROUTE_LANES), F32), jax.ShapeDtypeStruct((n, ROUTE_LANES), BF16)]
        + [jax.ShapeDtypeStruct(s.shape, BF16) for s in slabs],
        compiler_params=_params("parallel"),
    )(attn, y_rnn, gates, x2d, wa_b, wr_b, wo_b, b_gate, g2, w_cat_b, b_cat, *slabs)
    return outs[0], outs[1], outs[2], outs[3], tuple(o.reshape(a.shape) for o, a in zip(outs[4:], cast))


def _route_tables(comb, tile, batch):
    n = comb.shape[0]
    seq, chunk = n // batch, tile // batch
    nt = seq // chunk
    c = comb[:, EXPERT_LANE0:EXPERT_LANE0 + N_EXPERTS]
    sel = jnp.any(c.reshape(batch, nt, chunk, N_GROUPS, EXPERTS_PER_GROUP) != 0.0, axis=-1)
    sel = sel.transpose(1, 0, 2, 3).reshape(nt, tile, N_GROUPS)
    lower = (jnp.arange(tile)[:, None] >= jnp.arange(tile)[None, :]).astype(BF16)
    csum = jnp.einsum("ts,nsg->ntg", lower, sel.astype(BF16), preferred_element_type=F32).astype(jnp.int32)
    rank = jnp.where(sel, csum - 1, -1).astype(F32)
    counts = csum[:, -1, :].reshape(nt * N_GROUPS)
    rank_t = rank.transpose(0, 2, 1).reshape(nt * N_GROUPS, 1, tile)
    rank = rank.reshape(nt, batch, chunk, N_GROUPS).transpose(1, 0, 2, 3).reshape(batch, seq, N_GROUPS)
    rank = jnp.pad(rank, ((0, 0), (0, 0), (0, ROUTE_LANES - N_GROUPS)), constant_values=-1.0)
    return counts, rank, rank_t


def _moe_kernel(cnt_ref, x_ref, c3_ref, rk_ref, rkt_ref, x1_ref, gf_ref, *refs, tile):
    flat = lambda ref: ref[...].reshape(tile, ref.shape[-1])
    w1_refs, w3_refs, w2_refs = (refs[k * EXPERTS_PER_STEP:(k + 1) * EXPERTS_PER_STEP] for k in range(3))
    y_ref, xg, cg, yg = refs[3 * EXPERTS_PER_STEP:]
    t = pl.program_id(0)
    step = pl.program_id(1)
    steps_per_group = EXPERTS_PER_GROUP // EXPERTS_PER_STEP
    g = step // steps_per_group
    member = step % steps_per_group
    count = cnt_ref[t * N_GROUPS + g]
    nblk = (count + MOE_ROWS - 1) // MOE_ROWS

    @pl.when(step == 0)
    def _():
        y_ref[...] = jnp.zeros(y_ref.shape, F32)

    @pl.when(member == 0)
    def _():
        rank_row = rkt_ref[0]

        def gather(b, carry):
            slot = lax.broadcasted_iota(jnp.int32, (MOE_ROWS, tile), 0).astype(F32) + (b * MOE_ROWS).astype(F32)
            onehot = jnp.where(rank_row == slot, 1.0, 0.0).astype(BF16)
            xg[b] = jnp.dot(onehot, flat(x_ref), preferred_element_type=F32).astype(BF16)
            cg[b] = jnp.dot(onehot, flat(c3_ref), preferred_element_type=F32)
            yg[b] = jnp.zeros((MOE_ROWS, D_MODEL), F32)
            return carry

        lax.fori_loop(0, nblk, gather, 0)

    lane_r = lax.broadcasted_iota(jnp.int32, (MOE_ROWS, ROUTE_LANES), 1)

    def experts(b, carry):
        xc = xg[b]
        out = yg[b]
        for j in range(EXPERTS_PER_STEP):
            mine = (lane_r % N_EXPERTS == step * EXPERTS_PER_STEP + j) & (lane_r < 3 * N_EXPERTS)
            cw = jnp.sum(jnp.where(mine, cg[b], 0.0), axis=1, keepdims=True)
            a = jnp.dot(xc, w1_refs[j][0], preferred_element_type=F32)
            h = (a * _sigmoid(a)) * jnp.dot(xc, w3_refs[j][0], preferred_element_type=F32)
            out = out + cw * jnp.dot(h.astype(BF16), w2_refs[j][0], preferred_element_type=F32)
        yg[b] = out
        return carry

    lax.fori_loop(0, nblk, experts, 0)

    @pl.when(member == steps_per_group - 1)
    def _():
        lane = lax.broadcasted_iota(jnp.int32, (tile, ROUTE_LANES), 1)
        rank_col = jnp.sum(jnp.where(lane == g, flat(rk_ref), 0.0), axis=1, keepdims=True)

        def scatter(b, carry):
            slot_t = lax.broadcasted_iota(jnp.int32, (tile, MOE_ROWS), 1).astype(F32) + (b * MOE_ROWS).astype(F32)
            onehot_t = jnp.where(rank_col == slot_t, 1.0, 0.0).astype(BF16)
            y_ref[...] += jnp.dot(onehot_t, yg[b].astype(BF16), preferred_element_type=F32).reshape(y_ref.shape)
            return carry

        lax.fori_loop(0, nblk, scatter, 0)

    @pl.when(step == pl.num_programs(1) - 1)
    def _():
        y_ref[...] = _rms(x1_ref[...] + y_ref[...], gf_ref[...])


def _moe(xn2, comb, c3, x1, gf, w1_b, w3_b, w2_b, tile, batch):
    n = xn2.shape[0]
    seq, chunk = n // batch, tile // batch
    counts, rank, rank_t = _route_tables(comb, tile, batch)
    c3 = c3.reshape(batch, seq, ROUTE_LANES)
    max_blocks = -(-tile // MOE_ROWS)
    row = lambda t, s, cnt: (0, t, 0)
    rows = lambda width: pl.BlockSpec((batch, chunk, width), row)
    expert_specs = lambda shape: [pl.BlockSpec((1,) + shape, lambda t, s, cnt, j=j: (s * EXPERTS_PER_STEP + j, 0, 0))
                                  for j in range(EXPERTS_PER_STEP)]
    steps_per_group = EXPERTS_PER_GROUP // EXPERTS_PER_STEP
    return pl.pallas_call(
        functools.partial(_moe_kernel, tile=tile),
        grid_spec=pltpu.PrefetchScalarGridSpec(
            num_scalar_prefetch=1,
            grid=(n // tile, N_EXPERTS // EXPERTS_PER_STEP),
            in_specs=[rows(D_MODEL), rows(ROUTE_LANES), rows(ROUTE_LANES),
                      pl.BlockSpec((1, 1, tile), lambda t, s, cnt: (t * N_GROUPS + s // steps_per_group, 0, 0)),
                      rows(D_MODEL), pl.BlockSpec((1, D_MODEL), lambda t, s, cnt: (0, 0))]
            + expert_specs((D_MODEL, D_EXPERT)) + expert_specs((D_MODEL, D_EXPERT))
            + expert_specs((D_EXPERT, D_MODEL)),
            out_specs=rows(D_MODEL),
            scratch_shapes=[pltpu.VMEM((max_blocks, MOE_ROWS, D_MODEL), BF16),
                            pltpu.VMEM((max_blocks, MOE_ROWS, ROUTE_LANES), F32),
                            pltpu.VMEM((max_blocks, MOE_ROWS, D_MODEL), F32)]),
        out_shape=jax.ShapeDtypeStruct((batch, seq, D_MODEL), F32),
        compiler_params=_params("parallel", "arbitrary"),
    )(counts, xn2.reshape(batch, seq, D_MODEL), c3, rank, rank_t, x1.reshape(batch, seq, D_MODEL), gf, *([w1_b] * EXPERTS_PER_STEP + [w3_b] * EXPERTS_PER_STEP
                                                + [w2_b] * EXPERTS_PER_STEP))


def _tiles(batch, t):
    n = batch * t
    return dict(tm_in=min(256, n),
                tm_merge=min(2 * MERGE_PART_ROWS, n),
                tm_moe=min(1024, n),
                tq=min(512, t),
                tt=min(256, t))


def _stream(x, past, w, experts):
    batch, t, _ = x.shape
    n = batch * t
    tm_in, tm_merge, tm_moe, tq, tt = (_tiles(batch, t)[k] for k in ("tm_in", "tm_merge", "tm_moe", "tq", "tt"))
    x2d = x.reshape(n, D_MODEL)
    q, k, v, kb, vb, xr, gr, gates = _in_proj(x2d, w["g1"], w["w_in"], tm_in,
                                              seq=t if past is None else None)
    if past is None:
        attn = _attn_prompt(q, kb, vb, w["lams"], w["subln_g"], batch, t, tq)
        k = k.reshape(batch, N_HEADS, 2, QK_HEAD_DIM, t).transpose(0, 4, 1, 2, 3)
        conv_state = jnp.zeros((batch, CONV_WIDTH - 1, D_RNN), F32)
        h0 = jnp.zeros((batch, 1, D_RNN), F32)
    else:
        cache_k, cache_v, conv_state, h0 = past
        attn = _attn_sample(q, kb, vb, cache_k, cache_v, w["lams"], w["subln_g"], batch, t)
    y_rnn, new_conv, h_last = _rglru(xr, gr, conv_state, h0, w["conv_w"], w["conv_b"], w["w_a"], w["b_a"],
                                     w["w_i"], w["b_i"], w["rg_lambda"], batch, t, tt)
    to_cast = experts if experts[0].dtype != BF16 else ()
    x1, xn2, comb, c3, cast = _merge(attn, y_rnn, gates, x2d, w["w_attn"], w["w_rnn"], w["w_out"], w["b_gate"],
                                 w["g2"], w["w_cat"], w["b_cat"], tm_merge, cast=to_cast)
    experts = cast or experts
    y = _moe(xn2, comb, c3, x1, w["gf"], *experts, tm_moe, batch)
    return (y,
            k.reshape(1, batch, t, N_HEADS, 2, QK_HEAD_DIM),
            v.reshape(1, batch, t, N_HEADS, V_HEAD_DIM),
            new_conv.reshape(1, batch, CONV_WIDTH - 1, D_RNN),
            h_last.reshape(1, batch, D_RNN)), experts


def kernel(x_prompt, x_sample, cache_k, cache_v, state_conv, state_rnn, norm1_g, w_in, lambda_q1, lambda_k1, lambda_q2, lambda_k2, subln_g, w_attn_proj, conv_w, conv_b, w_rg_a, b_rg_a, w_rg_i, b_rg_i, rg_lambda, w_rnn_proj, b_gate, w_out, norm2_g, w_group, b_group, w_router, b_router, w1, w3, w2, final_norm_g):
    assert norm1_g.shape[0] == 1, "single-layer model"
    dec_batch, past_len = cache_k.shape[1], cache_k.shape[2]
    pad = ROUTE_LANES - N_GROUPS - N_EXPERTS
    w = {
        "g1": norm1_g, "g2": norm2_g, "gf": final_norm_g.reshape(1, D_MODEL),
        "w_in": w_in[0].astype(BF16),
        "lams": (lambda_q1, lambda_k1, lambda_q2, lambda_k2),
        "subln_g": subln_g,
        "w_attn": w_attn_proj[0].astype(BF16), "w_rnn": w_rnn_proj[0].astype(BF16),
        "w_out": w_out[0].astype(BF16),
        "conv_w": conv_w[0], "conv_b": conv_b,
        "w_a": w_rg_a[0].astype(BF16), "b_a": b_rg_a, "w_i": w_rg_i[0].astype(BF16), "b_i": b_rg_i,
        "rg_lambda": rg_lambda, "b_gate": b_gate,
        "w_cat": jnp.pad(jnp.concatenate([w_group[0], w_router[0]], axis=1), ((0, 0), (0, pad))).astype(BF16),
        "b_cat": jnp.pad(jnp.concatenate([b_group[0], b_router[0]]), (0, pad)).reshape(1, ROUTE_LANES),
    }
    (yp, kp, vp, cp, hp), experts_b = _stream(x_prompt, None, w, (w1[0], w3[0], w2[0]))
    past = (cache_k[0].transpose(0, 2, 3, 4, 1).reshape(dec_batch, D_MODEL, past_len),
            cache_v[0].reshape(dec_batch, past_len * N_HEADS, V_HEAD_DIM),
            state_conv[0], state_rnn[0].reshape(dec_batch, 1, D_RNN))
    x_sample, yp = lax.optimization_barrier((x_sample, yp))
    (ys, ks, vs, cs, hs), _ = _stream(x_sample, past, w, experts_b)
    return (yp, ys, kp, vp, cp, hp, ks, vs, cs, hs)
```

```python
import functools
import math

import jax
import jax.numpy as jnp
from jax import lax
from jax.experimental import pallas as pl
from jax.experimental.pallas import tpu as pltpu

F32 = jnp.float32
BF16 = jnp.bfloat16

D_MODEL = 1024
CHUNK = 64
N_HEADS = 8
QK_HEAD_DIM = 64
V_HEAD_DIM = 128
HEAD_COLS = 2 * QK_HEAD_DIM
D_RNN = 1024
N_RNN_BLOCKS = 8
RNN_BLOCK = D_RNN // N_RNN_BLOCKS
CONV_WIDTH = 4
RG_C = 8.0
N_GROUPS = 4
EXPERTS_PER_GROUP = 4
N_EXPERTS = N_GROUPS * EXPERTS_PER_GROUP
D_EXPERT = 512
EPS = 1e-6
LAMBDA_INIT = 0.8 - 0.6 * math.exp(-0.3 * 0)
Q_SCALE = QK_HEAD_DIM ** -0.5 * math.log2(math.e)
N_SEG = 7
IN_WIDTH = N_SEG * D_MODEL
ROUTE_LANES = 128
MOE_ROWS = 288
EXPERT_LANE0 = N_GROUPS
SUBLANES = 8
LANES = 128
HEADS_PER_STEP = 8
MERGE_PART_ROWS = 256
VMEM_LIMIT = 56 * 1024 * 1024
NEG = float(jnp.finfo(jnp.float32).min)


def _params(*sem):
    return pltpu.CompilerParams(dimension_semantics=sem, vmem_limit_bytes=VMEM_LIMIT)


def _rms(x, g):
    return x * lax.rsqrt(jnp.mean(x * x, axis=-1, keepdims=True) + EPS) * g


def _sigmoid(x):
    return 0.5 * jnp.tanh(0.5 * x) + 0.5


def _in_proj_kernel(x_ref, g_ref, w_ref, q_ref, k_ref, v_ref, kb_ref, vb_ref, xr_ref, gr_ref, gt_ref,
                    *, k_transposed):
    xn = _rms(x_ref[...], g_ref[...]).astype(BF16)

    def seg(j):
        return jnp.dot(xn, w_ref[:, j * D_MODEL:(j + 1) * D_MODEL], preferred_element_type=F32)

    q_ref[...] = (seg(0) * Q_SCALE).astype(BF16)
    if k_transposed:
        kt = seg(1).T
        k_ref[0] = kt
        kb_ref[0, 0] = kt.astype(BF16)
    else:
        k = seg(1)
        k_ref[...] = k
        kb_ref[...] = k.astype(BF16)
    v = seg(2)
    v_ref[...] = v
    vb_ref[...] = v.astype(BF16)
    xr_ref[...] = seg(3)
    gr_ref[...] = seg(4)
    gt_ref[:, :D_MODEL] = seg(5)
    gt_ref[:, D_MODEL:] = seg(6)


def _in_proj(x2d, g1, w_in_b, tm, seq=None):
    n = x2d.shape[0]
    row = lambda i: (i, 0)
    const = lambda i: (0, 0)
    wide = lambda dt: jax.ShapeDtypeStruct((n, D_MODEL), dt)
    wide_spec = pl.BlockSpec((tm, D_MODEL), row)
    if seq is None:
        k_shape, k_spec, kb_shape, kb_spec = wide(F32), wide_spec, wide(BF16), wide_spec
    else:
        nt = seq // tm
        k_shape = jax.ShapeDtypeStruct((n // seq, D_MODEL, seq), F32)
        k_spec = pl.BlockSpec((1, D_MODEL, tm), lambda i: (i // nt, 0, i % nt))
        kb_shape = jax.ShapeDtypeStruct((n // seq, nt, D_MODEL, tm), BF16)
        kb_spec = pl.BlockSpec((1, 1, D_MODEL, tm), lambda i: (i // nt, i % nt, 0, 0))
    return pl.pallas_call(
        functools.partial(_in_proj_kernel, k_transposed=seq is not None),
        grid=(n // tm,),
        in_specs=[wide_spec,
                  pl.BlockSpec((1, D_MODEL), const),
                  pl.BlockSpec((D_MODEL, IN_WIDTH), const, pipeline_mode=pl.Buffered(1))],
        out_specs=[wide_spec, k_spec, wide_spec, kb_spec, wide_spec, wide_spec, wide_spec,
                   pl.BlockSpec((tm, 2 * D_MODEL), row)],
        out_shape=[wide(BF16), k_shape, wide(F32), kb_shape, wide(BF16), wide(F32), wide(F32),
                   jax.ShapeDtypeStruct((n, 2 * D_MODEL), F32)],
        compiler_params=_params("parallel"),
    )(x2d, g1, w_in_b)


def _lam(lq1, lk1, lq2, lk2):
    return (jnp.exp(jnp.sum(lq1[...] * lk1[...], axis=-1, keepdims=True))
            - jnp.exp(jnp.sum(lq2[...] * lk2[...], axis=-1, keepdims=True)) + LAMBDA_INIT)


def _split_q(q):
    lane = lax.broadcasted_iota(jnp.int32, q.shape, 1)
    zero = jnp.zeros_like(q)
    return jnp.where(lane < QK_HEAD_DIM, q, zero), jnp.where(lane >= QK_HEAD_DIM, q, zero)


def _scores(qm, k):
    return lax.dot_general(qm, k, (((1,), (1,)), ((), ())), preferred_element_type=F32)


def _subln(o, g):
    return (_rms(o, g) * (1.0 - LAMBDA_INIT)).astype(BF16)


def _attn_prompt_kernel(lq1, lk1, lq2, lk2, sg_ref, q_ref, k_ref, v_ref, o_ref, m_ref, acc_ref, *, tq, kw):
    i = pl.program_id(2)
    ones = jnp.ones((tq, LANES), BF16)
    qq = []
    for h in range(HEADS_PER_STEP):
        q1, q2 = _split_q(q_ref[:, h * HEAD_COLS:(h + 1) * HEAD_COLS])
        qq.append(jnp.concatenate([q1, q2], axis=0))

    def row_max(chunks):
        part = chunks[0]
        for x in chunks[1:]:
            part = jnp.maximum(part, x)
        return jnp.max(part, axis=1, keepdims=True)

    def split(s):
        return [s[:, c * LANES:(c + 1) * LANES] for c in range(s.shape[1] // LANES)]

    def head_cols(h, width):
        return slice(h * width, (h + 1) * width)

    def diagonal_tile(h):
        row = lax.broadcasted_iota(jnp.int32, (2 * kw, LANES), 0)
        lane = lax.broadcasted_iota(jnp.int32, (2 * kw, LANES), 1)
        qchunk = (row % kw) // CHUNK
        for r in range(tq // kw):
            rows1, rows2 = slice(r * kw, (r + 1) * kw), slice(tq + r * kw, tq + (r + 1) * kw)
            qr = jnp.concatenate([qq[h][rows1], qq[h][rows2]], axis=0)
            chunks = []
            for j in range(r + 1):
                s = split(jnp.dot(qr, k_ref[0, i * (tq // kw) + j, head_cols(h, HEAD_COLS), :],
                                  preferred_element_type=F32))
                if j == r:
                    s = [jnp.where((lane + c * LANES) // CHUNK <= qchunk, x, NEG) for c, x in enumerate(s)]
                chunks += s
            m_new = jnp.broadcast_to(row_max(chunks), (2 * kw, LANES))
            p = jnp.concatenate([jnp.exp2(x - m_new).astype(BF16) for x in chunks], axis=1)
            keys = (r + 1) * kw
            start = pl.multiple_of(i * tq, tq)
            v1 = jnp.concatenate([v_ref[pl.ds(start, keys), head_cols(h, V_HEAD_DIM)],
                                  jnp.ones((keys, LANES), BF16)], axis=1)
            pv = jnp.dot(p, v1, preferred_element_type=F32)
            for rows, part in ((rows1, slice(0, kw)), (rows2, slice(kw, 2 * kw))):
                acc_ref[h, rows, :] = pv[part]
                m_ref[h, rows, :] = m_new[part]

    def tile(kt, h):
        start = pl.multiple_of(kt * tq, tq)
        chunks = []
        for j in range(tq // kw):
            chunks += split(jnp.dot(qq[h], k_ref[0, kt * (tq // kw) + j, head_cols(h, HEAD_COLS), :],
                                    preferred_element_type=F32))
        m_next = jnp.maximum(m_ref[h], row_max(chunks))
        p = jnp.concatenate([jnp.exp2(x - m_next).astype(BF16) for x in chunks], axis=1)
        v1 = jnp.concatenate([v_ref[pl.ds(start, tq), head_cols(h, V_HEAD_DIM)], ones], axis=1)
        alpha = jnp.exp2(m_ref[h] - m_next)
        acc_ref[h] = (jnp.concatenate([alpha, alpha], axis=1) * acc_ref[h]
                      + jnp.dot(p, v1, preferred_element_type=F32))
        m_ref[h] = m_next

    def body(kt, c):
        for h in range(HEADS_PER_STEP):
            tile(kt, h)
        return c

    for h in range(HEADS_PER_STEP):
        diagonal_tile(h)
    lax.fori_loop(0, i, body, 0)
    lam = _lam(lq1, lk1, lq2, lk2)
    for h in range(HEADS_PER_STEP):
        o1 = acc_ref[h, :tq, :V_HEAD_DIM] / acc_ref[h, :tq, V_HEAD_DIM:]
        o2 = acc_ref[h, tq:, :V_HEAD_DIM] / acc_ref[h, tq:, V_HEAD_DIM:]
        o_ref[:, h * V_HEAD_DIM:(h + 1) * V_HEAD_DIM] = _subln(o1 - lam * o2, sg_ref[...])


def _attn_prompt(q, kbt, vb, lams, subln_g, batch, seq, tq):
    n = q.shape[0]
    nq = seq // tq
    nkb, kw = kbt.shape[1], kbt.shape[3]
    small = lambda b, h, i: (0, 0)
    qmap = lambda b, h, i: (b * nq + i, h)
    kvmap = lambda b, h, i: (b, h)
    return pl.pallas_call(
        functools.partial(_attn_prompt_kernel, tq=tq, kw=kw),
        grid=(batch, N_HEADS // HEADS_PER_STEP, nq),
        in_specs=[pl.BlockSpec((1, QK_HEAD_DIM), small)] * 4 + [
            pl.BlockSpec((1, V_HEAD_DIM), small),
            pl.BlockSpec((tq, HEADS_PER_STEP * HEAD_COLS), qmap),
            pl.BlockSpec((1, nkb, HEADS_PER_STEP * HEAD_COLS, kw), lambda b, h, i: (b, 0, h, 0)),
            pl.BlockSpec((seq, HEADS_PER_STEP * V_HEAD_DIM), kvmap)],
        out_specs=pl.BlockSpec((tq, HEADS_PER_STEP * V_HEAD_DIM), qmap),
        out_shape=jax.ShapeDtypeStruct((n, N_HEADS * V_HEAD_DIM), BF16),
        scratch_shapes=[pltpu.VMEM((HEADS_PER_STEP, 2 * tq, LANES), F32),
                        pltpu.VMEM((HEADS_PER_STEP, 2 * tq, 2 * LANES), F32)],
        compiler_params=_params("parallel", "parallel", "arbitrary"),
    )(*lams, subln_g, q, kbt, vb)


def _attn_sample_kernel(lq1, lk1, lq2, lk2, sg_ref, q_ref, kn_ref, vn_ref, ck_ref, cv_ref, o_ref, *, past):
    lam = _lam(lq1, lk1, lq2, lk2)
    for h in range(N_HEADS):
        cols = slice(h * HEAD_COLS, (h + 1) * HEAD_COLS)
        qq = jnp.concatenate(_split_q(q_ref[:, cols]), axis=0)
        kct = ck_ref[0, h * HEAD_COLS:(h + 1) * HEAD_COLS, :].astype(BF16)
        vc = cv_ref[0, pl.ds(h, past, stride=N_HEADS), :].astype(BF16)
        sc = jnp.dot(qq, kct, preferred_element_type=F32)
        sn = _scores(qq, kn_ref[:, cols])
        m = jnp.maximum(jnp.max(sc, axis=-1, keepdims=True), jnp.max(sn, axis=-1, keepdims=True))
        pc = jnp.exp2(sc - m)
        pn = jnp.exp2(sn - m)
        l = jnp.sum(pc, axis=-1, keepdims=True) + jnp.sum(pn, axis=-1, keepdims=True)
        acc = (jnp.dot(pc.astype(BF16), vc, preferred_element_type=F32)
               + jnp.dot(pn.astype(BF16), vn_ref[:, cols], preferred_element_type=F32))
        o = acc / l
        t = q_ref.shape[0]
        o_ref[:, cols] = _subln(o[:t] - lam * o[t:], sg_ref[...])


def _attn_sample(q, kb, vb, cache_kt, cache_v, lams, subln_g, batch, t):
    past = cache_kt.shape[2]
    small = lambda b: (0, 0)
    row = lambda b: (b, 0)
    cache = lambda b: (b, 0, 0)
    return pl.pallas_call(
        functools.partial(_attn_sample_kernel, past=past),
        grid=(batch,),
        in_specs=[pl.BlockSpec((1, QK_HEAD_DIM), small)] * 4 + [
            pl.BlockSpec((1, V_HEAD_DIM), small),
            pl.BlockSpec((t, D_MODEL), row),
            pl.BlockSpec((t, D_MODEL), row),
            pl.BlockSpec((t, D_MODEL), row),
            pl.BlockSpec((1, D_MODEL, past), cache),
            pl.BlockSpec((1, past * N_HEADS, V_HEAD_DIM), cache)],
        out_specs=pl.BlockSpec((t, D_MODEL), row),
        out_shape=jax.ShapeDtypeStruct((batch * t, D_MODEL), BF16),
        compiler_params=_params("parallel"),
    )(*lams, subln_g, q, kb, vb, cache_kt, cache_v)


def _rglru_kernel(xr_ref, gr_ref, cs_ref, h0_ref, cw_ref, cb_ref, wa_ref, ba_ref, wi_ref, bi_ref, lam_ref,
                  y_ref, nc_ref, hl_ref, xpad, hcar, a_s, b_s, h_s, *, tt):
    t = pl.program_id(1)
    pad = SUBLANES

    @pl.when(t == 0)
    def _():
        xpad[0:pad, :] = jnp.zeros((pad, D_RNN), F32)
        xpad[pad - (CONV_WIDTH - 1):pad, :] = cs_ref[0]
        hcar[...] = h0_ref[0]

    xpad[pad:pad + tt, :] = xr_ref[...]
    xc = cb_ref[...]
    for j in range(CONV_WIDTH):
        off = pad - (CONV_WIDTH - 1) + j
        xc = xc + xpad[off:off + tt, :] * cw_ref[j:j + 1, :]
    xcb = xc.astype(BF16)

    z = -lam_ref[...]
    softplus = jnp.maximum(z, 0.0) + jnp.log1p(jnp.exp(-jnp.abs(z)))
    for n in range(N_RNN_BLOCKS):
        cols = slice(n * RNN_BLOCK, (n + 1) * RNN_BLOCK)
        xs = xcb[:, cols]
        r = _sigmoid(jnp.dot(xs, wa_ref[n], preferred_element_type=F32) + ba_ref[:, cols])
        i = _sigmoid(jnp.dot(xs, wi_ref[n], preferred_element_type=F32) + bi_ref[:, cols])
        log_a = -RG_C * r * softplus[:, cols]
        a_s[:, cols] = jnp.exp(log_a)
        th = jnp.tanh(log_a)
        b_s[:, cols] = jnp.sqrt(-2.0 * th / (1.0 - th)) * (i * xc[:, cols])

    h = hcar[...]
    for s in range(tt):
        h = a_s[s:s + 1, :] * h + b_s[s:s + 1, :]
        h_s[s:s + 1, :] = h
    hcar[...] = h
    y_ref[...] = (h_s[...] * jax.nn.gelu(gr_ref[...])).astype(BF16)
    nc_ref[0] = xpad[pad + tt - (CONV_WIDTH - 1):pad + tt, :]
    hl_ref[0] = h
    xpad[0:pad, :] = xpad[tt:tt + pad, :]


def _rglru(xr, gr, conv_state, h0, conv_w, conv_b, w_a_b, b_a, w_i_b, b_i, rg_lambda, batch, t, tt):
    nt = t // tt
    row = lambda b, s: (b * nt + s, 0)
    const2 = lambda b, s: (0, 0)
    const3 = lambda b, s: (0, 0, 0)
    per_b = lambda b, s: (b, 0, 0)
    vec = pl.BlockSpec((1, D_RNN), const2)
    blk = pl.BlockSpec((N_RNN_BLOCKS, RNN_BLOCK, RNN_BLOCK), const3)
    big = pltpu.VMEM((tt, D_RNN), F32)
    return pl.pallas_call(
        functools.partial(_rglru_kernel, tt=tt),
        grid=(batch, nt),
        in_specs=[pl.BlockSpec((tt, D_RNN), row), pl.BlockSpec((tt, D_RNN), row),
                  pl.BlockSpec((1, CONV_WIDTH - 1, D_RNN), per_b), pl.BlockSpec((1, 1, D_RNN), per_b),
                  pl.BlockSpec((CONV_WIDTH, D_RNN), const2), vec, blk, vec, blk, vec, vec],
        out_specs=[pl.BlockSpec((tt, D_RNN), row),
                   pl.BlockSpec((1, CONV_WIDTH - 1, D_RNN), per_b),
                   pl.BlockSpec((1, 1, D_RNN), per_b)],
        out_shape=[jax.ShapeDtypeStruct((batch * t, D_RNN), BF16),
                   jax.ShapeDtypeStruct((batch, CONV_WIDTH - 1, D_RNN), F32),
                   jax.ShapeDtypeStruct((batch, 1, D_RNN), F32)],
        scratch_shapes=[pltpu.VMEM((tt + SUBLANES, D_RNN), F32), pltpu.VMEM((1, D_RNN), F32), big, big, big],
        compiler_params=_params("parallel", "arbitrary"),
    )(xr, gr, conv_state, h0, conv_w, conv_b, w_a_b, b_a, w_i_b, b_i, rg_lambda)


def _first_lane_of_max(vals, valid, lane):
    masked = jnp.where(valid, vals, -jnp.inf)
    mx = jnp.max(masked, axis=-1, keepdims=True)
    idx = jnp.min(jnp.where(valid & (masked == mx), lane, ROUTE_LANES), axis=-1, keepdims=True)
    return mx, idx


def _route(logits):
    lane = lax.broadcasted_iota(jnp.int32, logits.shape, 1)
    is_group = lane < N_GROUPS
    gmax, gsel = _first_lane_of_max(logits, is_group, lane)
    gsum = jnp.sum(jnp.where(is_group, jnp.exp(logits - gmax), 0.0), axis=-1, keepdims=True)
    g_w = 1.0 / gsum
    expert = lane - EXPERT_LANE0
    in_group = (expert >= 0) & (expert < N_EXPERTS) & ((expert // EXPERTS_PER_GROUP) == gsel)
    v1, i1 = _first_lane_of_max(logits, in_group, lane)
    v2, i2 = _first_lane_of_max(logits, in_group & (lane != i1), lane)
    e2 = jnp.exp(v2 - v1)
    den = 1.0 + e2
    return jnp.where(lane == i1, (1.0 / den) * g_w, 0.0) + jnp.where(lane == i2, (e2 / den) * g_w, 0.0)


def _split_weights(comb):
    hi = comb.astype(BF16)
    rest = comb - hi.astype(F32)
    mid = rest.astype(BF16)
    lo = (rest - mid.astype(F32)).astype(BF16)
    src = lax.broadcasted_iota(jnp.int32, (ROUTE_LANES, ROUTE_LANES), 0)
    dst = lax.broadcasted_iota(jnp.int32, (ROUTE_LANES, ROUTE_LANES), 1)
    out = jnp.zeros(comb.shape, F32)
    for piece, base in ((hi, 0), (mid, N_EXPERTS), (lo, 2 * N_EXPERTS)):
        inside = jnp.where(dst < base + N_EXPERTS, dst - base, -1)
        move = jnp.where(src == inside + EXPERT_LANE0, 1.0, 0.0)
        move = jnp.where(inside >= 0, move, 0.0).astype(BF16)
        out = out + jnp.dot(piece, move, preferred_element_type=F32)
    return out.astype(BF16)


def _merge_kernel(at_ref, y_ref, gt_ref, x_ref, wa_ref, wr_ref, wo_ref, bg_ref, g2_ref, wc_ref, bc_ref, *refs,
                  parts, n_cast):
    x1_ref, xn_ref, cm_ref, c3_ref = refs[n_cast:n_cast + 4]
    for src, dst in zip(refs[:n_cast], refs[n_cast + 4:]):
        dst[...] = src[...].astype(BF16)
    rows_per_part = x_ref.shape[0] // parts
    for part in range(parts):
        rows = slice(part * rows_per_part, (part + 1) * rows_per_part)
        pa = jnp.dot(at_ref[rows, :], wa_ref[...], preferred_element_type=F32)
        pr = jnp.dot(y_ref[rows, :], wr_ref[...], preferred_element_type=F32)
        g = _sigmoid(gt_ref[rows, :] + bg_ref[...])
        merged = g[:, :D_MODEL] * pa + g[:, D_MODEL:] * pr
        x1 = x_ref[rows, :] + jnp.dot(merged.astype(BF16), wo_ref[...], preferred_element_type=F32)
        x1_ref[rows, :] = x1
        xn = _rms(x1, g2_ref[...]).astype(BF16)
        xn_ref[rows, :] = xn
        comb = _route(jnp.dot(xn, wc_ref[...], preferred_element_type=F32) + bc_ref[...])
        cm_ref[rows, :] = comb
        c3_ref[rows, :] = _split_weights(comb)


def _merge(attn, y_rnn, gates, x2d, wa_b, wr_b, wo_b, b_gate, g2, w_cat_b, b_cat, tm, cast=()):
    n = x2d.shape[0]
    steps = n // tm
    row = lambda i: (i, 0)
    const = lambda i: (0, 0)
    sq = pl.BlockSpec((D_MODEL, D_MODEL), const)
    slabs = [a.reshape(steps, -1, a.shape[-1]) for a in cast]
    slab_specs = [pl.BlockSpec((1,) + s.shape[1:], lambda i: (i, 0, 0)) for s in slabs]
    outs = pl.pallas_call(
        functools.partial(_merge_kernel, parts=max(1, tm // MERGE_PART_ROWS), n_cast=len(cast)),
        grid=(steps,),
        in_specs=[pl.BlockSpec((tm, D_MODEL), row), pl.BlockSpec((tm, D_MODEL), row),
                  pl.BlockSpec((tm, 2 * D_MODEL), row), pl.BlockSpec((tm, D_MODEL), row),
                  sq, sq, sq, pl.BlockSpec((1, 2 * D_MODEL), const), pl.BlockSpec((1, D_MODEL), const),
                  pl.BlockSpec((D_MODEL, ROUTE_LANES), const), pl.BlockSpec((1, ROUTE_LANES), const)] + slab_specs,
        out_specs=[pl.BlockSpec((tm, D_MODEL), row), pl.BlockSpec((tm, D_MODEL), row),
                   pl.BlockSpec((tm, ROUTE_LANES), row), pl.BlockSpec((tm, ROUTE_LANES), row)] + slab_specs,
        out_shape=[jax.ShapeDtypeStruct((n, D_MODEL), F32), jax.ShapeDtypeStruct((n, D_MODEL), BF16),
                   jax.ShapeDtypeStruct((n, ROUTE_LANES), F32), jax.ShapeDtypeStruct((n, ROUTE_LANES), BF16)]
        + [jax.ShapeDtypeStruct(s.shape, BF16) for s in slabs],
        compiler_params=_params("parallel"),
    )(attn, y_rnn, gates, x2d, wa_b, wr_b, wo_b, b_gate, g2, w_cat_b, b_cat, *slabs)
    return outs[0], outs[1], outs[2], outs[3], tuple(o.reshape(a.shape) for o, a in zip(outs[4:], cast))


def _route_tables(comb, tile, batch):
    n = comb.shape[0]
    seq, chunk = n // batch, tile // batch
    nt = seq // chunk
    c = comb[:, EXPERT_LANE0:EXPERT_LANE0 + N_EXPERTS]
    sel = jnp.any(c.reshape(batch, nt, chunk, N_GROUPS, EXPERTS_PER_GROUP) != 0.0, axis=-1)
    sel = sel.transpose(1, 0, 2, 3).reshape(nt, tile, N_GROUPS)
    lower = (jnp.arange(tile)[:, None] >= jnp.arange(tile)[None, :]).astype(BF16)
    csum = jnp.einsum("ts,nsg->ntg", lower, sel.astype(BF16), preferred_element_type=F32).astype(jnp.int32)
    rank = jnp.where(sel, csum - 1, -1).astype(F32)
    counts = csum[:, -1, :].reshape(nt * N_GROUPS)
    rank_t = rank.transpose(0, 2, 1).reshape(nt * N_GROUPS, 1, tile)
    rank = rank.reshape(nt, batch, chunk, N_GROUPS).transpose(1, 0, 2, 3).reshape(batch, seq, N_GROUPS)
    rank = jnp.pad(rank, ((0, 0), (0, 0), (0, ROUTE_LANES - N_GROUPS)), constant_values=-1.0)
    return counts, rank, rank_t


def _moe_kernel(cnt_ref, x_ref, c3_ref, rk_ref, rkt_ref, x1_ref, gf_ref, *refs, tile):
    flat = lambda ref: ref[...].reshape(tile, ref.shape[-1])
    w1_refs, w3_refs, w2_refs = (refs[k * EXPERTS_PER_GROUP:(k + 1) * EXPERTS_PER_GROUP] for k in range(3))
    y_ref = refs[3 * EXPERTS_PER_GROUP]
    t = pl.program_id(0)
    g = pl.program_id(1)
    count = cnt_ref[t * N_GROUPS + g]

    @pl.when(g == 0)
    def _():
        y_ref[...] = jnp.zeros(y_ref.shape, F32)

    rank_row = rkt_ref[0]
    lane = lax.broadcasted_iota(jnp.int32, (tile, ROUTE_LANES), 1)
    rank_col = jnp.sum(jnp.where(lane == g, flat(rk_ref), 0.0), axis=1, keepdims=True)
    lane_r = lax.broadcasted_iota(jnp.int32, (MOE_ROWS, ROUTE_LANES), 1)

    def block(b, carry):
        base = (b * MOE_ROWS).astype(F32)
        slot = lax.broadcasted_iota(jnp.int32, (MOE_ROWS, tile), 0).astype(F32) + base
        onehot = jnp.where(rank_row == slot, 1.0, 0.0).astype(BF16)
        xc = jnp.dot(onehot, flat(x_ref), preferred_element_type=F32).astype(BF16)
        c3 = jnp.dot(onehot, flat(c3_ref), preferred_element_type=F32)
        out = jnp.zeros((MOE_ROWS, D_MODEL), F32)
        for j in range(EXPERTS_PER_GROUP):
            mine = (lane_r % N_EXPERTS == g * EXPERTS_PER_GROUP + j) & (lane_r < 3 * N_EXPERTS)
            cw = jnp.sum(jnp.where(mine, c3, 0.0), axis=1, keepdims=True)
            a = jnp.dot(xc, w1_refs[j][0], preferred_element_type=F32)
            h = (a * _sigmoid(a)) * jnp.dot(xc, w3_refs[j][0], preferred_element_type=F32)
            out = out + cw * jnp.dot(h.astype(BF16), w2_refs[j][0], preferred_element_type=F32)
        slot_t = lax.broadcasted_iota(jnp.int32, (tile, MOE_ROWS), 1).astype(F32) + base
        onehot_t = jnp.where(rank_col == slot_t, 1.0, 0.0).astype(BF16)
        y_ref[...] += jnp.dot(onehot_t, out.astype(BF16), preferred_element_type=F32).reshape(y_ref.shape)
        return carry

    lax.fori_loop(0, (count + MOE_ROWS - 1) // MOE_ROWS, block, 0)

    @pl.when(g == N_GROUPS - 1)
    def _():
        y_ref[...] = _rms(x1_ref[...] + y_ref[...], gf_ref[...])


def _moe(xn2, comb, c3, x1, gf, w1_b, w3_b, w2_b, tile, batch):
    n = xn2.shape[0]
    seq, chunk = n // batch, tile // batch
    counts, rank, rank_t = _route_tables(comb, tile, batch)
    row = lambda t, g, cnt: (0, t, 0)
    rows = lambda width: pl.BlockSpec((batch, chunk, width), row)
    expert_specs = lambda shape: [pl.BlockSpec((1,) + shape, lambda t, g, cnt, j=j: (g * EXPERTS_PER_GROUP + j, 0, 0))
                                  for j in range(EXPERTS_PER_GROUP)]
    return pl.pallas_call(
        functools.partial(_moe_kernel, tile=tile),
        grid_spec=pltpu.PrefetchScalarGridSpec(
            num_scalar_prefetch=1,
            grid=(n // tile, N_GROUPS),
            in_specs=[rows(D_MODEL), rows(ROUTE_LANES), rows(ROUTE_LANES),
                      pl.BlockSpec((1, 1, tile), lambda t, g, cnt: (t * N_GROUPS + g, 0, 0)),
                      rows(D_MODEL), pl.BlockSpec((1, D_MODEL), lambda t, g, cnt: (0, 0))]
            + expert_specs((D_MODEL, D_EXPERT)) + expert_specs((D_MODEL, D_EXPERT))
            + expert_specs((D_EXPERT, D_MODEL)),
            out_specs=rows(D_MODEL)),
        out_shape=jax.ShapeDtypeStruct((batch, seq, D_MODEL), F32),
        compiler_params=_params("parallel", "arbitrary"),
    )(counts, xn2.reshape(batch, seq, D_MODEL), c3.reshape(batch, seq, ROUTE_LANES), rank, rank_t,
      x1.reshape(batch, seq, D_MODEL), gf,
      *([w1_b] * EXPERTS_PER_GROUP + [w3_b] * EXPERTS_PER_GROUP + [w2_b] * EXPERTS_PER_GROUP))


def _tiles(batch, t):
    n = batch * t
    return dict(tm_in=min(256, n),
                tm_merge=min(2 * MERGE_PART_ROWS, n),
                tm_moe=min(1024, n),
                tq=min(512, t),
                tt=min(256, t))


def _stream(x, past, w, experts):
    batch, t, _ = x.shape
    n = batch * t
    tm_in, tm_merge, tm_moe, tq, tt = (_tiles(batch, t)[k] for k in ("tm_in", "tm_merge", "tm_moe", "tq", "tt"))
    x2d = x.reshape(n, D_MODEL)
    q, k, v, kb, vb, xr, gr, gates = _in_proj(x2d, w["g1"], w["w_in"], tm_in,
                                              seq=t if past is None else None)
    if past is None:
        attn = _attn_prompt(q, kb, vb, w["lams"], w["subln_g"], batch, t, tq)
        k = k.reshape(batch, N_HEADS, 2, QK_HEAD_DIM, t).transpose(0, 4, 1, 2, 3)
        conv_state = jnp.zeros((batch, CONV_WIDTH - 1, D_RNN), F32)
        h0 = jnp.zeros((batch, 1, D_RNN), F32)
    else:
        cache_k, cache_v, conv_state, h0 = past
        attn = _attn_sample(q, kb, vb, cache_k, cache_v, w["lams"], w["subln_g"], batch, t)
    y_rnn, new_conv, h_last = _rglru(xr, gr, conv_state, h0, w["conv_w"], w["conv_b"], w["w_a"], w["b_a"],
                                     w["w_i"], w["b_i"], w["rg_lambda"], batch, t, tt)
    to_cast = experts if experts[0].dtype != BF16 else ()
    x1, xn2, comb, c3, cast = _merge(attn, y_rnn, gates, x2d, w["w_attn"], w["w_rnn"], w["w_out"], w["b_gate"],
                                 w["g2"], w["w_cat"], w["b_cat"], tm_merge, cast=to_cast)
    experts = cast or experts
    y = _moe(xn2, comb, c3, x1, w["gf"], *experts, tm_moe, batch)
    return (y,
            k.reshape(1, batch, t, N_HEADS, 2, QK_HEAD_DIM),
            v.reshape(1, batch, t, N_HEADS, V_HEAD_DIM),
            new_conv.reshape(1, batch, CONV_WIDTH - 1, D_RNN),
            h_last.reshape(1, batch, D_RNN)), experts


def kernel(x_prompt, x_sample, cache_k, cache_v, state_conv, state_rnn, norm1_g, w_in, lambda_q1, lambda_k1, lambda_q2, lambda_k2, subln_g, w_attn_proj, conv_w, conv_b, w_rg_a, b_rg_a, w_rg_i, b_rg_i, rg_lambda, w_rnn_proj, b_gate, w_out, norm2_g, w_group, b_group, w_router, b_router, w1, w3, w2, final_norm_g):
    assert norm1_g.shape[0] == 1, "single-layer model"
    dec_batch, past_len = cache_k.shape[1], cache_k.shape[2]
    pad = ROUTE_LANES - N_GROUPS - N_EXPERTS
    w = {
        "g1": norm1_g, "g2": norm2_g, "gf": final_norm_g.reshape(1, D_MODEL),
        "w_in": w_in[0].astype(BF16),
        "lams": (lambda_q1, lambda_k1, lambda_q2, lambda_k2),
        "subln_g": subln_g,
        "w_attn": w_attn_proj[0].astype(BF16), "w_rnn": w_rnn_proj[0].astype(BF16),
        "w_out": w_out[0].astype(BF16),
        "conv_w": conv_w[0], "conv_b": conv_b,
        "w_a": w_rg_a[0].astype(BF16), "b_a": b_rg_a, "w_i": w_rg_i[0].astype(BF16), "b_i": b_rg_i,
        "rg_lambda": rg_lambda, "b_gate": b_gate,
        "w_cat": jnp.pad(jnp.concatenate([w_group[0], w_router[0]], axis=1), ((0, 0), (0, pad))).astype(BF16),
        "b_cat": jnp.pad(jnp.concatenate([b_group[0], b_router[0]]), (0, pad)).reshape(1, ROUTE_LANES),
    }
    (yp, kp, vp, cp, hp), experts_b = _stream(x_prompt, None, w, (w1[0], w3[0], w2[0]))
    past = (cache_k[0].transpose(0, 2, 3, 4, 1).reshape(dec_batch, D_MODEL, past_len),
            cache_v[0].reshape(dec_batch, past_len * N_HEADS, V_HEAD_DIM),
            state_conv[0], state_rnn[0].reshape(dec_batch, 1, D_RNN))
    x_sample, yp = lax.optimization_barrier((x_sample, yp))
    (ys, ks, vs, cs, hs), _ = _stream(x_sample, past, w, experts_b)
    return (yp, ys, kp, vp, cp, hp, ks, vs, cs, hs)
```

```python
import functools
import math

import jax
import jax.numpy as jnp
from jax import lax
from jax.experimental import pallas as pl
from jax.experimental.pallas import tpu as pltpu

F32 = jnp.float32
BF16 = jnp.bfloat16

D_MODEL = 1024
CHUNK = 64
N_HEADS = 8
QK_HEAD_DIM = 64
V_HEAD_DIM = 128
HEAD_COLS = 2 * QK_HEAD_DIM
D_RNN = 1024
N_RNN_BLOCKS = 8
RNN_BLOCK = D_RNN // N_RNN_BLOCKS
CONV_WIDTH = 4
RG_C = 8.0
N_GROUPS = 4
EXPERTS_PER_GROUP = 4
N_EXPERTS = N_GROUPS * EXPERTS_PER_GROUP
D_EXPERT = 512
EPS = 1e-6
LAMBDA_INIT = 0.8 - 0.6 * math.exp(-0.3 * 0)
Q_SCALE = QK_HEAD_DIM ** -0.5 * math.log2(math.e)
N_SEG = 7
IN_WIDTH = N_SEG * D_MODEL
ROUTE_LANES = 128
MOE_ROWS = 304
EXPERT_LANE0 = N_GROUPS
SUBLANES = 8
LANES = 128
HEADS_PER_STEP = 8
MERGE_PART_ROWS = 256
VMEM_LIMIT = 56 * 1024 * 1024
NEG = float(jnp.finfo(jnp.float32).min)


def _params(*sem):
    return pltpu.CompilerParams(dimension_semantics=sem, vmem_limit_bytes=VMEM_LIMIT)


def _rms(x, g):
    return x * lax.rsqrt(jnp.mean(x * x, axis=-1, keepdims=True) + EPS) * g


def _sigmoid(x):
    return 0.5 * jnp.tanh(0.5 * x) + 0.5


def _in_proj_kernel(x_ref, g_ref, w_ref, q_ref, k_ref, v_ref, kb_ref, vb_ref, xr_ref, gr_ref, gt_ref,
                    *, k_transposed):
    xn = _rms(x_ref[...], g_ref[...]).astype(BF16)

    def seg(j):
        return jnp.dot(xn, w_ref[:, j * D_MODEL:(j + 1) * D_MODEL], preferred_element_type=F32)

    q_ref[...] = (seg(0) * Q_SCALE).astype(BF16)
    if k_transposed:
        kt = seg(1).T
        k_ref[0] = kt
        kb_ref[0, 0] = kt.astype(BF16)
    else:
        k = seg(1)
        k_ref[...] = k
        kb_ref[...] = k.astype(BF16)
    v = seg(2)
    v_ref[...] = v
    vb_ref[...] = v.astype(BF16)
    xr_ref[...] = seg(3)
    gr_ref[...] = seg(4)
    gt_ref[:, :D_MODEL] = seg(5)
    gt_ref[:, D_MODEL:] = seg(6)


def _in_proj(x2d, g1, w_in_b, tm, seq=None):
    n = x2d.shape[0]
    row = lambda i: (i, 0)
    const = lambda i: (0, 0)
    wide = lambda dt: jax.ShapeDtypeStruct((n, D_MODEL), dt)
    wide_spec = pl.BlockSpec((tm, D_MODEL), row)
    if seq is None:
        k_shape, k_spec, kb_shape, kb_spec = wide(F32), wide_spec, wide(BF16), wide_spec
    else:
        nt = seq // tm
        k_shape = jax.ShapeDtypeStruct((n // seq, D_MODEL, seq), F32)
        k_spec = pl.BlockSpec((1, D_MODEL, tm), lambda i: (i // nt, 0, i % nt))
        kb_shape = jax.ShapeDtypeStruct((n // seq, nt, D_MODEL, tm), BF16)
        kb_spec = pl.BlockSpec((1, 1, D_MODEL, tm), lambda i: (i // nt, i % nt, 0, 0))
    return pl.pallas_call(
        functools.partial(_in_proj_kernel, k_transposed=seq is not None),
        grid=(n // tm,),
        in_specs=[wide_spec,
                  pl.BlockSpec((1, D_MODEL), const),
                  pl.BlockSpec((D_MODEL, IN_WIDTH), const, pipeline_mode=pl.Buffered(1))],
        out_specs=[wide_spec, k_spec, wide_spec, kb_spec, wide_spec, wide_spec, wide_spec,
                   pl.BlockSpec((tm, 2 * D_MODEL), row)],
        out_shape=[wide(BF16), k_shape, wide(F32), kb_shape, wide(BF16), wide(F32), wide(F32),
                   jax.ShapeDtypeStruct((n, 2 * D_MODEL), F32)],
        compiler_params=_params("parallel"),
    )(x2d, g1, w_in_b)


def _lam(lq1, lk1, lq2, lk2):
    return (jnp.exp(jnp.sum(lq1[...] * lk1[...], axis=-1, keepdims=True))
            - jnp.exp(jnp.sum(lq2[...] * lk2[...], axis=-1, keepdims=True)) + LAMBDA_INIT)


def _split_q(q):
    lane = lax.broadcasted_iota(jnp.int32, q.shape, 1)
    zero = jnp.zeros_like(q)
    return jnp.where(lane < QK_HEAD_DIM, q, zero), jnp.where(lane >= QK_HEAD_DIM, q, zero)


def _scores(qm, k):
    return lax.dot_general(qm, k, (((1,), (1,)), ((), ())), preferred_element_type=F32)


def _subln(o, g):
    return (_rms(o, g) * (1.0 - LAMBDA_INIT)).astype(BF16)


def _attn_prompt_kernel(lq1, lk1, lq2, lk2, sg_ref, q_ref, k_ref, v_ref, o_ref, m_ref, acc_ref, *, tq, kw):
    i = pl.program_id(2)
    ones = jnp.ones((tq, LANES), BF16)
    qq = []
    for h in range(HEADS_PER_STEP):
        q1, q2 = _split_q(q_ref[:, h * HEAD_COLS:(h + 1) * HEAD_COLS])
        qq.append(jnp.concatenate([q1, q2], axis=0))

    def row_max(chunks):
        part = chunks[0]
        for x in chunks[1:]:
            part = jnp.maximum(part, x)
        return jnp.max(part, axis=1, keepdims=True)

    def split(s):
        return [s[:, c * LANES:(c + 1) * LANES] for c in range(s.shape[1] // LANES)]

    def head_cols(h, width):
        return slice(h * width, (h + 1) * width)

    def diagonal_tile(h):
        row = lax.broadcasted_iota(jnp.int32, (2 * kw, LANES), 0)
        lane = lax.broadcasted_iota(jnp.int32, (2 * kw, LANES), 1)
        qchunk = (row % kw) // CHUNK
        for r in range(tq // kw):
            rows1, rows2 = slice(r * kw, (r + 1) * kw), slice(tq + r * kw, tq + (r + 1) * kw)
            qr = jnp.concatenate([qq[h][rows1], qq[h][rows2]], axis=0)
            chunks = []
            for j in range(r + 1):
                s = split(jnp.dot(qr, k_ref[0, i * (tq // kw) + j, head_cols(h, HEAD_COLS), :],
                                  preferred_element_type=F32))
                if j == r:
                    s = [jnp.where((lane + c * LANES) // CHUNK <= qchunk, x, NEG) for c, x in enumerate(s)]
                chunks += s
            m_new = jnp.broadcast_to(row_max(chunks), (2 * kw, LANES))
            p = jnp.concatenate([jnp.exp2(x - m_new).astype(BF16) for x in chunks], axis=1)
            keys = (r + 1) * kw
            start = pl.multiple_of(i * tq, tq)
            v1 = jnp.concatenate([v_ref[pl.ds(start, keys), head_cols(h, V_HEAD_DIM)],
                                  jnp.ones((keys, LANES), BF16)], axis=1)
            pv = jnp.dot(p, v1, preferred_element_type=F32)
            for rows, part in ((rows1, slice(0, kw)), (rows2, slice(kw, 2 * kw))):
                acc_ref[h, rows, :] = pv[part]
                m_ref[h, rows, :] = m_new[part]

    def tile(kt, h):
        start = pl.multiple_of(kt * tq, tq)
        chunks = []
        for j in range(tq // kw):
            chunks += split(jnp.dot(qq[h], k_ref[0, kt * (tq // kw) + j, head_cols(h, HEAD_COLS), :],
                                    preferred_element_type=F32))
        m_next = jnp.maximum(m_ref[h], row_max(chunks))
        p = jnp.concatenate([jnp.exp2(x - m_next).astype(BF16) for x in chunks], axis=1)
        v1 = jnp.concatenate([v_ref[pl.ds(start, tq), head_cols(h, V_HEAD_DIM)], ones], axis=1)
        alpha = jnp.exp2(m_ref[h] - m_next)
        acc_ref[h] = (jnp.concatenate([alpha, alpha], axis=1) * acc_ref[h]
                      + jnp.dot(p, v1, preferred_element_type=F32))
        m_ref[h] = m_next

    def body(kt, c):
        for h in range(HEADS_PER_STEP):
            tile(kt, h)
        return c

    for h in range(HEADS_PER_STEP):
        diagonal_tile(h)
    lax.fori_loop(0, i, body, 0)
    lam = _lam(lq1, lk1, lq2, lk2)
    for h in range(HEADS_PER_STEP):
        o1 = acc_ref[h, :tq, :V_HEAD_DIM] / acc_ref[h, :tq, V_HEAD_DIM:]
        o2 = acc_ref[h, tq:, :V_HEAD_DIM] / acc_ref[h, tq:, V_HEAD_DIM:]
        o_ref[:, h * V_HEAD_DIM:(h + 1) * V_HEAD_DIM] = _subln(o1 - lam * o2, sg_ref[...])


def _attn_prompt(q, kbt, vb, lams, subln_g, batch, seq, tq):
    n = q.shape[0]
    nq = seq // tq
    nkb, kw = kbt.shape[1], kbt.shape[3]
    small = lambda b, h, i: (0, 0)
    qmap = lambda b, h, i: (b * nq + i, h)
    kvmap = lambda b, h, i: (b, h)
    return pl.pallas_call(
        functools.partial(_attn_prompt_kernel, tq=tq, kw=kw),
        grid=(batch, N_HEADS // HEADS_PER_STEP, nq),
        in_specs=[pl.BlockSpec((1, QK_HEAD_DIM), small)] * 4 + [
            pl.BlockSpec((1, V_HEAD_DIM), small),
            pl.BlockSpec((tq, HEADS_PER_STEP * HEAD_COLS), qmap),
            pl.BlockSpec((1, nkb, HEADS_PER_STEP * HEAD_COLS, kw), lambda b, h, i: (b, 0, h, 0)),
            pl.BlockSpec((seq, HEADS_PER_STEP * V_HEAD_DIM), kvmap)],
        out_specs=pl.BlockSpec((tq, HEADS_PER_STEP * V_HEAD_DIM), qmap),
        out_shape=jax.ShapeDtypeStruct((n, N_HEADS * V_HEAD_DIM), BF16),
        scratch_shapes=[pltpu.VMEM((HEADS_PER_STEP, 2 * tq, LANES), F32),
                        pltpu.VMEM((HEADS_PER_STEP, 2 * tq, 2 * LANES), F32)],
        compiler_params=_params("parallel", "parallel", "arbitrary"),
    )(*lams, subln_g, q, kbt, vb)


def _attn_sample_kernel(lq1, lk1, lq2, lk2, sg_ref, q_ref, kn_ref, vn_ref, ck_ref, cv_ref, o_ref, *, past):
    lam = _lam(lq1, lk1, lq2, lk2)
    for h in range(N_HEADS):
        cols = slice(h * HEAD_COLS, (h + 1) * HEAD_COLS)
        qq = jnp.concatenate(_split_q(q_ref[:, cols]), axis=0)
        kct = ck_ref[0, h * HEAD_COLS:(h + 1) * HEAD_COLS, :].astype(BF16)
        vc = cv_ref[0, pl.ds(h, past, stride=N_HEADS), :].astype(BF16)
        sc = jnp.dot(qq, kct, preferred_element_type=F32)
        sn = _scores(qq, kn_ref[:, cols])
        m = jnp.maximum(jnp.max(sc, axis=-1, keepdims=True), jnp.max(sn, axis=-1, keepdims=True))
        pc = jnp.exp2(sc - m)
        pn = jnp.exp2(sn - m)
        l = jnp.sum(pc, axis=-1, keepdims=True) + jnp.sum(pn, axis=-1, keepdims=True)
        acc = (jnp.dot(pc.astype(BF16), vc, preferred_element_type=F32)
               + jnp.dot(pn.astype(BF16), vn_ref[:, cols], preferred_element_type=F32))
        o = acc / l
        t = q_ref.shape[0]
        o_ref[:, cols] = _subln(o[:t] - lam * o[t:], sg_ref[...])


def _attn_sample(q, kb, vb, cache_kt, cache_v, lams, subln_g, batch, t):
    past = cache_kt.shape[2]
    small = lambda b: (0, 0)
    row = lambda b: (b, 0)
    cache = lambda b: (b, 0, 0)
    return pl.pallas_call(
        functools.partial(_attn_sample_kernel, past=past),
        grid=(batch,),
        in_specs=[pl.BlockSpec((1, QK_HEAD_DIM), small)] * 4 + [
            pl.BlockSpec((1, V_HEAD_DIM), small),
            pl.BlockSpec((t, D_MODEL), row),
            pl.BlockSpec((t, D_MODEL), row),
            pl.BlockSpec((t, D_MODEL), row),
            pl.BlockSpec((1, D_MODEL, past), cache),
            pl.BlockSpec((1, past * N_HEADS, V_HEAD_DIM), cache)],
        out_specs=pl.BlockSpec((t, D_MODEL), row),
        out_shape=jax.ShapeDtypeStruct((batch * t, D_MODEL), BF16),
        compiler_params=_params("parallel"),
    )(*lams, subln_g, q, kb, vb, cache_kt, cache_v)


def _rglru_kernel(xr_ref, gr_ref, cs_ref, h0_ref, cw_ref, cb_ref, wa_ref, ba_ref, wi_ref, bi_ref, lam_ref,
                  y_ref, nc_ref, hl_ref, xpad, hcar, a_s, b_s, h_s, *, tt):
    t = pl.program_id(1)
    pad = SUBLANES

    @pl.when(t == 0)
    def _():
        xpad[0:pad, :] = jnp.zeros((pad, D_RNN), F32)
        xpad[pad - (CONV_WIDTH - 1):pad, :] = cs_ref[0]
        hcar[...] = h0_ref[0]

    xpad[pad:pad + tt, :] = xr_ref[...]
    xc = cb_ref[...]
    for j in range(CONV_WIDTH):
        off = pad - (CONV_WIDTH - 1) + j
        xc = xc + xpad[off:off + tt, :] * cw_ref[j:j + 1, :]
    xcb = xc.astype(BF16)

    z = -lam_ref[...]
    softplus = jnp.maximum(z, 0.0) + jnp.log1p(jnp.exp(-jnp.abs(z)))
    for n in range(N_RNN_BLOCKS):
        cols = slice(n * RNN_BLOCK, (n + 1) * RNN_BLOCK)
        xs = xcb[:, cols]
        r = _sigmoid(jnp.dot(xs, wa_ref[n], preferred_element_type=F32) + ba_ref[:, cols])
        i = _sigmoid(jnp.dot(xs, wi_ref[n], preferred_element_type=F32) + bi_ref[:, cols])
        log_a = -RG_C * r * softplus[:, cols]
        a_s[:, cols] = jnp.exp(log_a)
        th = jnp.tanh(log_a)
        b_s[:, cols] = jnp.sqrt(-2.0 * th / (1.0 - th)) * (i * xc[:, cols])

    h = hcar[...]
    for s in range(tt):
        h = a_s[s:s + 1, :] * h + b_s[s:s + 1, :]
        h_s[s:s + 1, :] = h
    hcar[...] = h
    y_ref[...] = (h_s[...] * jax.nn.gelu(gr_ref[...])).astype(BF16)
    nc_ref[0] = xpad[pad + tt - (CONV_WIDTH - 1):pad + tt, :]
    hl_ref[0] = h
    xpad[0:pad, :] = xpad[tt:tt + pad, :]


def _rglru(xr, gr, conv_state, h0, conv_w, conv_b, w_a_b, b_a, w_i_b, b_i, rg_lambda, batch, t, tt):
    nt = t // tt
    row = lambda b, s: (b * nt + s, 0)
    const2 = lambda b, s: (0, 0)
    const3 = lambda b, s: (0, 0, 0)
    per_b = lambda b, s: (b, 0, 0)
    vec = pl.BlockSpec((1, D_RNN), const2)
    blk = pl.BlockSpec((N_RNN_BLOCKS, RNN_BLOCK, RNN_BLOCK), const3)
    big = pltpu.VMEM((tt, D_RNN), F32)
    return pl.pallas_call(
        functools.partial(_rglru_kernel, tt=tt),
        grid=(batch, nt),
        in_specs=[pl.BlockSpec((tt, D_RNN), row), pl.BlockSpec((tt, D_RNN), row),
                  pl.BlockSpec((1, CONV_WIDTH - 1, D_RNN), per_b), pl.BlockSpec((1, 1, D_RNN), per_b),
                  pl.BlockSpec((CONV_WIDTH, D_RNN), const2), vec, blk, vec, blk, vec, vec],
        out_specs=[pl.BlockSpec((tt, D_RNN), row),
                   pl.BlockSpec((1, CONV_WIDTH - 1, D_RNN), per_b),
                   pl.BlockSpec((1, 1, D_RNN), per_b)],
        out_shape=[jax.ShapeDtypeStruct((batch * t, D_RNN), BF16),
                   jax.ShapeDtypeStruct((batch, CONV_WIDTH - 1, D_RNN), F32),
                   jax.ShapeDtypeStruct((batch, 1, D_RNN), F32)],
        scratch_shapes=[pltpu.VMEM((tt + SUBLANES, D_RNN), F32), pltpu.VMEM((1, D_RNN), F32), big, big, big],
        compiler_params=_params("parallel", "arbitrary"),
    )(xr, gr, conv_state, h0, conv_w, conv_b, w_a_b, b_a, w_i_b, b_i, rg_lambda)


def _first_lane_of_max(vals, valid, lane):
    masked = jnp.where(valid, vals, -jnp.inf)
    mx = jnp.max(masked, axis=-1, keepdims=True)
    idx = jnp.min(jnp.where(valid & (masked == mx), lane, ROUTE_LANES), axis=-1, keepdims=True)
    return mx, idx


def _route(logits):
    lane = lax.broadcasted_iota(jnp.int32, logits.shape, 1)
    is_group = lane < N_GROUPS
    gmax, gsel = _first_lane_of_max(logits, is_group, lane)
    gsum = jnp.sum(jnp.where(is_group, jnp.exp(logits - gmax), 0.0), axis=-1, keepdims=True)
    g_w = 1.0 / gsum
    expert = lane - EXPERT_LANE0
    in_group = (expert >= 0) & (expert < N_EXPERTS) & ((expert // EXPERTS_PER_GROUP) == gsel)
    v1, i1 = _first_lane_of_max(logits, in_group, lane)
    v2, i2 = _first_lane_of_max(logits, in_group & (lane != i1), lane)
    e2 = jnp.exp(v2 - v1)
    den = 1.0 + e2
    return jnp.where(lane == i1, (1.0 / den) * g_w, 0.0) + jnp.where(lane == i2, (e2 / den) * g_w, 0.0)


def _split_weights(comb):
    hi = comb.astype(BF16)
    rest = comb - hi.astype(F32)
    mid = rest.astype(BF16)
    lo = (rest - mid.astype(F32)).astype(BF16)
    src = lax.broadcasted_iota(jnp.int32, (ROUTE_LANES, ROUTE_LANES), 0)
    dst = lax.broadcasted_iota(jnp.int32, (ROUTE_LANES, ROUTE_LANES), 1)
    out = jnp.zeros(comb.shape, F32)
    for piece, base in ((hi, 0), (mid, N_EXPERTS), (lo, 2 * N_EXPERTS)):
        inside = jnp.where(dst < base + N_EXPERTS, dst - base, -1)
        move = jnp.where(src == inside + EXPERT_LANE0, 1.0, 0.0)
        move = jnp.where(inside >= 0, move, 0.0).astype(BF16)
        out = out + jnp.dot(piece, move, preferred_element_type=F32)
    return out.astype(BF16)


def _merge_kernel(at_ref, y_ref, gt_ref, x_ref, wa_ref, wr_ref, wo_ref, bg_ref, g2_ref, wc_ref, bc_ref, *refs,
                  parts, n_cast):
    x1_ref, xn_ref, cm_ref, c3_ref = refs[n_cast:n_cast + 4]
    for src, dst in zip(refs[:n_cast], refs[n_cast + 4:]):
        dst[...] = src[...].astype(BF16)
    rows_per_part = x_ref.shape[0] // parts
    for part in range(parts):
        rows = slice(part * rows_per_part, (part + 1) * rows_per_part)
        pa = jnp.dot(at_ref[rows, :], wa_ref[...], preferred_element_type=F32)
        pr = jnp.dot(y_ref[rows, :], wr_ref[...], preferred_element_type=F32)
        g = _sigmoid(gt_ref[rows, :] + bg_ref[...])
        merged = g[:, :D_MODEL] * pa + g[:, D_MODEL:] * pr
        x1 = x_ref[rows, :] + jnp.dot(merged.astype(BF16), wo_ref[...], preferred_element_type=F32)
        x1_ref[rows, :] = x1
        xn = _rms(x1, g2_ref[...]).astype(BF16)
        xn_ref[rows, :] = xn
        comb = _route(jnp.dot(xn, wc_ref[...], preferred_element_type=F32) + bc_ref[...])
        cm_ref[rows, :] = comb
        c3_ref[rows, :] = _split_weights(comb)


def _merge(attn, y_rnn, gates, x2d, wa_b, wr_b, wo_b, b_gate, g2, w_cat_b, b_cat, tm, cast=()):
    n = x2d.shape[0]
    steps = n // tm
    row = lambda i: (i, 0)
    const = lambda i: (0, 0)
    sq = pl.BlockSpec((D_MODEL, D_MODEL), const)
    slabs = [a.reshape(steps, -1, a.shape[-1]) for a in cast]
    slab_specs = [pl.BlockSpec((1,) + s.shape[1:], lambda i: (i, 0, 0)) for s in slabs]
    outs = pl.pallas_call(
        functools.partial(_merge_kernel, parts=max(1, tm // MERGE_PART_ROWS), n_cast=len(cast)),
        grid=(steps,),
        in_specs=[pl.BlockSpec((tm, D_MODEL), row), pl.BlockSpec((tm, D_MODEL), row),
                  pl.BlockSpec((tm, 2 * D_MODEL), row), pl.BlockSpec((tm, D_MODEL), row),
                  sq, sq, sq, pl.BlockSpec((1, 2 * D_MODEL), const), pl.BlockSpec((1, D_MODEL), const),
                  pl.BlockSpec((D_MODEL, ROUTE_LANES), const), pl.BlockSpec((1, ROUTE_LANES), const)] + slab_specs,
        out_specs=[pl.BlockSpec((tm, D_MODEL), row), pl.BlockSpec((tm, D_MODEL), row),
                   pl.BlockSpec((tm, ROUTE_LANES), row), pl.BlockSpec((tm, ROUTE_LANES), row)] + slab_specs,
        out_shape=[jax.ShapeDtypeStruct((n, D_MODEL), F32), jax.ShapeDtypeStruct((n, D_MODEL), BF16),
                   jax.ShapeDtypeStruct((n, ROUTE_LANES), F32), jax.ShapeDtypeStruct((n, ROUTE_LANES), BF16)]
        + [jax.ShapeDtypeStruct(s.shape, BF16) for s in slabs],
        compiler_params=_params("parallel"),
    )(attn, y_rnn, gates, x2d, wa_b, wr_b, wo_b, b_gate, g2, w_cat_b, b_cat, *slabs)
    return outs[0], outs[1], outs[2], outs[3], tuple(o.reshape(a.shape) for o, a in zip(outs[4:], cast))


def _route_tables(comb, tile, batch):
    n = comb.shape[0]
    seq, chunk = n // batch, tile // batch
    nt = seq // chunk
    c = comb[:, EXPERT_LANE0:EXPERT_LANE0 + N_EXPERTS]
    sel = jnp.any(c.reshape(batch, nt, chunk, N_GROUPS, EXPERTS_PER_GROUP) != 0.0, axis=-1)
    sel = sel.transpose(1, 0, 2, 3).reshape(nt, tile, N_GROUPS)
    lower = (jnp.arange(tile)[:, None] >= jnp.arange(tile)[None, :]).astype(BF16)
    csum = jnp.einsum("ts,nsg->ntg", lower, sel.astype(BF16), preferred_element_type=F32).astype(jnp.int32)
    rank = jnp.where(sel, csum - 1, -1).astype(F32)
    counts = csum[:, -1, :].reshape(nt * N_GROUPS)
    rank_t = rank.transpose(0, 2, 1).reshape(nt * N_GROUPS, 1, tile)
    rank = rank.reshape(nt, batch, chunk, N_GROUPS).transpose(1, 0, 2, 3).reshape(batch, seq, N_GROUPS)
    rank = jnp.pad(rank, ((0, 0), (0, 0), (0, ROUTE_LANES - N_GROUPS)), constant_values=-1.0)
    return counts, rank, rank_t


def _moe_kernel(cnt_ref, x_ref, c3_ref, rk_ref, rkt_ref, x1_ref, gf_ref, *refs, tile):
    flat = lambda ref: ref[...].reshape(tile, ref.shape[-1])
    w1_refs, w3_refs, w2_refs = (refs[k * EXPERTS_PER_GROUP:(k + 1) * EXPERTS_PER_GROUP] for k in range(3))
    y_ref = refs[3 * EXPERTS_PER_GROUP]
    t = pl.program_id(0)
    g = pl.program_id(1)
    count = cnt_ref[t * N_GROUPS + g]

    @pl.when(g == 0)
    def _():
        y_ref[...] = jnp.zeros(y_ref.shape, F32)

    rank_row = rkt_ref[0]
    lane = lax.broadcasted_iota(jnp.int32, (tile, ROUTE_LANES), 1)
    rank_col = jnp.sum(jnp.where(lane == g, flat(rk_ref), 0.0), axis=1, keepdims=True)
    lane_r = lax.broadcasted_iota(jnp.int32, (MOE_ROWS, ROUTE_LANES), 1)

    def block(b, carry):
        base = (b * MOE_ROWS).astype(F32)
        slot = lax.broadcasted_iota(jnp.int32, (MOE_ROWS, tile), 0).astype(F32) + base
        onehot = jnp.where(rank_row == slot, 1.0, 0.0).astype(BF16)
        xc = jnp.dot(onehot, flat(x_ref), preferred_element_type=F32).astype(BF16)
        c3 = jnp.dot(onehot, flat(c3_ref), preferred_element_type=F32)
        out = jnp.zeros((MOE_ROWS, D_MODEL), F32)
        for j in range(EXPERTS_PER_GROUP):
            mine = (lane_r % N_EXPERTS == g * EXPERTS_PER_GROUP + j) & (lane_r < 3 * N_EXPERTS)
            cw = jnp.sum(jnp.where(mine, c3, 0.0), axis=1, keepdims=True)
            a = jnp.dot(xc, w1_refs[j][0], preferred_element_type=F32)
            h = (a * _sigmoid(a)) * jnp.dot(xc, w3_refs[j][0], preferred_element_type=F32)
            out = out + cw * jnp.dot(h.astype(BF16), w2_refs[j][0], preferred_element_type=F32)
        slot_t = lax.broadcasted_iota(jnp.int32, (tile, MOE_ROWS), 1).astype(F32) + base
        onehot_t = jnp.where(rank_col == slot_t, 1.0, 0.0).astype(BF16)
        y_ref[...] += jnp.dot(onehot_t, out.astype(BF16), preferred_element_type=F32).reshape(y_ref.shape)
        return carry

    lax.fori_loop(0, (count + MOE_ROWS - 1) // MOE_ROWS, block, 0)

    @pl.when(g == N_GROUPS - 1)
    def _():
        y_ref[...] = _rms(x1_ref[...] + y_ref[...], gf_ref[...])


def _moe(xn2, comb, c3, x1, gf, w1_b, w3_b, w2_b, tile, batch):
    n = xn2.shape[0]
    seq, chunk = n // batch, tile // batch
    counts, rank, rank_t = _route_tables(comb, tile, batch)
    row = lambda t, g, cnt: (0, t, 0)
    rows = lambda width: pl.BlockSpec((batch, chunk, width), row)
    expert_specs = lambda shape: [pl.BlockSpec((1,) + shape, lambda t, g, cnt, j=j: (g * EXPERTS_PER_GROUP + j, 0, 0))
                                  for j in range(EXPERTS_PER_GROUP)]
    return pl.pallas_call(
        functools.partial(_moe_kernel, tile=tile),
        grid_spec=pltpu.PrefetchScalarGridSpec(
            num_scalar_prefetch=1,
            grid=(n // tile, N_GROUPS),
            in_specs=[rows(D_MODEL), rows(ROUTE_LANES), rows(ROUTE_LANES),
                      pl.BlockSpec((1, 1, tile), lambda t, g, cnt: (t * N_GROUPS + g, 0, 0)),
                      rows(D_MODEL), pl.BlockSpec((1, D_MODEL), lambda t, g, cnt: (0, 0))]
            + expert_specs((D_MODEL, D_EXPERT)) + expert_specs((D_MODEL, D_EXPERT))
            + expert_specs((D_EXPERT, D_MODEL)),
            out_specs=rows(D_MODEL)),
        out_shape=jax.ShapeDtypeStruct((batch, seq, D_MODEL), F32),
        compiler_params=_params("parallel", "arbitrary"),
    )(counts, xn2.reshape(batch, seq, D_MODEL), c3.reshape(batch, seq, ROUTE_LANES), rank, rank_t,
      x1.reshape(batch, seq, D_MODEL), gf,
      *([w1_b] * EXPERTS_PER_GROUP + [w3_b] * EXPERTS_PER_GROUP + [w2_b] * EXPERTS_PER_GROUP))


def _tiles(batch, t):
    n = batch * t
    return dict(tm_in=min(256, n),
                tm_merge=min(2 * MERGE_PART_ROWS, n),
                tm_moe=min(1024, n),
                tq=min(512, t),
                tt=min(256, t))


def _stream(x, past, w, experts):
    batch, t, _ = x.shape
    n = batch * t
    tm_in, tm_merge, tm_moe, tq, tt = (_tiles(batch, t)[k] for k in ("tm_in", "tm_merge", "tm_moe", "tq", "tt"))
    x2d = x.reshape(n, D_MODEL)
    q, k, v, kb, vb, xr, gr, gates = _in_proj(x2d, w["g1"], w["w_in"], tm_in,
                                              seq=t if past is None else None)
    if past is None:
        attn = _attn_prompt(q, kb, vb, w["lams"], w["subln_g"], batch, t, tq)
        k = k.reshape(batch, N_HEADS, 2, QK_HEAD_DIM, t).transpose(0, 4, 1, 2, 3)
        conv_state = jnp.zeros((batch, CONV_WIDTH - 1, D_RNN), F32)
        h0 = jnp.zeros((batch, 1, D_RNN), F32)
    else:
        cache_k, cache_v, conv_state, h0 = past
        attn = _attn_sample(q, kb, vb, cache_k, cache_v, w["lams"], w["subln_g"], batch, t)
    y_rnn, new_conv, h_last = _rglru(xr, gr, conv_state, h0, w["conv_w"], w["conv_b"], w["w_a"], w["b_a"],
                                     w["w_i"], w["b_i"], w["rg_lambda"], batch, t, tt)
    to_cast = experts if experts[0].dtype != BF16 else ()
    x1, xn2, comb, c3, cast = _merge(attn, y_rnn, gates, x2d, w["w_attn"], w["w_rnn"], w["w_out"], w["b_gate"],
                                 w["g2"], w["w_cat"], w["b_cat"], tm_merge, cast=to_cast)
    experts = cast or experts
    y = _moe(xn2, comb, c3, x1, w["gf"], *experts, tm_moe, batch)
    return (y,
            k.reshape(1, batch, t, N_HEADS, 2, QK_HEAD_DIM),
            v.reshape(1, batch, t, N_HEADS, V_HEAD_DIM),
            new_conv.reshape(1, batch, CONV_WIDTH - 1, D_RNN),
            h_last.reshape(1, batch, D_RNN)), experts


def kernel(x_prompt, x_sample, cache_k, cache_v, state_conv, state_rnn, norm1_g, w_in, lambda_q1, lambda_k1, lambda_q2, lambda_k2, subln_g, w_attn_proj, conv_w, conv_b, w_rg_a, b_rg_a, w_rg_i, b_rg_i, rg_lambda, w_rnn_proj, b_gate, w_out, norm2_g, w_group, b_group, w_router, b_router, w1, w3, w2, final_norm_g):
    assert norm1_g.shape[0] == 1, "single-layer model"
    dec_batch, past_len = cache_k.shape[1], cache_k.shape[2]
    pad = ROUTE_LANES - N_GROUPS - N_EXPERTS
    w = {
        "g1": norm1_g, "g2": norm2_g, "gf": final_norm_g.reshape(1, D_MODEL),
        "w_in": w_in[0].astype(BF16),
        "lams": (lambda_q1, lambda_k1, lambda_q2, lambda_k2),
        "subln_g": subln_g,
        "w_attn": w_attn_proj[0].astype(BF16), "w_rnn": w_rnn_proj[0].astype(BF16),
        "w_out": w_out[0].astype(BF16),
        "conv_w": conv_w[0], "conv_b": conv_b,
        "w_a": w_rg_a[0].astype(BF16), "b_a": b_rg_a, "w_i": w_rg_i[0].astype(BF16), "b_i": b_rg_i,
        "rg_lambda": rg_lambda, "b_gate": b_gate,
        "w_cat": jnp.pad(jnp.concatenate([w_group[0], w_router[0]], axis=1), ((0, 0), (0, pad))).astype(BF16),
        "b_cat": jnp.pad(jnp.concatenate([b_group[0], b_router[0]]), (0, pad)).reshape(1, ROUTE_LANES),
    }
    (yp, kp, vp, cp, hp), experts_b = _stream(x_prompt, None, w, (w1[0], w3[0], w2[0]))
    past = (cache_k[0].transpose(0, 2, 3, 4, 1).reshape(dec_batch, D_MODEL, past_len),
            cache_v[0].reshape(dec_batch, past_len * N_HEADS, V_HEAD_DIM),
            state_conv[0], state_rnn[0].reshape(dec_batch, 1, D_RNN))
    x_sample, yp = lax.optimization_barrier((x_sample, yp))
    (ys, ks, vs, cs, hs), _ = _stream(x_sample, past, w, experts_b)
    return (yp, ys, kp, vp, cp, hp, ks, vs, cs, hs)
```
